```python
import jax
import jax.numpy as jnp
from jax import lax
import numpy as np

D_MODEL = 1024
BATCH = 8
SEQ = 4096
DEPTH = 2

GRID_W = 64
CTX_LEN = 256
N_RET_HEADS = 4
RET_HEAD_DIM = 128
N_GLA_HEADS = 4
GLA_KEY_DIM = 64
GLA_VAL_DIM = 128
GLA_GATE_RANK = 16
GLA_GATE_NORM = 16.0
D_FF = 4 * D_MODEL
CHUNK = 64
ROPE_BASE = 10000.0
LN2 = 0.6931471805599453
EPS = 1e-6

RET_W = N_RET_HEADS * RET_HEAD_DIM
GLA_K = N_GLA_HEADS * GLA_KEY_DIM
GLA_V = N_GLA_HEADS * GLA_VAL_DIM
MIX_W = RET_W + GLA_V
IN_SPLITS = (RET_W, RET_W, RET_W, RET_W, GLA_K, GLA_K, GLA_V, GLA_V, GLA_GATE_RANK, GLA_GATE_RANK)
IN_W = 4 * RET_W + 2 * GLA_K + 2 * GLA_V + 2 * GLA_GATE_RANK

kernel_name = 'hybrid_retention_gla_dit_block'


def rmsnorm(x, g):
    xf = x.astype(jnp.float32)
    y = xf * lax.rsqrt(jnp.mean(xf * xf, axis=-1, keepdims=True) + EPS)
    return (y * g.astype(jnp.float32)).astype(x.dtype)


def head_layernorm(o):
    of = o.astype(jnp.float32)
    mu = jnp.mean(of, axis=-1, keepdims=True)
    var = jnp.mean(jnp.square(of - mu), axis=-1, keepdims=True)
    return ((of - mu) * lax.rsqrt(var + EPS)).astype(o.dtype)


def modulate(h, shift, scale):
    return h * (1.0 + scale) + shift


def to_heads(t, n_heads):
    b, l, _ = t.shape
    return t.reshape(b, l, n_heads, -1).transpose(0, 2, 1, 3)


def from_heads(t):
    b, h, l, d = t.shape
    return t.transpose(0, 2, 1, 3).reshape(b, l, h * d)


def grid_positions(n_rows):
    row = jnp.repeat(jnp.arange(n_rows), GRID_W).astype(jnp.float32)
    col = jnp.tile(jnp.arange(GRID_W), n_rows).astype(jnp.float32)
    return row, col


def rotary_2d(t, row_pos, col_pos):
    d = t.shape[-1]
    half = d // 2
    nf = half // 2
    inv = ROPE_BASE ** (-jnp.arange(nf, dtype=jnp.float32) / nf)

    def rot(u, p):
        ang = p[:, None] * inv[None, :]
        cos, sin = jnp.cos(ang), jnp.sin(ang)
        u1, u2 = u[..., :nf], u[..., nf:]
        return jnp.concatenate([u1 * cos - u2 * sin, u1 * sin + u2 * cos], axis=-1)

    tf = t.astype(jnp.float32)
    out = jnp.concatenate([rot(tf[..., :half], row_pos), rot(tf[..., half:], col_pos)], axis=-1)
    return out.astype(t.dtype)


def chunked_gated_linear_attention(q, k, v, log_a, s0, strict):
    b_, h_, l_, _ = q.shape
    dv = v.shape[-1]
    n_chunks = l_ // CHUNK
    f32 = jnp.float32

    def blocks(t):
        return t.astype(f32).reshape(b_, h_, n_chunks, CHUNK, t.shape[-1]).transpose(2, 0, 1, 3, 4)

    qs, ks, vs, gs = blocks(q), blocks(k), blocks(v), blocks(log_a)
    mask = jnp.tril(jnp.ones((CHUNK, CHUNK), dtype=bool), -1 if strict else 0)

    def step(s, blk):
        qc, kc, vc, gc = blk
        cum = jnp.cumsum(gc, axis=-2)
        cum_last = cum[..., -1:, :]
        q_dec = qc * jnp.exp(cum)
        k_inv = kc * jnp.exp(-cum)
        k_end = kc * jnp.exp(cum_last - cum)
        scores = jnp.where(mask, jnp.einsum('bhtd,bhsd->bhts', q_dec, k_inv), 0.0)
        o = jnp.einsum('bhts,bhsv->bhtv', scores, vc) + jnp.einsum('bhtd,bhdv->bhtv', q_dec, s)
        s_new = jnp.exp(cum_last[..., 0, :])[..., :, None] * s + jnp.einsum('bhsd,bhsv->bhdv', k_end, vc)
        return s_new, o

    s_fin, o = lax.scan(step, s0.astype(f32), (qs, ks, vs, gs))
    o = o.transpose(1, 2, 0, 3, 4).reshape(b_, h_, l_, dv)
    return o.astype(v.dtype), s_fin


def bidirectional_scan(lat_in, ctx_in):
    q, k, v, a_f, a_b = lat_in
    cq, ck, cv, ca_f, ca_b = ctx_in
    b_, h_, _, dk = q.shape
    dv = v.shape[-1]
    zero = jnp.zeros((b_, h_, dk, dv), jnp.float32)

    def flip(t):
        return jnp.flip(t, axis=2)

    c_f, s_f = chunked_gated_linear_attention(cq, ck, cv, ca_f, zero, False)
    x_f, _ = chunked_gated_linear_attention(q, k, v, a_f, s_f, False)
    c_b, s_b = chunked_gated_linear_attention(flip(cq), flip(ck), flip(cv), flip(ca_b), zero, True)
    x_b, _ = chunked_gated_linear_attention(flip(q), flip(k), flip(v), flip(a_b), s_b, True)
    return x_f + flip(x_b), c_f + flip(c_b)


def project_in(h, w_in):
    offsets = np.cumsum(np.array(IN_SPLITS))[:-1].tolist()
    return jnp.split(h @ w_in, offsets, axis=-1)


def gla_log_gate(d, up, ub):
    return jax.nn.log_sigmoid((d @ up + ub).astype(jnp.float32)) / GLA_GATE_NORM


def hybrid_mixer(h, hc, w_in, ret_decay, gate_up, gate_b, gla_norm_g, row_pos, col_pos, need_ctx):
    lat = project_in(h, w_in)
    cx = project_in(hc, w_in)
    log_gamma = jnp.log1p(-jnp.exp(ret_decay.astype(jnp.float32)))

    def retention_inputs(parts, rotate):
        q = to_heads(parts[0], N_RET_HEADS) * (RET_HEAD_DIM ** -0.5)
        k = to_heads(parts[1], N_RET_HEADS)
        if rotate:
            q = rotary_2d(q, row_pos, col_pos)
            k = rotary_2d(k, row_pos, col_pos)
        v = to_heads(parts[2], N_RET_HEADS)
        shape = q.shape[:3] + (1,)
        a_f = jnp.broadcast_to(log_gamma[0][None, :, None, None], shape)
        a_b = jnp.broadcast_to(log_gamma[1][None, :, None, None], shape)
        return q, k, v, a_f, a_b

    def gla_inputs(parts):
        q = to_heads(parts[4], N_GLA_HEADS) * (GLA_KEY_DIM ** -0.5)
        k = to_heads(parts[5], N_GLA_HEADS)
        v = to_heads(parts[6], N_GLA_HEADS)
        a_f = to_heads(gla_log_gate(parts[8], gate_up[0], gate_b[0]), N_GLA_HEADS)
        a_b = to_heads(gla_log_gate(parts[9], gate_up[1], gate_b[1]), N_GLA_HEADS)
        return q, k, v, a_f, a_b

    r_lat, r_ctx = bidirectional_scan(retention_inputs(lat, True), retention_inputs(cx, False))
    g_lat, g_ctx = bidirectional_scan(gla_inputs(lat), gla_inputs(cx))

    def merge(r, g, parts):
        ret = from_heads(head_layernorm(r)) * jax.nn.silu(parts[3])
        gla = from_heads(rmsnorm(g, gla_norm_g)) * jax.nn.silu(parts[7])
        return jnp.concatenate([ret, gla], axis=-1)

    y_lat = merge(r_lat, g_lat, lat)
    y_ctx = merge(r_ctx, g_ctx, cx) if need_ctx else None
    return y_lat, y_ctx


def squared_relu_mlp(h, w1, w2):
    return jnp.square(jax.nn.relu(h @ w1)) @ w2


def setup_inputs(seed: int = 0) -> dict:
    key = jax.random.key(seed)
    ks = jax.random.split(key, 17)
    f32 = jnp.float32

    def nrm(k, shape, s):
        return jax.random.normal(k, shape, f32) * s

    heads = jnp.arange(N_RET_HEADS, dtype=f32)
    base = jnp.stack([-(5.0 + heads) * LN2, -(5.5 + heads) * LN2], axis=0)
    ret_decay = base[None] + nrm(ks[8], (DEPTH, 2, N_RET_HEADS), 0.01)
    return {
        'x': nrm(ks[0], (BATCH, SEQ, D_MODEL), 1.0),
        'c': nrm(ks[1], (BATCH, D_MODEL), 1.0),
        'ctx': nrm(ks[2], (BATCH, CTX_LEN, D_MODEL), 1.0),
        'c_ctx': nrm(ks[3], (D_MODEL,), 1.0),
        'ada_w': nrm(ks[4], (DEPTH, D_MODEL, 6 * D_MODEL), 0.5 * D_MODEL ** -0.5),
        'ada_b': nrm(ks[5], (DEPTH, 6 * D_MODEL), 0.02),
        'norm1_g': 1.0 + nrm(ks[6], (DEPTH, D_MODEL), 0.02),
        'w_in': nrm(ks[7], (DEPTH, D_MODEL, IN_W), D_MODEL ** -0.5),
        'ret_decay': ret_decay,
        'gla_gate_up': nrm(ks[9], (DEPTH, 2, GLA_GATE_RANK, GLA_K), GLA_GATE_RANK ** -0.5),
        'gla_gate_b': nrm(ks[10], (DEPTH, 2, GLA_K), 0.1),
        'gla_norm_g': 1.0 + nrm(ks[11], (DEPTH, GLA_VAL_DIM), 0.02),
        'w_out': nrm(ks[12], (DEPTH, MIX_W, D_MODEL), MIX_W ** -0.5),
        'norm2_g': 1.0 + nrm(ks[13], (DEPTH, D_MODEL), 0.02),
        'w_mlp1': nrm(ks[14], (DEPTH, D_MODEL, D_FF), D_MODEL ** -0.5),
        'w_mlp2': nrm(ks[15], (DEPTH, D_FF, D_MODEL), D_FF ** -0.5),
        'final_g': 1.0 + nrm(ks[16], (D_MODEL,), 0.02),
    }


def reference(x, c, ctx, c_ctx, ada_w, ada_b, norm1_g, w_in, ret_decay, gla_gate_up, gla_gate_b,
              gla_norm_g, w_out, norm2_g, w_mlp1, w_mlp2, final_g):
    n_rows = x.shape[1] // GRID_W
    row_pos, col_pos = grid_positions(n_rows)
    silu_c = jax.nn.silu(c)
    silu_cc = jax.nn.silu(c_ctx)
    for layer in range(DEPTH):
        need_ctx = layer < DEPTH - 1
        mod_x = silu_c @ ada_w[layer] + ada_b[layer]
        mod_c = silu_cc @ ada_w[layer] + ada_b[layer]
        sh1, sc1, g1, sh2, sc2, g2 = [m[:, None, :] for m in jnp.split(mod_x, 6, axis=-1)]
        csh1, csc1, cg1, csh2, csc2, cg2 = jnp.split(mod_c, 6, axis=-1)

        hx = modulate(rmsnorm(x, norm1_g[layer]), sh1, sc1)
        hc = modulate(rmsnorm(ctx, norm1_g[layer]), csh1, csc1)
        mx, mc = hybrid_mixer(hx, hc, w_in[layer], ret_decay[layer], gla_gate_up[layer], gla_gate_b[layer],
                              gla_norm_g[layer], row_pos, col_pos, need_ctx)
        x = x + g1 * (mx @ w_out[layer])
        hx = modulate(rmsnorm(x, norm2_g[layer]), sh2, sc2)
        x = x + g2 * squared_relu_mlp(hx, w_mlp1[layer], w_mlp2[layer])

        if need_ctx:
            ctx = ctx + cg1 * (mc @ w_out[layer])
            hc = modulate(rmsnorm(ctx, norm2_g[layer]), csh2, csc2)
            ctx = ctx + cg2 * squared_relu_mlp(hc, w_mlp1[layer], w_mlp2[layer])
    return rmsnorm(x, final_g)
```

```python
import functools
import math

import jax
import jax.numpy as jnp
from jax import lax
from jax.experimental import pallas as pl
from jax.experimental.pallas import tpu as pltpu

F32 = jnp.float32
BF16 = jnp.bfloat16

GRID_W = 64
N_RET_HEADS = 4
RET_HEAD_DIM = 128
N_GLA_HEADS = 4
GLA_KEY_DIM = 64
GLA_VAL_DIM = 128
GLA_GATE_RANK = 16
GLA_GATE_NORM = 16.0
ROPE_BASE = 10000.0
EPS = 1e-6

RET_W = N_RET_HEADS * RET_HEAD_DIM
GLA_K = N_GLA_HEADS * GLA_KEY_DIM
GLA_V = N_GLA_HEADS * GLA_VAL_DIM
MIX_W = RET_W + GLA_V
LANES = 128

OFF_RQ, OFF_RQS, OFF_RK, OFF_RKS, OFF_RV, OFF_RG, OFF_GG = (i * RET_W for i in range(7))
OFF_GQ = 7 * RET_W
OFF_GK = OFF_GQ + GLA_K
OFF_GV = OFF_GK + GLA_K
OFF_D = OFF_GV + GLA_V
PROJ_W = OFF_D + LANES

RET_BLOCK = 256
GLA_BLOCK = 128
GLA_HALF = GLA_BLOCK // 2

VMEM_LIMIT = 56 * 1024 * 1024


def _const_spec(shape):
    zeros = (0,) * len(shape)
    return pl.BlockSpec(shape, lambda *_: zeros, pipeline_mode=pl.Buffered(1))


def _mod_kernel(cs_ref, w_ref, b_ref, o_ref):
    cs = cs_ref[...]
    s = cs * jax.nn.sigmoid(cs)
    o_ref[0] = jnp.dot(s.astype(BF16), w_ref[0].astype(BF16), preferred_element_type=F32) + b_ref[0]


def _modulation(cs, ada_w, ada_b):
    depth, d, n = ada_w.shape
    rows = cs.shape[0]
    tn = 1024
    return pl.pallas_call(
        _mod_kernel,
        out_shape=jax.ShapeDtypeStruct((depth, rows, n), F32),
        grid=(depth, n // tn),
        in_specs=[
            pl.BlockSpec((rows, d), lambda l, j: (0, 0)),
            pl.BlockSpec((1, d, tn), lambda l, j: (l, 0, j)),
            pl.BlockSpec((1, 1, tn), lambda l, j: (l, 0, j)),
        ],
        out_specs=pl.BlockSpec((1, rows, tn), lambda l, j: (l, 0, j)),
        compiler_params=pltpu.CompilerParams(dimension_semantics=("parallel", "parallel"),
                                             vmem_limit_bytes=VMEM_LIMIT),
        name="adaln_modulation",
    )(cs, ada_w, ada_b.reshape(depth, 1, n))


def _rope_kernel(cq_ref, sq_ref, ck_ref, sk_ref):
    shape = cq_ref.shape
    t = lax.broadcasted_iota(jnp.int32, shape, 0)
    lane = lax.broadcasted_iota(jnp.int32, shape, 1)
    nf = RET_HEAD_DIM // 4
    freq = (lane & (nf - 1)).astype(F32)
    inv = jnp.exp(freq * (-math.log(ROPE_BASE) / nf))
    row = lax.shift_right_logical(t, GRID_W.bit_length() - 1)
    pos = jnp.where(lane < RET_HEAD_DIM // 2, row, t & (GRID_W - 1)).astype(F32)
    ang = pos * inv
    cos = jnp.cos(ang)
    sin = jnp.where((lane & (2 * nf - 1)) < nf, -jnp.sin(ang), jnp.sin(ang))
    scale = RET_HEAD_DIM ** -0.5
    cq_ref[...] = cos * scale
    sq_ref[...] = sin * scale
    ck_ref[...] = cos
    sk_ref[...] = sin


def _rope_tables(seq):
    sds = jax.ShapeDtypeStruct((seq, RET_HEAD_DIM), F32)
    return pl.pallas_call(_rope_kernel, out_shape=(sds, sds, sds, sds), name="rope_tables")()


def _proj_kernel(x_ref, mod_ref, n1_ref, w_ref, up_ref, ub_ref, cq_ref, sq_ref, ck_ref, sk_ref,
                 rq_ref, rk_ref, rv_ref, g_ref, gq_ref, gk_ref, gv_ref, ga_ref):
    x = x_ref[0]
    mod = mod_ref[0]
    ms = jnp.mean(x * x, axis=-1, keepdims=True)
    h = x * lax.rsqrt(ms + EPS) * n1_ref[...]
    h = h * (1.0 + mod[1:2]) + mod[0:1]
    hb = h.astype(BF16)

    def mm(off, n):
        return jnp.dot(hb, w_ref[:, off:off + n], preferred_element_type=F32)

    def rotary(z, zs, c_ref, s_ref, o_ref):
        c = c_ref[...]
        s = s_ref[...]
        for hd in range(N_RET_HEADS):
            sl = slice(hd * RET_HEAD_DIM, (hd + 1) * RET_HEAD_DIM)
            o_ref[0, :, sl] = (z[:, sl] * c + zs[:, sl] * s).astype(BF16)

    rotary(mm(OFF_RQ, RET_W), mm(OFF_RQS, RET_W), cq_ref, sq_ref, rq_ref)
    rotary(mm(OFF_RK, RET_W), mm(OFF_RKS, RET_W), ck_ref, sk_ref, rk_ref)
    rv_ref[0] = mm(OFF_RV, RET_W).astype(BF16)
    z = mm(OFF_RG, RET_W)
    g_ref[0, :, 0:RET_W] = (z * jax.nn.sigmoid(z)).astype(BF16)
    z = mm(OFF_GG, GLA_V)
    g_ref[0, :, RET_W:MIX_W] = (z * jax.nn.sigmoid(z)).astype(BF16)
    gq_ref[0] = (mm(OFF_GQ, GLA_K) * (GLA_KEY_DIM ** -0.5)).astype(BF16)
    gk_ref[0] = mm(OFF_GK, GLA_K).astype(BF16)
    gv_ref[0] = mm(OFF_GV, GLA_V).astype(BF16)
    d = mm(OFF_D, LANES).astype(BF16)
    zg = jnp.dot(d, up_ref[...], preferred_element_type=F32) + ub_ref[...]
    ga_ref[0] = (jnp.minimum(zg, 0.0) - jnp.log1p(jnp.exp(-jnp.abs(zg)))) * (1.0 / GLA_GATE_NORM)


def _project(xa, modl, mod_row, n1, w, up, ub, tables, tm):
    b, seq, d = xa.shape
    if mod_row is None:
        mod_map = lambda i, j: (i, 0, 0)
    else:
        mod_map = lambda i, j: (mod_row, 0, 0)
    tok = lambda n: pl.BlockSpec((1, tm, n), lambda i, j: (i, j, 0))
    tab = pl.BlockSpec((tm, RET_HEAD_DIM), lambda i, j: (j, 0))
    sd = lambda n, dt: jax.ShapeDtypeStruct((b, seq, n), dt)
    return pl.pallas_call(
        _proj_kernel,
        out_shape=(sd(RET_W, BF16), sd(RET_W, BF16), sd(RET_W, BF16), sd(MIX_W, BF16),
                   sd(GLA_K, BF16), sd(GLA_K, BF16), sd(GLA_V, BF16), sd(2 * GLA_K, F32)),
        grid=(b, seq // tm),
        in_specs=[tok(d), pl.BlockSpec((1, 6, d), mod_map), _const_spec((1, d)), _const_spec(w.shape),
                  _const_spec(up.shape), _const_spec(ub.shape), tab, tab, tab, tab],
        out_specs=(tok(RET_W), tok(RET_W), tok(RET_W), tok(MIX_W), tok(GLA_K), tok(GLA_K), tok(GLA_V),
                   tok(2 * GLA_K)),
        compiler_params=pltpu.CompilerParams(dimension_semantics=("parallel", "parallel"),
                                             vmem_limit_bytes=VMEM_LIMIT),
        name="norm_project",
    )(xa, modl, n1, w, up, ub, *tables)


def _ret_kernel(rd_ref, q_ref, k_ref, v_ref, g_ref, cq_ref, ck_ref, cv_ref, cg_ref,
                y_ref, cy_ref, rb_ref, crb_ref):
    t_blk = RET_BLOCK
    dh = RET_HEAD_DIM
    lg = jnp.log1p(-jnp.exp(rd_ref[0]))
    lgf, lgb = lg[0:1, :], lg[1:2, :]
    lgf_h, lgb_h = lgf[:, :dh], lgb[:, :dh]
    ti = lax.broadcasted_iota(jnp.int32, (t_blk, t_blk), 0)
    si = lax.broadcasted_iota(jnp.int32, (t_blk, t_blk), 1)
    diff = (ti - si).astype(F32)
    mask = jnp.exp(jnp.where(diff >= 0, diff * lgf, -diff * lgb))
    tr = lax.broadcasted_iota(jnp.int32, (t_blk, dh), 0).astype(F32)
    q_dec_f = jnp.exp((tr + 1.0) * lgf_h)
    q_dec_b = jnp.exp((t_blk - tr) * lgb_h)
    k_dec_f = jnp.exp((t_blk - 1.0 - tr) * lgf_h)
    k_dec_b = jnp.exp(tr * lgb_h)
    blk_f = jnp.exp(t_blk * lgf_h)
    blk_b = jnp.exp(t_blk * lgb_h)
    tn_dims = (((0,), (0,)), ((), ()))
    nt_dims = (((1,), (1,)), ((), ()))

    def rows(i):
        return pl.ds(pl.multiple_of(i * t_blk, t_blk), t_blk)

    def sweep_backward(qr, kr, vr, rbr, nblk, s0):
        def body(j, s):
            r = rows(nblk - 1 - j)
            q, k, v = qr[0, r, :], kr[0, r, :], vr[0, r, :]
            rbr[r, :] = q_dec_b * jnp.dot(q, s.astype(BF16), preferred_element_type=F32)
            kv = lax.dot_general(k, (v.astype(F32) * k_dec_b).astype(BF16), tn_dims,
                                 preferred_element_type=F32)
            return blk_b * s + kv
        return lax.fori_loop(0, nblk, body, s0)

    def sweep_forward(qr, kr, vr, gr, rbr, yr, nblk, s0):
        def body(i, s):
            r = rows(i)
            q, k, v = qr[0, r, :], kr[0, r, :], vr[0, r, :]
            a = lax.dot_general(q, k, nt_dims, preferred_element_type=F32) * mask
            o = jnp.dot(a.astype(BF16), v, preferred_element_type=F32)
            o = o + q_dec_f * jnp.dot(q, s.astype(BF16), preferred_element_type=F32) + rbr[r, :]
            mu = jnp.mean(o, axis=-1, keepdims=True)
            dlt = o - mu
            var = jnp.mean(dlt * dlt, axis=-1, keepdims=True)
            yr[0, r, :] = (dlt * lax.rsqrt(var + EPS) * gr[0, r, :].astype(F32)).astype(BF16)
            kv = lax.dot_general(k, (v.astype(F32) * k_dec_f).astype(BF16), tn_dims,
                                 preferred_element_type=F32)
            return blk_f * s + kv
        return lax.fori_loop(0, nblk, body, s0)

    zero = jnp.zeros((dh, dh), F32)
    n_ctx = cq_ref.shape[1] // t_blk
    n_lat = q_ref.shape[1] // t_blk
    s_b = sweep_backward(cq_ref, ck_ref, cv_ref, crb_ref, n_ctx, zero)
    s_f = sweep_forward(cq_ref, ck_ref, cv_ref, cg_ref, crb_ref, cy_ref, n_ctx, zero)
    sweep_backward(q_ref, k_ref, v_ref, rb_ref, n_lat, s_b)
    sweep_forward(q_ref, k_ref, v_ref, g_ref, rb_ref, y_ref, n_lat, s_f)


def _retention(rd, lat, ctx):
    rq, rk, rv, g = lat
    crq, crk, crv, cg = ctx
    b, seq, _ = rq.shape
    cl = crq.shape[1]
    dh = RET_HEAD_DIM
    head = lambda n: pl.BlockSpec((1, n, dh), lambda i, h: (i, 0, h))
    return pl.pallas_call(
        _ret_kernel,
        out_shape=(jax.ShapeDtypeStruct((b, seq, RET_W), BF16), jax.ShapeDtypeStruct((b, cl, RET_W), BF16)),
        grid=(b, N_RET_HEADS),
        in_specs=[pl.BlockSpec((1, 2, RET_BLOCK), lambda i, h: (h, 0, 0)),
                  head(seq), head(seq), head(seq), head(seq), head(cl), head(cl), head(cl), head(cl)],
        out_specs=(head(seq), head(cl)),
        scratch_shapes=[pltpu.VMEM((seq, dh), F32), pltpu.VMEM((cl, dh), F32)],
        compiler_params=pltpu.CompilerParams(dimension_semantics=("parallel", "parallel"),
                                             vmem_limit_bytes=VMEM_LIMIT),
        name="retention_scan",
    )(rd, rq, rk, rv, g, crq, crk, crv, cg)


def _gla_kernel(gn_ref, q_ref, k_ref, v_ref, g_ref, af_ref, ab_ref, cq_ref, ck_ref, cv_ref, cg_ref,
                caf_ref, cab_ref, y_ref, cy_ref, ob_ref, cob_ref):
    t_blk = GLA_BLOCK
    dv = GLA_VAL_DIM
    ti = lax.broadcasted_iota(jnp.int32, (t_blk, t_blk), 0)
    si = lax.broadcasted_iota(jnp.int32, (t_blk, t_blk), 1)
    lower = si <= ti
    upper = si >= ti
    strict_upper = si > ti
    cum_f = lower.astype(F32)
    cum_b = upper.astype(F32)
    lane = lax.broadcasted_iota(jnp.int32, (t_blk, 2 * GLA_KEY_DIM), 1)
    head0 = lane < GLA_KEY_DIM
    srow = lax.broadcasted_iota(jnp.int32, (2 * dv, 2 * GLA_KEY_DIM), 0)
    slane = lax.broadcasted_iota(jnp.int32, (2 * dv, 2 * GLA_KEY_DIM), 1)
    smask = (srow < dv) == (slane < GLA_KEY_DIM)
    tn_dims = (((0,), (0,)), ((), ()))
    nt_dims = (((1,), (1,)), ((), ()))

    def rows(i):
        return pl.ds(pl.multiple_of(i * t_blk, t_blk), t_blk)

    def block(q, k, v, a, s, cum_mat, mid, edge, keep):
        c = jnp.dot(cum_mat, a, preferred_element_type=F32, precision=lax.Precision.HIGHEST)
        c_mid = c[mid:mid + 1, :]
        c_edge = c[edge:edge + 1, :]
        qf = q.astype(F32)
        kf = k.astype(F32)
        q_mid = qf * jnp.exp(c - c_mid)
        k_mid = (kf * jnp.exp(c_mid - c)).astype(BF16)
        k_end = (kf * jnp.exp(c_edge - c)).astype(BF16)
        q_in = (qf * jnp.exp(c)).astype(BF16)
        outs = []
        for hd, sel in enumerate((head0, jnp.logical_not(head0))):
            sc = lax.dot_general(jnp.where(sel, q_mid, 0.0).astype(BF16), k_mid, nt_dims,
                                 preferred_element_type=F32)
            sc = jnp.where(keep, sc, 0.0).astype(BF16)
            outs.append(jnp.dot(sc, v[:, hd * dv:(hd + 1) * dv], preferred_element_type=F32))
        o = jnp.concatenate(outs, axis=1)
        o = o + lax.dot_general(q_in, s.astype(BF16), nt_dims, preferred_element_type=F32)
        kv = lax.dot_general(v, k_end, tn_dims, preferred_element_type=F32)
        s_new = jnp.exp(c_edge) * s + jnp.where(smask, kv, 0.0)
        return o, s_new

    def sweep_backward(qr, kr, vr, ar, obr, nblk, s0):
        def body(j, s):
            r = rows(nblk - 1 - j)
            o, s_new = block(qr[0, r, :], kr[0, r, :], vr[0, r, :], ar[0, r, :], s,
                             cum_b, GLA_HALF, 0, strict_upper)
            obr[r, :] = o
            return s_new
        return lax.fori_loop(0, nblk, body, s0)

    def sweep_forward(qr, kr, vr, ar, gr, obr, yr, nblk, s0):
        gn = gn_ref[...]

        def body(i, s):
            r = rows(i)
            o, s_new = block(qr[0, r, :], kr[0, r, :], vr[0, r, :], ar[0, r, :], s,
                             cum_f, GLA_HALF - 1, t_blk - 1, lower)
            o = o + obr[r, :]
            gate = gr[0, r, :].astype(F32)
            for hd in range(2):
                sl = slice(hd * dv, (hd + 1) * dv)
                oh = o[:, sl]
                ms = jnp.mean(oh * oh, axis=-1, keepdims=True)
                yr[0, r, sl] = (oh * lax.rsqrt(ms + EPS) * gn * gate[:, sl]).astype(BF16)
            return s_new
        return lax.fori_loop(0, nblk, body, s0)

    zero = jnp.zeros((2 * dv, 2 * GLA_KEY_DIM), F32)
    n_ctx = cq_ref.shape[1] // t_blk
    n_lat = q_ref.shape[1] // t_blk
    s_b = sweep_backward(cq_ref, ck_ref, cv_ref, cab_ref, cob_ref, n_ctx, zero)
    s_f = sweep_forward(cq_ref, ck_ref, cv_ref, caf_ref, cg_ref, cob_ref, cy_ref, n_ctx, zero)
    sweep_backward(q_ref, k_ref, v_ref, ab_ref, ob_ref, n_lat, s_b)
    sweep_forward(q_ref, k_ref, v_ref, af_ref, g_ref, ob_ref, y_ref, n_lat, s_f)


def _gla(gn, lat, ctx):
    gq, gk, gv, g, ga = lat
    cgq, cgk, cgv, cg, cga = ctx
    b, seq, _ = gq.shape
    cl = cgq.shape[1]
    pair_k = 2 * GLA_KEY_DIM
    pair_v = 2 * GLA_VAL_DIM
    n_pairs = N_GLA_HEADS // 2
    gate_off = RET_W // pair_v
    key = lambda n: pl.BlockSpec((1, n, pair_k), lambda i, p: (i, 0, p))
    key_b = lambda n: pl.BlockSpec((1, n, pair_k), lambda i, p: (i, 0, n_pairs + p))
    val = lambda n: pl.BlockSpec((1, n, pair_v), lambda i, p: (i, 0, p))
    mix = lambda n: pl.BlockSpec((1, n, pair_v), lambda i, p: (i, 0, gate_off + p))
    return pl.pallas_call(
        _gla_kernel,
        out_shape=(jax.ShapeDtypeStruct((b, seq, GLA_V), BF16), jax.ShapeDtypeStruct((b, cl, GLA_V), BF16)),
        grid=(b, n_pairs),
        in_specs=[_const_spec(gn.shape),
                  key(seq), key(seq), val(seq), mix(seq), key(seq), key_b(seq),
                  key(cl), key(cl), val(cl), mix(cl), key(cl), key_b(cl)],
        out_specs=(val(seq), val(cl)),
        scratch_shapes=[pltpu.VMEM((seq, pair_v), F32), pltpu.VMEM((cl, pair_v), F32)],
        compiler_params=pltpu.CompilerParams(dimension_semantics=("parallel", "parallel"),
                                             vmem_limit_bytes=VMEM_LIMIT),
        name="gla_scan",
    )(gn, gq, gk, gv, g, ga, ga, cgq, cgk, cgv, cg, cga, cga)


def _mlp_kernel(x_ref, yr_ref, yg_ref, mod_ref, n2_ref, wo_ref, w1_ref, w2_ref, fg_ref, o_ref, *,
                final, ff_chunk):
    x = x_ref[0]
    mod = mod_ref[0]
    mix = (jnp.dot(yr_ref[0], wo_ref[0:RET_W, :], preferred_element_type=F32)
           + jnp.dot(yg_ref[0], wo_ref[RET_W:MIX_W, :], preferred_element_type=F32))
    x1 = x + mod[2:3] * mix
    ms = jnp.mean(x1 * x1, axis=-1, keepdims=True)
    h = x1 * lax.rsqrt(ms + EPS) * n2_ref[...]
    hb = (h * (1.0 + mod[4:5]) + mod[3:4]).astype(BF16)
    acc = jnp.zeros(x.shape, F32)
    for c in range(w1_ref.shape[1] // ff_chunk):
        sl = slice(c * ff_chunk, (c + 1) * ff_chunk)
        a = jnp.maximum(jnp.dot(hb, w1_ref[:, sl], preferred_element_type=F32), 0.0)
        acc = acc + jnp.dot((a * a).astype(BF16), w2_ref[sl, :], preferred_element_type=F32)
    x2 = x1 + mod[5:6] * acc
    if final:
        ms = jnp.mean(x2 * x2, axis=-1, keepdims=True)
        x2 = x2 * lax.rsqrt(ms + EPS) * fg_ref[...]
    o_ref[0] = x2


def _out_mlp(xa, yr, yg, modl, mod_row, n2, wo, w1, w2, fg, tm, final):
    b, seq, d = xa.shape
    if mod_row is None:
        mod_map = lambda i, j: (i, 0, 0)
    else:
        mod_map = lambda i, j: (mod_row, 0, 0)
    tok = lambda n: pl.BlockSpec((1, tm, n), lambda i, j: (i, j, 0))
    return pl.pallas_call(
        functools.partial(_mlp_kernel, final=final, ff_chunk=1024),
        out_shape=jax.ShapeDtypeStruct(xa.shape, F32),
        grid=(b, seq // tm),
        in_specs=[tok(d), tok(RET_W), tok(GLA_V), pl.BlockSpec((1, 6, d), mod_map), _const_spec((1, d)),
                  _const_spec(wo.shape), _const_spec(w1.shape), _const_spec(w2.shape), _const_spec((1, d))],
        out_specs=tok(d),
        compiler_params=pltpu.CompilerParams(dimension_semantics=("parallel", "parallel"),
                                             vmem_limit_bytes=VMEM_LIMIT),
        name="out_mlp",
    )(xa, yr, yg, modl, n2, wo, w1, w2, fg)


def _arrange_w_in(w):
    splits = (RET_W, RET_W, RET_W, RET_W, GLA_K, GLA_K, GLA_V, GLA_V, GLA_GATE_RANK, GLA_GATE_RANK)
    offs = [0]
    for s in splits:
        offs.append(offs[-1] + s)
    rq, rk, rv, rg, gq, gk, gv, gg, df, db = (w[:, offs[i]:offs[i + 1]] for i in range(len(splits)))
    nf = RET_HEAD_DIM // 4
    col = jnp.arange(RET_W)
    partner = jnp.where(col % (2 * nf) < nf, col + nf, col - nf)
    pad = jnp.zeros((w.shape[0], LANES - 2 * GLA_GATE_RANK), w.dtype)
    return jnp.concatenate([rq, rq[:, partner], rk, rk[:, partner], rv, rg, gg, gq, gk, gv, df, db, pad],
                           axis=1).astype(BF16)


def _arrange_gate_up(up, ub):
    z = jnp.zeros((GLA_GATE_RANK, GLA_K), up.dtype)
    top = jnp.concatenate([up[0], z], axis=1)
    bot = jnp.concatenate([z, up[1]], axis=1)
    pad = jnp.zeros((LANES - 2 * GLA_GATE_RANK, 2 * GLA_K), up.dtype)
    return jnp.concatenate([top, bot, pad], axis=0).astype(BF16), ub.reshape(1, 2 * GLA_K)


def kernel(x, c, ctx, c_ctx, ada_w, ada_b, norm1_g, w_in, ret_decay, gla_gate_up, gla_gate_b, gla_norm_g,
           w_out, norm2_g, w_mlp1, w_mlp2, final_g):
    batch, seq, d = x.shape
    cl = ctx.shape[1]
    depth = ada_w.shape[0]
    ctx_row = batch
    mod_rows = 16
    cs = jnp.concatenate([c, c_ctx[None, :], jnp.zeros((mod_rows - batch - 1, d), c.dtype)], axis=0)
    mod = _modulation(cs, ada_w, ada_b).reshape(depth, mod_rows, 6, d)

    lat_tables = _rope_tables(seq)
    scale = RET_HEAD_DIM ** -0.5
    ones = jnp.ones((cl, RET_HEAD_DIM), F32)
    ctx_tables = (ones * scale, ones * 0.0, ones, ones * 0.0)

    fg = final_g.reshape(1, d)
    out = None
    for layer in range(depth):
        last = layer == depth - 1
        modl = mod[layer]
        w = _arrange_w_in(w_in[layer])
        up, ub = _arrange_gate_up(gla_gate_up[layer], gla_gate_b[layer])
        n1 = norm1_g[layer].reshape(1, d)
        n2 = norm2_g[layer].reshape(1, d)
        wo = w_out[layer].astype(BF16)
        w1 = w_mlp1[layer].astype(BF16)
        w2 = w_mlp2[layer].astype(BF16)
        rd = jnp.broadcast_to(ret_decay[layer].T[:, :, None], (N_RET_HEADS, 2, RET_BLOCK))
        gn = gla_norm_g[layer].reshape(1, GLA_VAL_DIM)

        lat = _project(x, modl, None, n1, w, up, ub, lat_tables, 512)
        cx = _project(ctx, modl, ctx_row, n1, w, up, ub, ctx_tables, cl)
        rq, rk, rv, g, gq, gk, gv, ga = lat
        crq, crk, crv, cg, cgq, cgk, cgv, cga = cx
        yr, cyr = _retention(rd, (rq, rk, rv, g), (crq, crk, crv, cg))
        yg, cyg = _gla(gn, (gq, gk, gv, g, ga), (cgq, cgk, cgv, cg, cga))
        x = _out_mlp(x, yr, yg, modl, None, n2, wo, w1, w2, fg, 512, last)
        if not last:
            ctx = _out_mlp(ctx, cyr, cyg, modl, ctx_row, n2, wo, w1, w2, fg, cl, False)
    return x
```

```python
import functools
import math

import jax
import jax.numpy as jnp
from jax import lax
from jax.experimental import pallas as pl
from jax.experimental.pallas import tpu as pltpu

F32 = jnp.float32
BF16 = jnp.bfloat16

GRID_W = 64
N_RET_HEADS = 4
RET_HEAD_DIM = 128
N_GLA_HEADS = 4
GLA_KEY_DIM = 64
GLA_VAL_DIM = 128
GLA_GATE_RANK = 16
GLA_GATE_NORM = 16.0
ROPE_BASE = 10000.0
EPS = 1e-6

RET_W = N_RET_HEADS * RET_HEAD_DIM
GLA_K = N_GLA_HEADS * GLA_KEY_DIM
GLA_V = N_GLA_HEADS * GLA_VAL_DIM
MIX_W = RET_W + GLA_V
LANES = 128

OFF_RQ, OFF_RQS, OFF_RK, OFF_RKS, OFF_RV, OFF_RG, OFF_GG = (i * RET_W for i in range(7))
OFF_GQ = 7 * RET_W
OFF_GK = OFF_GQ + GLA_K
OFF_GV = OFF_GK + GLA_K
OFF_D = OFF_GV + GLA_V
PROJ_W = OFF_D + LANES

RET_BLOCK = 256
GLA_BLOCK = 128
GLA_HALF = GLA_BLOCK // 2

VMEM_LIMIT = 56 * 1024 * 1024


def _const_spec(shape):
    zeros = (0,) * len(shape)
    return pl.BlockSpec(shape, lambda *_: zeros, pipeline_mode=pl.Buffered(1))


def _mod_kernel(cs_ref, w_ref, b_ref, o_ref):
    cs = cs_ref[...]
    s = cs * jax.nn.sigmoid(cs)
    o_ref[0] = jnp.dot(s.astype(BF16), w_ref[0].astype(BF16), preferred_element_type=F32) + b_ref[0]


def _modulation(cs, ada_w, ada_b):
    depth, d, n = ada_w.shape
    rows = cs.shape[0]
    tn = 1024
    return pl.pallas_call(
        _mod_kernel,
        out_shape=jax.ShapeDtypeStruct((depth, rows, n), F32),
        grid=(depth, n // tn),
        in_specs=[
            pl.BlockSpec((rows, d), lambda l, j: (0, 0)),
            pl.BlockSpec((1, d, tn), lambda l, j: (l, 0, j)),
            pl.BlockSpec((1, 1, tn), lambda l, j: (l, 0, j)),
        ],
        out_specs=pl.BlockSpec((1, rows, tn), lambda l, j: (l, 0, j)),
        compiler_params=pltpu.CompilerParams(dimension_semantics=("parallel", "parallel"),
                                             vmem_limit_bytes=VMEM_LIMIT),
        name="adaln_modulation",
    )(cs, ada_w, ada_b.reshape(depth, 1, n))


def _rope_kernel(cq_ref, sq_ref, ck_ref, sk_ref):
    shape = cq_ref.shape
    t = lax.broadcasted_iota(jnp.int32, shape, 0)
    lane = lax.broadcasted_iota(jnp.int32, shape, 1)
    nf = RET_HEAD_DIM // 4
    freq = (lane & (nf - 1)).astype(F32)
    inv = jnp.exp(freq * (-math.log(ROPE_BASE) / nf))
    row = lax.shift_right_logical(t, GRID_W.bit_length() - 1)
    pos = jnp.where(lane < RET_HEAD_DIM // 2, row, t & (GRID_W - 1)).astype(F32)
    ang = pos * inv
    cos = jnp.cos(ang)
    sin = jnp.where((lane & (2 * nf - 1)) < nf, -jnp.sin(ang), jnp.sin(ang))
    scale = RET_HEAD_DIM ** -0.5
    cq_ref[...] = cos * scale
    sq_ref[...] = sin * scale
    ck_ref[...] = cos
    sk_ref[...] = sin


def _rope_tables(seq):
    sds = jax.ShapeDtypeStruct((seq, RET_HEAD_DIM), F32)
    return pl.pallas_call(_rope_kernel, out_shape=(sds, sds, sds, sds), name="rope_tables")()


def _proj_kernel(x_ref, mod_ref, n1_ref, w_ref, up_ref, ub_ref, cq_ref, sq_ref, ck_ref, sk_ref,
                 rq_ref, rk_ref, rv_ref, g_ref, gq_ref, gk_ref, gv_ref, ga_ref):
    x = x_ref[0]
    mod = mod_ref[0]
    ms = jnp.mean(x * x, axis=-1, keepdims=True)
    h = x * lax.rsqrt(ms + EPS) * n1_ref[...]
    h = h * (1.0 + mod[1:2]) + mod[0:1]
    hb = h.astype(BF16)

    def mm(off, n):
        return jnp.dot(hb, w_ref[:, off:off + n], preferred_element_type=F32)

    def rotary(z, zs, c_ref, s_ref, o_ref):
        c = c_ref[...]
        s = s_ref[...]
        for hd in range(N_RET_HEADS):
            sl = slice(hd * RET_HEAD_DIM, (hd + 1) * RET_HEAD_DIM)
            o_ref[0, :, sl] = (z[:, sl] * c + zs[:, sl] * s).astype(BF16)

    rotary(mm(OFF_RQ, RET_W), mm(OFF_RQS, RET_W), cq_ref, sq_ref, rq_ref)
    rotary(mm(OFF_RK, RET_W), mm(OFF_RKS, RET_W), ck_ref, sk_ref, rk_ref)
    rv_ref[0] = mm(OFF_RV, RET_W).astype(BF16)
    z = mm(OFF_RG, RET_W)
    g_ref[0, :, 0:RET_W] = (z * jax.nn.sigmoid(z)).astype(BF16)
    z = mm(OFF_GG, GLA_V)
    g_ref[0, :, RET_W:MIX_W] = (z * jax.nn.sigmoid(z)).astype(BF16)
    gq_ref[0] = (mm(OFF_GQ, GLA_K) * (GLA_KEY_DIM ** -0.5)).astype(BF16)
    gk_ref[0] = mm(OFF_GK, GLA_K).astype(BF16)
    gv_ref[0] = mm(OFF_GV, GLA_V).astype(BF16)
    d = mm(OFF_D, LANES).astype(BF16)
    zg = jnp.dot(d, up_ref[...], preferred_element_type=F32) + ub_ref[...]
    ga_ref[0] = (jnp.minimum(zg, 0.0) - jnp.log1p(jnp.exp(-jnp.abs(zg)))) * (1.0 / GLA_GATE_NORM)


def _project(xa, modl, mod_row, n1, w, up, ub, tables, tm):
    b, seq, d = xa.shape
    if mod_row is None:
        mod_map = lambda i, j: (i, 0, 0)
    else:
        mod_map = lambda i, j: (mod_row, 0, 0)
    tok = lambda n: pl.BlockSpec((1, tm, n), lambda i, j: (i, j, 0))
    tab = pl.BlockSpec((tm, RET_HEAD_DIM), lambda i, j: (j, 0))
    sd = lambda n, dt: jax.ShapeDtypeStruct((b, seq, n), dt)
    return pl.pallas_call(
        _proj_kernel,
        out_shape=(sd(RET_W, BF16), sd(RET_W, BF16), sd(RET_W, BF16), sd(MIX_W, BF16),
                   sd(GLA_K, BF16), sd(GLA_K, BF16), sd(GLA_V, BF16), sd(2 * GLA_K, F32)),
        grid=(b, seq // tm),
        in_specs=[tok(d), pl.BlockSpec((1, 6, d), mod_map), _const_spec((1, d)), _const_spec(w.shape),
                  _const_spec(up.shape), _const_spec(ub.shape), tab, tab, tab, tab],
        out_specs=(tok(RET_W), tok(RET_W), tok(RET_W), tok(MIX_W), tok(GLA_K), tok(GLA_K), tok(GLA_V),
                   tok(2 * GLA_K)),
        compiler_params=pltpu.CompilerParams(dimension_semantics=("parallel", "parallel"),
                                             vmem_limit_bytes=VMEM_LIMIT),
        name="norm_project",
    )(xa, modl, n1, w, up, ub, *tables)


def _ret_kernel(rd_ref, q_ref, k_ref, v_ref, g_ref, cq_ref, ck_ref, cv_ref, cg_ref,
                y_ref, cy_ref, kv_ref, st_ref, sc_ref):
    t_blk = RET_BLOCK
    dh = RET_HEAD_DIM
    lg = jnp.log1p(-jnp.exp(rd_ref[0]))
    lgf, lgb = lg[0:1, :], lg[1:2, :]
    lgf_h, lgb_h = lgf[:, :dh], lgb[:, :dh]
    ti = lax.broadcasted_iota(jnp.int32, (t_blk, t_blk), 0)
    si = lax.broadcasted_iota(jnp.int32, (t_blk, t_blk), 1)
    diff = (ti - si).astype(F32)
    mask = jnp.exp(jnp.where(diff >= 0, diff * lgf, -diff * lgb))
    tr = lax.broadcasted_iota(jnp.int32, (t_blk, dh), 0).astype(F32)
    q_dec = jnp.concatenate([jnp.exp((tr + 1.0) * lgf_h), jnp.exp((t_blk - tr) * lgb_h)], axis=1)
    k_dec_f = jnp.exp((t_blk - 1.0 - tr) * lgf_h)
    k_dec_b = jnp.exp(tr * lgb_h)
    blk_f = jnp.exp(t_blk * lgf_h)
    blk_b = jnp.exp(t_blk * lgb_h)
    tn_dims = (((0,), (0,)), ((), ()))
    nt_dims = (((1,), (1,)), ((), ()))
    n_ctx = cq_ref.shape[1] // t_blk
    n_lat = q_ref.shape[1] // t_blk
    n_tot = n_ctx + n_lat
    fwd = slice(0, dh)
    bwd = slice(dh, 2 * dh)

    def rows(i):
        return pl.ds(pl.multiple_of(i * t_blk, t_blk), t_blk)

    def kv_block(kr, vr, i, j):
        r = rows(i)
        v = vr[0, r, :].astype(F32)
        vv = jnp.concatenate([(v * k_dec_f).astype(BF16), (v * k_dec_b).astype(BF16)], axis=1)
        kv_ref[j] = lax.dot_general(kr[0, r, :], vv, tn_dims, preferred_element_type=F32)

    def score_block(qr, kr, i, j):
        r = rows(i)
        a = lax.dot_general(qr[0, r, :], kr[0, r, :], nt_dims, preferred_element_type=F32) * mask
        sc_ref[j] = a.astype(BF16)

    def out_block(qr, vr, gr, yr, i, j):
        r = rows(i)
        q, v = qr[0, r, :], vr[0, r, :]
        o = jnp.dot(sc_ref[j], v, preferred_element_type=F32)
        inter = jnp.dot(q, st_ref[j], preferred_element_type=F32) * q_dec
        o = o + inter[:, fwd] + inter[:, bwd]
        mu = jnp.mean(o, axis=-1, keepdims=True)
        dlt = o - mu
        var = jnp.mean(dlt * dlt, axis=-1, keepdims=True)
        yr[0, r, :] = (dlt * lax.rsqrt(var + EPS) * gr[0, r, :].astype(F32)).astype(BF16)

    def for_blocks(n, fn):
        def body(i, carry):
            fn(i)
            return carry
        lax.fori_loop(0, n, body, 0, unroll=math.gcd(n, 8))

    for_blocks(n_ctx, lambda i: kv_block(ck_ref, cv_ref, i, i))
    for_blocks(n_lat, lambda i: kv_block(k_ref, v_ref, i, n_ctx + i))

    def step_f(j, s):
        st_ref[j, :, fwd] = s.astype(BF16)
        return blk_f * s + kv_ref[j, :, fwd]

    def step_b(j, s):
        st_ref[j, :, bwd] = s.astype(BF16)
        return blk_b * s + kv_ref[j, :, bwd]

    zero = jnp.zeros((dh, dh), F32)
    lax.fori_loop(0, n_tot, step_f, zero)
    s_b = lax.fori_loop(0, n_ctx, lambda t, s: step_b(n_ctx - 1 - t, s), zero)
    lax.fori_loop(0, n_lat, lambda t, s: step_b(n_tot - 1 - t, s), s_b)

    for_blocks(n_ctx, lambda i: score_block(cq_ref, ck_ref, i, i))
    for_blocks(n_lat, lambda i: score_block(q_ref, k_ref, i, n_ctx + i))
    for_blocks(n_ctx, lambda i: out_block(cq_ref, cv_ref, cg_ref, cy_ref, i, i))
    for_blocks(n_lat, lambda i: out_block(q_ref, v_ref, g_ref, y_ref, i, n_ctx + i))


def _retention(rd, lat, ctx):
    rq, rk, rv, g = lat
    crq, crk, crv, cg = ctx
    b, seq, _ = rq.shape
    cl = crq.shape[1]
    dh = RET_HEAD_DIM
    n_blocks = (seq + cl) // RET_BLOCK
    head = lambda n: pl.BlockSpec((1, n, dh), lambda i, h: (i, 0, h))
    return pl.pallas_call(
        _ret_kernel,
        out_shape=(jax.ShapeDtypeStruct((b, seq, RET_W), BF16), jax.ShapeDtypeStruct((b, cl, RET_W), BF16)),
        grid=(b, N_RET_HEADS),
        in_specs=[pl.BlockSpec((1, 2, RET_BLOCK), lambda i, h: (h, 0, 0)),
                  head(seq), head(seq), head(seq), head(seq), head(cl), head(cl), head(cl), head(cl)],
        out_specs=(head(seq), head(cl)),
        scratch_shapes=[pltpu.VMEM((n_blocks, dh, 2 * dh), F32), pltpu.VMEM((n_blocks, dh, 2 * dh), BF16),
                        pltpu.VMEM((n_blocks, RET_BLOCK, RET_BLOCK), BF16)],
        compiler_params=pltpu.CompilerParams(dimension_semantics=("parallel", "parallel"),
                                             vmem_limit_bytes=VMEM_LIMIT),
        name="retention_scan",
    )(rd, rq, rk, rv, g, crq, crk, crv, cg)


def _gla_kernel(gn_ref, q_ref, k_ref, v_ref, g_ref, af_ref, ab_ref, cq_ref, ck_ref, cv_ref, cg_ref,
                caf_ref, cab_ref, y_ref, cy_ref, c_ref, kv_ref, dec_ref, st_ref, sc_ref):
    t_blk = GLA_BLOCK
    dv = GLA_VAL_DIM
    dk2 = 2 * GLA_KEY_DIM
    assert t_blk == dk2 == dv
    ti = lax.broadcasted_iota(jnp.int32, (t_blk, t_blk), 0)
    si = lax.broadcasted_iota(jnp.int32, (t_blk, t_blk), 1)
    lower = si <= ti
    cum_f = lower.astype(BF16)
    cum_b = (si >= ti).astype(BF16)
    head0 = si < GLA_KEY_DIM
    tn_dims = (((0,), (0,)), ((), ()))
    nt_dims = (((1,), (1,)), ((), ()))

    def rows(i):
        return pl.ds(pl.multiple_of(i * t_blk, t_blk), t_blk)

    def cumulate(cum_mat, a):
        a1 = a.astype(BF16)
        r1 = a - a1.astype(F32)
        a2 = r1.astype(BF16)
        a3 = (r1 - a2.astype(F32)).astype(BF16)
        z = jnp.dot(cum_mat, jnp.concatenate([a1, a2, a3], axis=1), preferred_element_type=F32)
        return z[:, 0:dk2] + z[:, dk2:2 * dk2] + z[:, 2 * dk2:3 * dk2]

    dirs = ((cum_f, GLA_HALF - 1, t_blk - 1), (cum_b, GLA_HALF, 0))

    def cum_block(ars, i, row0):
        for d, (cum_mat, _, _) in enumerate(dirs):
            c_ref[d, rows(row0 + i), :] = cumulate(cum_mat, ars[d][0, rows(i), :])

    def kv_block(kr, vr, i, j, row0):
        r = rows(i)
        cr = rows(row0 + i)
        k = kr[0, r, :].astype(F32)
        v = vr[0, r, :]
        for d, (_, _, edge) in enumerate(dirs):
            c = c_ref[d, cr, :]
            c_edge = c[edge:edge + 1, :]
            k_end = (k * jnp.exp(c_edge - c)).astype(BF16)
            kv = lax.dot_general(v, k_end, tn_dims, preferred_element_type=F32)
            kv_ref[d, j] = jnp.where(head0, kv[0:dv, :], kv[dv:2 * dv, :])
            dec_ref[d, j] = jnp.broadcast_to(jnp.exp(c_edge), (8, dk2))

    def score_block(qr, kr, i, j, row0):
        r = rows(i)
        cr = rows(row0 + i)
        q = qr[0, r, :].astype(F32)
        k = kr[0, r, :].astype(F32)
        q_mid, k_mid = [], []
        for d, (_, mid, _) in enumerate(dirs):
            c = c_ref[d, cr, :]
            c_mid = c[mid:mid + 1, :]
            q_mid.append(q * jnp.exp(c - c_mid))
            k_mid.append((k * jnp.exp(c_mid - c)).astype(BF16))
        for hd, sel in enumerate((head0, jnp.logical_not(head0))):
            sc_f = lax.dot_general(jnp.where(sel, q_mid[0], 0.0).astype(BF16), k_mid[0], nt_dims,
                                   preferred_element_type=F32)
            sc_b = lax.dot_general(jnp.where(sel, q_mid[1], 0.0).astype(BF16), k_mid[1], nt_dims,
                                   preferred_element_type=F32)
            sc_ref[hd, j] = jnp.where(lower, sc_f, sc_b).astype(BF16)

    def out_block(qr, vr, gr, yr, i, j, row0):
        r = rows(i)
        cr = rows(row0 + i)
        q = qr[0, r, :].astype(F32)
        v = vr[0, r, :]
        q_in, s_exp = [], []
        for d in range(2):
            q_in.append((q * jnp.exp(c_ref[d, cr, :])).astype(BF16))
            s = st_ref[d, j]
            zero = jnp.zeros_like(s)
            s_exp.append(jnp.concatenate([jnp.where(head0, s, zero), jnp.where(head0, zero, s)], axis=0))
        o = jnp.concatenate([jnp.dot(sc_ref[hd, j], v[:, hd * dv:(hd + 1) * dv], preferred_element_type=F32)
                             for hd in range(2)], axis=1)
        o = o + lax.dot_general(jnp.concatenate(q_in, axis=1), jnp.concatenate(s_exp, axis=1), nt_dims,
                                preferred_element_type=F32)
        gate = gr[0, r, :].astype(F32)
        gn = gn_ref[...]
        for hd in range(2):
            sl = slice(hd * dv, (hd + 1) * dv)
            oh = o[:, sl]
            ms = jnp.mean(oh * oh, axis=-1, keepdims=True)
            yr[0, r, sl] = (oh * lax.rsqrt(ms + EPS) * gn * gate[:, sl]).astype(BF16)

    def for_blocks(n, fn):
        def body(i, carry):
            fn(i)
            return carry
        lax.fori_loop(0, n, body, 0, unroll=math.gcd(n, 8))

    n_ctx = cq_ref.shape[1] // t_blk
    n_lat = q_ref.shape[1] // t_blk
    n_tot = n_ctx + n_lat
    for_blocks(n_ctx, lambda i: cum_block((caf_ref, cab_ref), i, 0))
    for_blocks(n_lat, lambda i: cum_block((af_ref, ab_ref), i, n_ctx))
    for_blocks(n_ctx, lambda i: kv_block(ck_ref, cv_ref, i, i, 0))
    for_blocks(n_lat, lambda i: kv_block(k_ref, v_ref, i, n_ctx + i, n_ctx))

    def step(d, j, s):
        st_ref[d, j] = s.astype(BF16)
        return dec_ref[d, j, 0:1, :] * s + kv_ref[d, j]

    zero = jnp.zeros((dv, dk2), F32)
    lax.fori_loop(0, n_tot, lambda j, s: step(0, j, s), zero)
    s_b = lax.fori_loop(0, n_ctx, lambda t, s: step(1, n_ctx - 1 - t, s), zero)
    lax.fori_loop(0, n_lat, lambda t, s: step(1, n_tot - 1 - t, s), s_b)

    for_blocks(n_ctx, lambda i: score_block(cq_ref, ck_ref, i, i, 0))
    for_blocks(n_lat, lambda i: score_block(q_ref, k_ref, i, n_ctx + i, n_ctx))
    for_blocks(n_ctx, lambda i: out_block(cq_ref, cv_ref, cg_ref, cy_ref, i, i, 0))
    for_blocks(n_lat, lambda i: out_block(q_ref, v_ref, g_ref, y_ref, i, n_ctx + i, n_ctx))


def _gla(gn, lat, ctx):
    gq, gk, gv, g, ga = lat
    cgq, cgk, cgv, cg, cga = ctx
    b, seq, _ = gq.shape
    cl = cgq.shape[1]
    pair_k = 2 * GLA_KEY_DIM
    pair_v = 2 * GLA_VAL_DIM
    n_pairs = N_GLA_HEADS // 2
    n_blocks = (seq + cl) // GLA_BLOCK
    gate_off = RET_W // pair_v
    key = lambda n: pl.BlockSpec((1, n, pair_k), lambda i, p: (i, 0, p))
    key_b = lambda n: pl.BlockSpec((1, n, pair_k), lambda i, p: (i, 0, n_pairs + p))
    val = lambda n: pl.BlockSpec((1, n, pair_v), lambda i, p: (i, 0, p))
    mix = lambda n: pl.BlockSpec((1, n, pair_v), lambda i, p: (i, 0, gate_off + p))
    return pl.pallas_call(
        _gla_kernel,
        out_shape=(jax.ShapeDtypeStruct((b, seq, GLA_V), BF16), jax.ShapeDtypeStruct((b, cl, GLA_V), BF16)),
        grid=(b, n_pairs),
        in_specs=[_const_spec(gn.shape),
                  key(seq), key(seq), val(seq), mix(seq), key(seq), key_b(seq),
                  key(cl), key(cl), val(cl), mix(cl), key(cl), key_b(cl)],
        out_specs=(val(seq), val(cl)),
        scratch_shapes=[pltpu.VMEM((2, seq + cl, pair_k), F32),
                        pltpu.VMEM((2, n_blocks, GLA_VAL_DIM, pair_k), F32),
                        pltpu.VMEM((2, n_blocks, 8, pair_k), F32),
                        pltpu.VMEM((2, n_blocks, GLA_VAL_DIM, pair_k), BF16),
                        pltpu.VMEM((2, n_blocks, GLA_BLOCK, GLA_BLOCK), BF16)],
        compiler_params=pltpu.CompilerParams(dimension_semantics=("parallel", "parallel"),
                                             vmem_limit_bytes=VMEM_LIMIT),
        name="gla_scan",
    )(gn, gq, gk, gv, g, ga, ga, cgq, cgk, cgv, cg, cga, cga)


def _mlp_kernel(x_ref, yr_ref, yg_ref, mod_ref, n2_ref, wo_ref, w1_ref, w2_ref, fg_ref, o_ref, *,
                final, ff_chunk):
    x = x_ref[0]
    mod = mod_ref[0]
    mix = (jnp.dot(yr_ref[0], wo_ref[0:RET_W, :], preferred_element_type=F32)
           + jnp.dot(yg_ref[0], wo_ref[RET_W:MIX_W, :], preferred_element_type=F32))
    x1 = x + mod[2:3] * mix
    ms = jnp.mean(x1 * x1, axis=-1, keepdims=True)
    h = x1 * lax.rsqrt(ms + EPS) * n2_ref[...]
    hb = (h * (1.0 + mod[4:5]) + mod[3:4]).astype(BF16)
    acc = jnp.zeros(x.shape, F32)
    for c in range(w1_ref.shape[1] // ff_chunk):
        sl = slice(c * ff_chunk, (c + 1) * ff_chunk)
        a = jnp.maximum(jnp.dot(hb, w1_ref[:, sl], preferred_element_type=F32), 0.0)
        acc = acc + jnp.dot((a * a).astype(BF16), w2_ref[sl, :], preferred_element_type=F32)
    x2 = x1 + mod[5:6] * acc
    if final:
        ms = jnp.mean(x2 * x2, axis=-1, keepdims=True)
        x2 = x2 * lax.rsqrt(ms + EPS) * fg_ref[...]
    o_ref[0] = x2


def _out_mlp(xa, yr, yg, modl, mod_row, n2, wo, w1, w2, fg, tm, final):
    b, seq, d = xa.shape
    if mod_row is None:
        mod_map = lambda i, j: (i, 0, 0)
    else:
        mod_map = lambda i, j: (mod_row, 0, 0)
    tok = lambda n: pl.BlockSpec((1, tm, n), lambda i, j: (i, j, 0))
    return pl.pallas_call(
        functools.partial(_mlp_kernel, final=final, ff_chunk=1024),
        out_shape=jax.ShapeDtypeStruct(xa.shape, F32),
        grid=(b, seq // tm),
        in_specs=[tok(d), tok(RET_W), tok(GLA_V), pl.BlockSpec((1, 6, d), mod_map), _const_spec((1, d)),
                  _const_spec(wo.shape), _const_spec(w1.shape), _const_spec(w2.shape), _const_spec((1, d))],
        out_specs=tok(d),
        compiler_params=pltpu.CompilerParams(dimension_semantics=("parallel", "parallel"),
                                             vmem_limit_bytes=VMEM_LIMIT),
        name="out_mlp",
    )(xa, yr, yg, modl, n2, wo, w1, w2, fg)


def _arrange_w_in(w):
    splits = (RET_W, RET_W, RET_W, RET_W, GLA_K, GLA_K, GLA_V, GLA_V, GLA_GATE_RANK, GLA_GATE_RANK)
    offs = [0]
    for s in splits:
        offs.append(offs[-1] + s)
    rq, rk, rv, rg, gq, gk, gv, gg, df, db = (w[:, offs[i]:offs[i + 1]] for i in range(len(splits)))
    nf = RET_HEAD_DIM // 4
    col = jnp.arange(RET_W)
    partner = jnp.where(col % (2 * nf) < nf, col + nf, col - nf)
    pad = jnp.zeros((w.shape[0], LANES - 2 * GLA_GATE_RANK), w.dtype)
    return jnp.concatenate([rq, rq[:, partner], rk, rk[:, partner], rv, rg, gg, gq, gk, gv, df, db, pad],
                           axis=1).astype(BF16)


def _arrange_gate_up(up, ub):
    z = jnp.zeros((GLA_GATE_RANK, GLA_K), up.dtype)
    top = jnp.concatenate([up[0], z], axis=1)
    bot = jnp.concatenate([z, up[1]], axis=1)
    pad = jnp.zeros((LANES - 2 * GLA_GATE_RANK, 2 * GLA_K), up.dtype)
    return jnp.concatenate([top, bot, pad], axis=0).astype(BF16), ub.reshape(1, 2 * GLA_K)


def kernel(x, c, ctx, c_ctx, ada_w, ada_b, norm1_g, w_in, ret_decay, gla_gate_up, gla_gate_b, gla_norm_g,
           w_out, norm2_g, w_mlp1, w_mlp2, final_g):
    batch, seq, d = x.shape
    cl = ctx.shape[1]
    depth = ada_w.shape[0]
    ctx_row = batch
    mod_rows = 16
    cs = jnp.concatenate([c, c_ctx[None, :], jnp.zeros((mod_rows - batch - 1, d), c.dtype)], axis=0)
    mod = _modulation(cs, ada_w, ada_b).reshape(depth, mod_rows, 6, d)

    lat_tables = _rope_tables(seq)
    scale = RET_HEAD_DIM ** -0.5
    ones = jnp.ones((cl, RET_HEAD_DIM), F32)
    ctx_tables = (ones * scale, ones * 0.0, ones, ones * 0.0)

    fg = final_g.reshape(1, d)
    out = None
    for layer in range(depth):
        last = layer == depth - 1
        modl = mod[layer]
        w = _arrange_w_in(w_in[layer])
        up, ub = _arrange_gate_up(gla_gate_up[layer], gla_gate_b[layer])
        n1 = norm1_g[layer].reshape(1, d)
        n2 = norm2_g[layer].reshape(1, d)
        wo = w_out[layer].astype(BF16)
        w1 = w_mlp1[layer].astype(BF16)
        w2 = w_mlp2[layer].astype(BF16)
        rd = jnp.broadcast_to(ret_decay[layer].T[:, :, None], (N_RET_HEADS, 2, RET_BLOCK))
        gn = gla_norm_g[layer].reshape(1, GLA_VAL_DIM)

        lat = _project(x, modl, None, n1, w, up, ub, lat_tables, 512)
        cx = _project(ctx, modl, ctx_row, n1, w, up, ub, ctx_tables, cl)
        rq, rk, rv, g, gq, gk, gv, ga = lat
        crq, crk, crv, cg, cgq, cgk, cgv, cga = cx
        yr, cyr = _retention(rd, (rq, rk, rv, g), (crq, crk, crv, cg))
        yg, cyg = _gla(gn, (gq, gk, gv, g, ga), (cgq, cgk, cgv, cg, cga))
        x = _out_mlp(x, yr, yg, modl, None, n2, wo, w1, w2, fg, 512, last)
        if not last:
            ctx = _out_mlp(ctx, cyr, cyg, modl, ctx_row, n2, wo, w1, w2, fg, cl, False)
    return x
```

```python
import functools
import math

import jax
import jax.numpy as jnp
from jax import lax
from jax.experimental import pallas as pl
from jax.experimental.pallas import tpu as pltpu

F32 = jnp.float32
BF16 = jnp.bfloat16

GRID_W = 64
N_RET_HEADS = 4
RET_HEAD_DIM = 128
N_GLA_HEADS = 4
GLA_KEY_DIM = 64
GLA_VAL_DIM = 128
GLA_GATE_RANK = 16
GLA_GATE_NORM = 16.0
ROPE_BASE = 10000.0
EPS = 1e-6

RET_W = N_RET_HEADS * RET_HEAD_DIM
GLA_K = N_GLA_HEADS * GLA_KEY_DIM
GLA_V = N_GLA_HEADS * GLA_VAL_DIM
MIX_W = RET_W + GLA_V

OFF_RV = 2 * RET_W
OFF_RG = 3 * RET_W
OFF_GQ = 4 * RET_W
OFF_GK = OFF_GQ + GLA_K
OFF_GV = OFF_GK + GLA_K
OFF_GG = OFF_GV + GLA_V
OFF_D = OFF_GG + GLA_V
IN_W = OFF_D + 2 * GLA_GATE_RANK

RET_BLOCK = 256
GLA_BLOCK = 128
GLA_HALF = GLA_BLOCK // 2
SCAN_UNROLL = 8

VMEM_LIMIT = 56 * 1024 * 1024


def _layer_spec(shape, layer):
    zeros = (0,) * len(shape)
    return pl.BlockSpec((None,) + tuple(shape), lambda *_: (layer,) + zeros, pipeline_mode=pl.Buffered(1))


def _const_spec(shape):
    zeros = (0,) * len(shape)
    return pl.BlockSpec(shape, lambda *_: zeros, pipeline_mode=pl.Buffered(1))


def _mod_kernel(cs_ref, w_ref, b_ref, o_ref):
    cs = cs_ref[...]
    s = cs * jax.nn.sigmoid(cs)
    o_ref[0] = jnp.dot(s.astype(BF16), w_ref[0].astype(BF16), preferred_element_type=F32) + b_ref[0]


def _modulation(cs, ada_w, ada_b):
    depth, d, n = ada_w.shape
    rows = cs.shape[0]
    tn = 1024
    return pl.pallas_call(
        _mod_kernel,
        out_shape=jax.ShapeDtypeStruct((depth, rows, n), F32),
        grid=(depth, n // tn),
        in_specs=[
            pl.BlockSpec((rows, d), lambda l, j: (0, 0)),
            pl.BlockSpec((1, d, tn), lambda l, j: (l, 0, j)),
            pl.BlockSpec((1, 1, tn), lambda l, j: (l, 0, j)),
        ],
        out_specs=pl.BlockSpec((1, rows, tn), lambda l, j: (l, 0, j)),
        compiler_params=pltpu.CompilerParams(dimension_semantics=("parallel", "parallel"),
                                             vmem_limit_bytes=VMEM_LIMIT),
        name="adaln_modulation",
    )(cs, ada_w, ada_b.reshape(depth, 1, n))


_QUARTER = RET_HEAD_DIM // 4


def _rope_lane_perm():
    src = []
    for lane in range(RET_HEAD_DIM):
        second, col = lane // (2 * _QUARTER), (lane // _QUARTER) % 2
        src.append(col * 2 * _QUARTER + second * _QUARTER + lane % _QUARTER)
    return src


def _rope_kernel(cq_ref, sq_ref, ck_ref, sk_ref):
    shape = cq_ref.shape
    t = lax.broadcasted_iota(jnp.int32, shape, 0)
    lane = lax.broadcasted_iota(jnp.int32, shape, 1)
    freq = (lane & (_QUARTER - 1)).astype(F32)
    inv = jnp.exp(freq * (-math.log(ROPE_BASE) / _QUARTER))
    row = lax.shift_right_logical(t, GRID_W.bit_length() - 1)
    pos = jnp.where((lane & _QUARTER) == 0, row, t & (GRID_W - 1)).astype(F32)
    ang = pos * inv
    cos = jnp.cos(ang)
    sin = jnp.where(lane < 2 * _QUARTER, -jnp.sin(ang), jnp.sin(ang))
    scale = RET_HEAD_DIM ** -0.5
    cq_ref[...] = cos * scale
    sq_ref[...] = sin * scale
    ck_ref[...] = cos
    sk_ref[...] = sin


def _rope_tables(seq):
    sds = jax.ShapeDtypeStruct((seq, RET_HEAD_DIM), F32)
    return pl.pallas_call(_rope_kernel, out_shape=(sds, sds, sds, sds), name="rope_tables")()


def _proj_kernel(x_ref, mod_ref, n1_ref, wqk_ref, w_ref, up_ref, ub_ref, cq_ref, sq_ref, ck_ref, sk_ref,
                 rq_ref, rk_ref, rv_ref, g_ref, gq_ref, gk_ref, gv_ref, ga_ref):
    x = x_ref[0]
    mod = mod_ref[0]
    ms = jnp.mean(x * x, axis=-1, keepdims=True)
    h = x * lax.rsqrt(ms + EPS) * n1_ref[...]
    h = h * (1.0 + mod[1:2]) + mod[0:1]
    hb = h.astype(BF16)

    def mm(wr, off, n):
        return jnp.dot(hb, wr[:, off:off + n], preferred_element_type=F32)

    def rotary(z, c_ref, s_ref, o_ref):
        c = c_ref[...]
        s = s_ref[...]
        for hd in range(N_RET_HEADS):
            sl = slice(hd * RET_HEAD_DIM, (hd + 1) * RET_HEAD_DIM)
            zh = z[:, sl]
            o_ref[0, :, sl] = (zh * c + pltpu.roll(zh, RET_HEAD_DIM // 2, axis=1) * s).astype(BF16)

    rotary(mm(wqk_ref, 0, RET_W), cq_ref, sq_ref, rq_ref)
    rotary(mm(wqk_ref, RET_W, RET_W), ck_ref, sk_ref, rk_ref)
    rv_ref[0] = mm(w_ref, OFF_RV, RET_W).astype(BF16)
    z = mm(w_ref, OFF_RG, RET_W)
    g_ref[0, :, 0:RET_W] = (z * jax.nn.sigmoid(z)).astype(BF16)
    z = mm(w_ref, OFF_GG, GLA_V)
    g_ref[0, :, RET_W:MIX_W] = (z * jax.nn.sigmoid(z)).astype(BF16)
    gq_ref[0] = (mm(w_ref, OFF_GQ, GLA_K) * (GLA_KEY_DIM ** -0.5)).astype(BF16)
    gk_ref[0] = mm(w_ref, OFF_GK, GLA_K).astype(BF16)
    gv_ref[0] = mm(w_ref, OFF_GV, GLA_V).astype(BF16)
    d = mm(w_ref, OFF_D, 2 * GLA_GATE_RANK).astype(BF16)
    zg = jnp.dot(d, up_ref[...], preferred_element_type=F32) + ub_ref[...]
    ga_ref[0] = (jnp.minimum(zg, 0.0) - jnp.log1p(jnp.exp(-jnp.abs(zg)))) * (1.0 / GLA_GATE_NORM)


def _project(xa, mod, mod_index, layer, n1, wqk, w, up, ub, tables, tm):
    b, seq, d = xa.shape
    tok = lambda n: pl.BlockSpec((1, tm, n), lambda i, j: (i, j, 0))
    tab = pl.BlockSpec((tm, RET_HEAD_DIM), lambda i, j: (j, 0))
    sd = lambda n, dt: jax.ShapeDtypeStruct((b, seq, n), dt)
    return pl.pallas_call(
        _proj_kernel,
        out_shape=(sd(RET_W, BF16), sd(RET_W, BF16), sd(RET_W, BF16), sd(MIX_W, BF16),
                   sd(GLA_K, BF16), sd(GLA_K, BF16), sd(GLA_V, BF16), sd(2 * GLA_K, F32)),
        grid=(b, seq // tm),
        in_specs=[tok(d), pl.BlockSpec((1, 6, d), lambda i, j: (mod_index(i), 0, 0)),
                  _layer_spec(n1.shape[1:], layer), _layer_spec(wqk.shape[1:], layer),
                  _layer_spec(w.shape[1:], layer), _layer_spec(up.shape[1:], layer),
                  _layer_spec(ub.shape[1:], layer), tab, tab, tab, tab],
        out_specs=(tok(RET_W), tok(RET_W), tok(RET_W), tok(MIX_W), tok(GLA_K), tok(GLA_K), tok(GLA_V),
                   tok(2 * GLA_K)),
        compiler_params=pltpu.CompilerParams(dimension_semantics=("parallel", "parallel"),
                                             vmem_limit_bytes=VMEM_LIMIT),
        name="norm_project",
    )(xa, mod, n1, wqk, w, up, ub, *tables)


def _ret_kernel(rd_ref, q_ref, k_ref, v_ref, g_ref, cq_ref, ck_ref, cv_ref, cg_ref,
                y_ref, cy_ref, kv_ref, st_ref, sc_ref):
    t_blk = RET_BLOCK
    dh = RET_HEAD_DIM
    lg = jnp.log1p(-jnp.exp(rd_ref[0]))
    lgf, lgb = lg[0:1, :], lg[1:2, :]
    lgf_h, lgb_h = lgf[:, :dh], lgb[:, :dh]
    ti = lax.broadcasted_iota(jnp.int32, (t_blk, t_blk), 0)
    si = lax.broadcasted_iota(jnp.int32, (t_blk, t_blk), 1)
    diff = (ti - si).astype(F32)
    mask = jnp.exp(jnp.where(diff >= 0, diff * lgf, -diff * lgb))
    tr = lax.broadcasted_iota(jnp.int32, (t_blk, dh), 0).astype(F32)
    q_dec = jnp.concatenate([jnp.exp((tr + 1.0) * lgf_h), jnp.exp((t_blk - tr) * lgb_h)], axis=1)
    k_dec_f = jnp.exp((t_blk - 1.0 - tr) * lgf_h)
    k_dec_b = jnp.exp(tr * lgb_h)
    blk_f = jnp.exp(t_blk * lgf_h)
    blk_b = jnp.exp(t_blk * lgb_h)
    tn_dims = (((0,), (0,)), ((), ()))
    nt_dims = (((1,), (1,)), ((), ()))
    n_ctx = cq_ref.shape[1] // t_blk
    n_lat = q_ref.shape[1] // t_blk
    n_tot = n_ctx + n_lat
    fwd = slice(0, dh)
    bwd = slice(dh, 2 * dh)

    def rows(i):
        return pl.ds(pl.multiple_of(i * t_blk, t_blk), t_blk)

    def kv_block(kr, vr, i, j):
        r = rows(i)
        v = vr[0, r, :].astype(F32)
        vv = jnp.concatenate([(v * k_dec_f).astype(BF16), (v * k_dec_b).astype(BF16)], axis=1)
        kv_ref[j] = lax.dot_general(kr[0, r, :], vv, tn_dims, preferred_element_type=F32)

    def score_block(qr, kr, i, j):
        r = rows(i)
        a = lax.dot_general(qr[0, r, :], kr[0, r, :], nt_dims, preferred_element_type=F32) * mask
        sc_ref[j] = a.astype(BF16)

    def out_block(qr, vr, gr, yr, i, j):
        r = rows(i)
        q, v = qr[0, r, :], vr[0, r, :]
        o = jnp.dot(sc_ref[j], v, preferred_element_type=F32)
        inter = jnp.dot(q, st_ref[j], preferred_element_type=F32) * q_dec
        o = o + inter[:, fwd] + inter[:, bwd]
        mu = jnp.mean(o, axis=-1, keepdims=True)
        dlt = o - mu
        var = jnp.mean(dlt * dlt, axis=-1, keepdims=True)
        yr[0, r, :] = (dlt * lax.rsqrt(var + EPS) * gr[0, r, :].astype(F32)).astype(BF16)

    def for_blocks(n, fn):
        def body(i, carry):
            fn(i)
            return carry
        lax.fori_loop(0, n, body, 0, unroll=math.gcd(n, SCAN_UNROLL))

    for_blocks(n_ctx, lambda i: kv_block(ck_ref, cv_ref, i, i))
    for_blocks(n_lat, lambda i: kv_block(k_ref, v_ref, i, n_ctx + i))

    def step_f(j, s):
        st_ref[j, :, fwd] = s.astype(BF16)
        return blk_f * s + kv_ref[j, :, fwd]

    def step_b(j, s):
        st_ref[j, :, bwd] = s.astype(BF16)
        return blk_b * s + kv_ref[j, :, bwd]

    zero = jnp.zeros((dh, dh), F32)
    lax.fori_loop(0, n_tot, step_f, zero)
    s_b = lax.fori_loop(0, n_ctx, lambda t, s: step_b(n_ctx - 1 - t, s), zero)
    lax.fori_loop(0, n_lat, lambda t, s: step_b(n_tot - 1 - t, s), s_b)

    for_blocks(n_ctx, lambda i: score_block(cq_ref, ck_ref, i, i))
    for_blocks(n_lat, lambda i: score_block(q_ref, k_ref, i, n_ctx + i))
    for_blocks(n_ctx, lambda i: out_block(cq_ref, cv_ref, cg_ref, cy_ref, i, i))
    for_blocks(n_lat, lambda i: out_block(q_ref, v_ref, g_ref, y_ref, i, n_ctx + i))


def _retention(rd, layer, lat, ctx):
    rq, rk, rv, g = lat
    crq, crk, crv, cg = ctx
    b, seq, _ = rq.shape
    cl = crq.shape[1]
    dh = RET_HEAD_DIM
    n_blocks = (seq + cl) // RET_BLOCK
    head = lambda n: pl.BlockSpec((1, n, dh), lambda i, h: (i, 0, h))
    return pl.pallas_call(
        _ret_kernel,
        out_shape=(jax.ShapeDtypeStruct((b, seq, RET_W), BF16), jax.ShapeDtypeStruct((b, cl, RET_W), BF16)),
        grid=(b, N_RET_HEADS),
        in_specs=[pl.BlockSpec((None, 1, 2, RET_BLOCK), lambda i, h: (layer, h, 0, 0)),
                  head(seq), head(seq), head(seq), head(seq), head(cl), head(cl), head(cl), head(cl)],
        out_specs=(head(seq), head(cl)),
        scratch_shapes=[pltpu.VMEM((n_blocks, dh, 2 * dh), F32), pltpu.VMEM((n_blocks, dh, 2 * dh), BF16),
                        pltpu.VMEM((n_blocks, RET_BLOCK, RET_BLOCK), BF16)],
        compiler_params=pltpu.CompilerParams(dimension_semantics=("parallel", "parallel"),
                                             vmem_limit_bytes=VMEM_LIMIT),
        name="retention_scan",
    )(rd, rq, rk, rv, g, crq, crk, crv, cg)


def _gla_kernel(gn_ref, q_ref, k_ref, v_ref, g_ref, af_ref, ab_ref, cq_ref, ck_ref, cv_ref, cg_ref,
                caf_ref, cab_ref, y_ref, cy_ref, c_ref, kv_ref, dec_ref, st_ref, sc_ref):
    t_blk = GLA_BLOCK
    dv = GLA_VAL_DIM
    dk2 = 2 * GLA_KEY_DIM
    assert t_blk == dk2 == dv
    ti = lax.broadcasted_iota(jnp.int32, (t_blk, t_blk), 0)
    si = lax.broadcasted_iota(jnp.int32, (t_blk, t_blk), 1)
    lower = si <= ti
    cum_f = lower.astype(BF16)
    cum_b = (si >= ti).astype(BF16)
    head0 = si < GLA_KEY_DIM
    tn_dims = (((0,), (0,)), ((), ()))
    nt_dims = (((1,), (1,)), ((), ()))

    def rows(i):
        return pl.ds(pl.multiple_of(i * t_blk, t_blk), t_blk)

    def cumulate(cum_mat, a):
        a1 = a.astype(BF16)
        r1 = a - a1.astype(F32)
        a2 = r1.astype(BF16)
        a3 = (r1 - a2.astype(F32)).astype(BF16)
        z = jnp.dot(cum_mat, jnp.concatenate([a1, a2, a3], axis=1), preferred_element_type=F32)
        return z[:, 0:dk2] + z[:, dk2:2 * dk2] + z[:, 2 * dk2:3 * dk2]

    dirs = ((cum_f, GLA_HALF - 1, t_blk - 1), (cum_b, GLA_HALF, 0))

    def cum_block(ars, i, row0):
        for d, (cum_mat, _, _) in enumerate(dirs):
            c_ref[d, rows(row0 + i), :] = cumulate(cum_mat, ars[d][0, rows(i), :])

    def kv_block(kr, vr, i, j, row0):
        r = rows(i)
        cr = rows(row0 + i)
        k = kr[0, r, :].astype(F32)
        v = vr[0, r, :]
        for d, (_, _, edge) in enumerate(dirs):
            c = c_ref[d, cr, :]
            c_edge = c[edge:edge + 1, :]
            k_end = (k * jnp.exp(c_edge - c)).astype(BF16)
            kv = lax.dot_general(v, k_end, tn_dims, preferred_element_type=F32)
            kv_ref[d, j] = jnp.where(head0, kv[0:dv, :], kv[dv:2 * dv, :])
            dec_ref[d, j] = jnp.broadcast_to(jnp.exp(c_edge), (8, dk2))

    def score_block(qr, kr, i, j, row0):
        r = rows(i)
        cr = rows(row0 + i)
        q = qr[0, r, :].astype(F32)
        k = kr[0, r, :].astype(F32)
        q_mid, k_mid = [], []
        for d, (_, mid, _) in enumerate(dirs):
            c = c_ref[d, cr, :]
            c_mid = c[mid:mid + 1, :]
            q_mid.append(q * jnp.exp(c - c_mid))
            k_mid.append((k * jnp.exp(c_mid - c)).astype(BF16))
        for hd, sel in enumerate((head0, jnp.logical_not(head0))):
            sc_f = lax.dot_general(jnp.where(sel, q_mid[0], 0.0).astype(BF16), k_mid[0], nt_dims,
                                   preferred_element_type=F32)
            sc_b = lax.dot_general(jnp.where(sel, q_mid[1], 0.0).astype(BF16), k_mid[1], nt_dims,
                                   preferred_element_type=F32)
            sc_ref[hd, j] = jnp.where(lower, sc_f, sc_b).astype(BF16)

    def out_block(qr, vr, gr, yr, i, j, row0):
        r = rows(i)
        cr = rows(row0 + i)
        q = qr[0, r, :].astype(F32)
        v = vr[0, r, :]
        q_in, s_exp = [], []
        for d in range(2):
            q_in.append((q * jnp.exp(c_ref[d, cr, :])).astype(BF16))
            s = st_ref[d, j]
            zero = jnp.zeros_like(s)
            s_exp.append(jnp.concatenate([jnp.where(head0, s, zero), jnp.where(head0, zero, s)], axis=0))
        o = jnp.concatenate([jnp.dot(sc_ref[hd, j], v[:, hd * dv:(hd + 1) * dv], preferred_element_type=F32)
                             for hd in range(2)], axis=1)
        o = o + lax.dot_general(jnp.concatenate(q_in, axis=1), jnp.concatenate(s_exp, axis=1), nt_dims,
                                preferred_element_type=F32)
        gate = gr[0, r, :].astype(F32)
        gn = gn_ref[...]
        for hd in range(2):
            sl = slice(hd * dv, (hd + 1) * dv)
            oh = o[:, sl]
            ms = jnp.mean(oh * oh, axis=-1, keepdims=True)
            yr[0, r, sl] = (oh * lax.rsqrt(ms + EPS) * gn * gate[:, sl]).astype(BF16)

    def for_blocks(n, fn):
        def body(i, carry):
            fn(i)
            return carry
        lax.fori_loop(0, n, body, 0, unroll=math.gcd(n, SCAN_UNROLL))

    n_ctx = cq_ref.shape[1] // t_blk
    n_lat = q_ref.shape[1] // t_blk
    n_tot = n_ctx + n_lat
    for_blocks(n_ctx, lambda i: cum_block((caf_ref, cab_ref), i, 0))
    for_blocks(n_lat, lambda i: cum_block((af_ref, ab_ref), i, n_ctx))
    for_blocks(n_ctx, lambda i: kv_block(ck_ref, cv_ref, i, i, 0))
    for_blocks(n_lat, lambda i: kv_block(k_ref, v_ref, i, n_ctx + i, n_ctx))

    def step(d, j, s):
        st_ref[d, j] = s.astype(BF16)
        return dec_ref[d, j, 0:1, :] * s + kv_ref[d, j]

    zero = jnp.zeros((dv, dk2), F32)
    lax.fori_loop(0, n_tot, lambda j, s: step(0, j, s), zero)
    s_b = lax.fori_loop(0, n_ctx, lambda t, s: step(1, n_ctx - 1 - t, s), zero)
    lax.fori_loop(0, n_lat, lambda t, s: step(1, n_tot - 1 - t, s), s_b)

    for_blocks(n_ctx, lambda i: score_block(cq_ref, ck_ref, i, i, 0))
    for_blocks(n_lat, lambda i: score_block(q_ref, k_ref, i, n_ctx + i, n_ctx))
    for_blocks(n_ctx, lambda i: out_block(cq_ref, cv_ref, cg_ref, cy_ref, i, i, 0))
    for_blocks(n_lat, lambda i: out_block(q_ref, v_ref, g_ref, y_ref, i, n_ctx + i, n_ctx))


def _gla(gn, layer, lat, ctx):
    gq, gk, gv, g, ga = lat
    cgq, cgk, cgv, cg, cga = ctx
    b, seq, _ = gq.shape
    cl = cgq.shape[1]
    pair_k = 2 * GLA_KEY_DIM
    pair_v = 2 * GLA_VAL_DIM
    n_pairs = N_GLA_HEADS // 2
    n_blocks = (seq + cl) // GLA_BLOCK
    gate_off = RET_W // pair_v
    key = lambda n: pl.BlockSpec((1, n, pair_k), lambda i, p: (i, 0, p))
    key_b = lambda n: pl.BlockSpec((1, n, pair_k), lambda i, p: (i, 0, n_pairs + p))
    val = lambda n: pl.BlockSpec((1, n, pair_v), lambda i, p: (i, 0, p))
    mix = lambda n: pl.BlockSpec((1, n, pair_v), lambda i, p: (i, 0, gate_off + p))
    return pl.pallas_call(
        _gla_kernel,
        out_shape=(jax.ShapeDtypeStruct((b, seq, GLA_V), BF16), jax.ShapeDtypeStruct((b, cl, GLA_V), BF16)),
        grid=(b, n_pairs),
        in_specs=[_layer_spec(gn.shape[1:], layer),
                  key(seq), key(seq), val(seq), mix(seq), key(seq), key_b(seq),
                  key(cl), key(cl), val(cl), mix(cl), key(cl), key_b(cl)],
        out_specs=(val(seq), val(cl)),
        scratch_shapes=[pltpu.VMEM((2, seq + cl, pair_k), F32),
                        pltpu.VMEM((2, n_blocks, GLA_VAL_DIM, pair_k), F32),
                        pltpu.VMEM((2, n_blocks, 8, pair_k), F32),
                        pltpu.VMEM((2, n_blocks, GLA_VAL_DIM, pair_k), BF16),
                        pltpu.VMEM((2, n_blocks, GLA_BLOCK, GLA_BLOCK), BF16)],
        compiler_params=pltpu.CompilerParams(dimension_semantics=("parallel", "parallel"),
                                             vmem_limit_bytes=VMEM_LIMIT),
        name="gla_scan",
    )(gn, gq, gk, gv, g, ga, ga, cgq, cgk, cgv, cg, cga, cga)


def _mlp_kernel(x_ref, yr_ref, yg_ref, mod_ref, n2_ref, wo_ref, w1_ref, w2_ref, fg_ref, o_ref, *,
                final, ff_chunk):
    x = x_ref[0]
    mod = mod_ref[0]
    mix = (jnp.dot(yr_ref[0], wo_ref[0:RET_W, :], preferred_element_type=F32)
           + jnp.dot(yg_ref[0], wo_ref[RET_W:MIX_W, :], preferred_element_type=F32))
    x1 = x + mod[2:3] * mix
    ms = jnp.mean(x1 * x1, axis=-1, keepdims=True)
    h = x1 * lax.rsqrt(ms + EPS) * n2_ref[...]
    hb = (h * (1.0 + mod[4:5]) + mod[3:4]).astype(BF16)
    acc = jnp.zeros(x.shape, F32)
    for c in range(w1_ref.shape[1] // ff_chunk):
        sl = slice(c * ff_chunk, (c + 1) * ff_chunk)
        a = jnp.maximum(jnp.dot(hb, w1_ref[:, sl], preferred_element_type=F32), 0.0)
        acc = acc + jnp.dot((a * a).astype(BF16), w2_ref[sl, :], preferred_element_type=F32)
    x2 = x1 + mod[5:6] * acc
    if final:
        ms = jnp.mean(x2 * x2, axis=-1, keepdims=True)
        x2 = x2 * lax.rsqrt(ms + EPS) * fg_ref[...]
    o_ref[0] = x2


def _out_mlp(xa, yr, yg, mod, mod_index, layer, n2, wo, w1, w2, fg, tm, final):
    b, seq, d = xa.shape
    tok = lambda n: pl.BlockSpec((1, tm, n), lambda i, j: (i, j, 0))
    return pl.pallas_call(
        functools.partial(_mlp_kernel, final=final, ff_chunk=1024),
        out_shape=jax.ShapeDtypeStruct(xa.shape, F32),
        grid=(b, seq // tm),
        in_specs=[tok(d), tok(RET_W), tok(GLA_V), pl.BlockSpec((1, 6, d), lambda i, j: (mod_index(i), 0, 0)),
                  _layer_spec(n2.shape[1:], layer), _layer_spec(wo.shape[1:], layer),
                  _layer_spec(w1.shape[1:], layer), _layer_spec(w2.shape[1:], layer), _const_spec((1, d))],
        out_specs=tok(d),
        compiler_params=pltpu.CompilerParams(dimension_semantics=("parallel", "parallel"),
                                             vmem_limit_bytes=VMEM_LIMIT),
        name="out_mlp",
    )(xa, yr, yg, mod, n2, wo, w1, w2, fg)


def kernel(x, c, ctx, c_ctx, ada_w, ada_b, norm1_g, w_in, ret_decay, gla_gate_up, gla_gate_b, gla_norm_g,
           w_out, norm2_g, w_mlp1, w_mlp2, final_g):
    batch, seq, d = x.shape
    cl = ctx.shape[1]
    depth = ada_w.shape[0]
    assert w_in.shape[2] == IN_W
    mod_rows = 16
    cs = jnp.concatenate([c, c_ctx[None, :], jnp.zeros((mod_rows - batch - 1, d), c.dtype)], axis=0)
    mod = _modulation(cs, ada_w, ada_b).reshape(depth * mod_rows, 6, d)

    lat_tables = _rope_tables(seq)
    scale = RET_HEAD_DIM ** -0.5
    ones = jnp.ones((cl, RET_HEAD_DIM), F32)
    ctx_tables = (ones * scale, ones * 0.0, ones, ones * 0.0)

    w_in_b = w_in.astype(BF16)
    perm = jnp.asarray(_rope_lane_perm(), jnp.int32)
    qk_cols = (jnp.arange(2 * N_RET_HEADS, dtype=jnp.int32)[:, None] * RET_HEAD_DIM + perm[None, :]).reshape(-1)
    wqk = jnp.take(w_in_b, qk_cols, axis=2)
    zero = jnp.zeros((depth, GLA_GATE_RANK, GLA_K), gla_gate_up.dtype)
    up = jnp.concatenate([jnp.concatenate([gla_gate_up[:, 0], zero], axis=2),
                          jnp.concatenate([zero, gla_gate_up[:, 1]], axis=2)], axis=1).astype(BF16)
    ub = gla_gate_b.reshape(depth, 1, 2 * GLA_K)
    wo = w_out.astype(BF16)
    w1 = w_mlp1.astype(BF16)
    w2 = w_mlp2.astype(BF16)
    n1 = norm1_g.reshape(depth, 1, d)
    n2 = norm2_g.reshape(depth, 1, d)
    gn = gla_norm_g.reshape(depth, 1, GLA_VAL_DIM)
    fg = final_g.reshape(1, d)
    rd = jnp.broadcast_to(jnp.swapaxes(ret_decay, 1, 2)[..., None], (depth, N_RET_HEADS, 2, RET_BLOCK))

    for layer in range(depth):
        last = layer == depth - 1
        lat_mod = lambda i, layer=layer: layer * mod_rows + i
        ctx_mod = lambda i, layer=layer: layer * mod_rows + batch
        lat = _project(x, mod, lat_mod, layer, n1, wqk, w_in_b, up, ub, lat_tables, 512)
        cx = _project(ctx, mod, ctx_mod, layer, n1, wqk, w_in_b, up, ub, ctx_tables, cl)
        rq, rk, rv, g, gq, gk, gv, ga = lat
        crq, crk, crv, cg, cgq, cgk, cgv, cga = cx
        yr, cyr = _retention(rd, layer, (rq, rk, rv, g), (crq, crk, crv, cg))
        yg, cyg = _gla(gn, layer, (gq, gk, gv, g, ga), (cgq, cgk, cgv, cg, cga))
        x = _out_mlp(x, yr, yg, mod, lat_mod, layer, n2, wo, w1, w2, fg, 512, last)
        if not last:
            ctx = _out_mlp(ctx, cyr, cyg, mod, ctx_mod, layer, n2, wo, w1, w2, fg, cl, False)
    return x
```

```python
import functools
import math

import jax
import jax.numpy as jnp
from jax import lax
from jax.experimental import pallas as pl
from jax.experimental.pallas import tpu as pltpu

F32 = jnp.float32
BF16 = jnp.bfloat16

GRID_W = 64
N_RET_HEADS = 4
RET_HEAD_DIM = 128
N_GLA_HEADS = 4
GLA_KEY_DIM = 64
GLA_VAL_DIM = 128
GLA_GATE_RANK = 16
GLA_GATE_NORM = 16.0
ROPE_BASE = 10000.0
EPS = 1e-6

RET_W = N_RET_HEADS * RET_HEAD_DIM
GLA_K = N_GLA_HEADS * GLA_KEY_DIM
GLA_V = N_GLA_HEADS * GLA_VAL_DIM
MIX_W = RET_W + GLA_V

OFF_RV = 2 * RET_W
OFF_RG = 3 * RET_W
OFF_GQ = 4 * RET_W
OFF_GK = OFF_GQ + GLA_K
OFF_GV = OFF_GK + GLA_K
OFF_GG = OFF_GV + GLA_V
OFF_D = OFF_GG + GLA_V
IN_W = OFF_D + 2 * GLA_GATE_RANK

RET_BLOCK = 256
GLA_BLOCK = 128
GLA_HALF = GLA_BLOCK // 2

VMEM_LIMIT = 56 * 1024 * 1024


def _layer_spec(shape, layer):
    zeros = (0,) * len(shape)
    return pl.BlockSpec((None,) + tuple(shape), lambda *_: (layer,) + zeros, pipeline_mode=pl.Buffered(1))


def _const_spec(shape):
    zeros = (0,) * len(shape)
    return pl.BlockSpec(shape, lambda *_: zeros, pipeline_mode=pl.Buffered(1))


def _mod_kernel(cs_ref, w_ref, b_ref, o_ref):
    cs = cs_ref[...]
    s = cs * jax.nn.sigmoid(cs)
    o_ref[0] = jnp.dot(s.astype(BF16), w_ref[0].astype(BF16), preferred_element_type=F32) + b_ref[0]


def _modulation(cs, ada_w, ada_b):
    depth, d, n = ada_w.shape
    rows = cs.shape[0]
    tn = 1024
    return pl.pallas_call(
        _mod_kernel,
        out_shape=jax.ShapeDtypeStruct((depth, rows, n), F32),
        grid=(depth, n // tn),
        in_specs=[
            pl.BlockSpec((rows, d), lambda l, j: (0, 0)),
            pl.BlockSpec((1, d, tn), lambda l, j: (l, 0, j)),
            pl.BlockSpec((1, 1, tn), lambda l, j: (l, 0, j)),
        ],
        out_specs=pl.BlockSpec((1, rows, tn), lambda l, j: (l, 0, j)),
        compiler_params=pltpu.CompilerParams(dimension_semantics=("parallel", "parallel"),
                                             vmem_limit_bytes=VMEM_LIMIT),
        name="adaln_modulation",
    )(cs, ada_w, ada_b.reshape(depth, 1, n))


_QUARTER = RET_HEAD_DIM // 4


def _rope_kernel(cq_ref, sq_ref, ck_ref, sk_ref):
    shape = (GRID_W, RET_HEAD_DIM)
    p = lax.broadcasted_iota(jnp.int32, shape, 0).astype(F32)
    lane = lax.broadcasted_iota(jnp.int32, shape, 1)
    freq = (lane & (_QUARTER - 1)).astype(F32)
    ang = p * jnp.exp(freq * (-math.log(ROPE_BASE) / _QUARTER))
    cos = jnp.cos(ang)
    sin = jnp.where((lane & (2 * _QUARTER - 1)) < _QUARTER, -jnp.sin(ang), jnp.sin(ang))
    by_row = lane < 2 * _QUARTER
    scale = RET_HEAD_DIM ** -0.5
    for r in range(cq_ref.shape[0] // GRID_W):
        c_blk = jnp.where(by_row, cos[r:r + 1, :], cos)
        s_blk = jnp.where(by_row, sin[r:r + 1, :], sin)
        rs = slice(r * GRID_W, (r + 1) * GRID_W)
        cq_ref[rs, :] = c_blk * scale
        sq_ref[rs, :] = s_blk * scale
        ck_ref[rs, :] = c_blk
        sk_ref[rs, :] = s_blk


def _rope_tables(seq):
    sds = jax.ShapeDtypeStruct((seq, RET_HEAD_DIM), F32)
    return pl.pallas_call(_rope_kernel, out_shape=(sds, sds, sds, sds), name="rope_tables")()


def _proj_kernel(x_ref, mod_ref, n1_ref, w_ref, up_ref, ub_ref, cq_ref, sq_ref, ck_ref, sk_ref,
                 rq_ref, rk_ref, rv_ref, g_ref, gq_ref, gk_ref, gv_ref, ga_ref):
    x = x_ref[0]
    mod = mod_ref[0]
    ms = jnp.mean(x * x, axis=-1, keepdims=True)
    h = x * lax.rsqrt(ms + EPS) * n1_ref[...]
    h = h * (1.0 + mod[1:2]) + mod[0:1]
    hb = h.astype(BF16)

    def mm(off, n):
        return jnp.dot(hb, w_ref[:, off:off + n], preferred_element_type=F32)

    lane = lax.broadcasted_iota(jnp.int32, (x.shape[0], RET_HEAD_DIM), 1)
    first = (lane & (2 * _QUARTER - 1)) < _QUARTER

    def rotary(z, c_ref, s_ref, o_ref):
        c = c_ref[...]
        s = s_ref[...]
        for hd in range(N_RET_HEADS):
            sl = slice(hd * RET_HEAD_DIM, (hd + 1) * RET_HEAD_DIM)
            zh = z[:, sl]
            partner = jnp.where(first, pltpu.roll(zh, RET_HEAD_DIM - _QUARTER, axis=1),
                                pltpu.roll(zh, _QUARTER, axis=1))
            o_ref[0, :, sl] = (zh * c + partner * s).astype(BF16)

    d = mm(OFF_D, 2 * GLA_GATE_RANK).astype(BF16)
    zg = jnp.dot(d, up_ref[...], preferred_element_type=F32) + ub_ref[...]
    ga_ref[0] = (jnp.minimum(zg, 0.0) - jnp.log1p(jnp.exp(-jnp.abs(zg)))) * (1.0 / GLA_GATE_NORM)
    rotary(mm(0, RET_W), cq_ref, sq_ref, rq_ref)
    rotary(mm(RET_W, RET_W), ck_ref, sk_ref, rk_ref)
    z = mm(OFF_RG, RET_W)
    g_ref[0, :, 0:RET_W] = (z * jax.nn.sigmoid(z)).astype(BF16)
    z = mm(OFF_GG, GLA_V)
    g_ref[0, :, RET_W:MIX_W] = (z * jax.nn.sigmoid(z)).astype(BF16)
    gq_ref[0] = (mm(OFF_GQ, GLA_K) * (GLA_KEY_DIM ** -0.5)).astype(BF16)
    rv_ref[0] = mm(OFF_RV, RET_W).astype(BF16)
    gk_ref[0] = mm(OFF_GK, GLA_K).astype(BF16)
    gv_ref[0] = mm(OFF_GV, GLA_V).astype(BF16)


def _project(xa, mod, mod_index, layer, n1, w, up, ub, tables, tm):
    b, seq, d = xa.shape
    tok = lambda n: pl.BlockSpec((1, tm, n), lambda i, j: (i, j, 0))
    tab = pl.BlockSpec((tm, RET_HEAD_DIM), lambda i, j: (j, 0))
    sd = lambda n, dt: jax.ShapeDtypeStruct((b, seq, n), dt)
    return pl.pallas_call(
        _proj_kernel,
        out_shape=(sd(RET_W, BF16), sd(RET_W, BF16), sd(RET_W, BF16), sd(MIX_W, BF16),
                   sd(GLA_K, BF16), sd(GLA_K, BF16), sd(GLA_V, BF16), sd(2 * GLA_K, F32)),
        grid=(b, seq // tm),
        in_specs=[tok(d), pl.BlockSpec((1, 6, d), lambda i, j: (mod_index(i), 0, 0)),
                  _layer_spec(n1.shape[1:], layer),
                  _layer_spec(w.shape[1:], layer), _layer_spec(up.shape[1:], layer),
                  _layer_spec(ub.shape[1:], layer), tab, tab, tab, tab],
        out_specs=(tok(RET_W), tok(RET_W), tok(RET_W), tok(MIX_W), tok(GLA_K), tok(GLA_K), tok(GLA_V),
                   tok(2 * GLA_K)),
        compiler_params=pltpu.CompilerParams(dimension_semantics=("parallel", "parallel"),
                                             vmem_limit_bytes=VMEM_LIMIT),
        name="norm_project",
    )(xa, mod, n1, w, up, ub, *tables)


def _ret_kernel(rd_ref, q_ref, k_ref, v_ref, g_ref, cq_ref, ck_ref, cv_ref, cg_ref,
                y_ref, cy_ref, kv_ref, st_ref, sc_ref):
    t_blk = RET_BLOCK
    dh = RET_HEAD_DIM
    lg = jnp.log1p(-jnp.exp(rd_ref[0]))
    lgf, lgb = lg[0:1, :], lg[1:2, :]
    lgf_h, lgb_h = lgf[:, :dh], lgb[:, :dh]
    ti = lax.broadcasted_iota(jnp.int32, (t_blk, t_blk), 0)
    si = lax.broadcasted_iota(jnp.int32, (t_blk, t_blk), 1)
    diff = (ti - si).astype(F32)
    mask = jnp.exp(jnp.where(diff >= 0, diff * lgf, -diff * lgb))
    tr = lax.broadcasted_iota(jnp.int32, (t_blk, dh), 0).astype(F32)
    q_dec = jnp.concatenate([jnp.exp((tr + 1.0) * lgf_h), jnp.exp((t_blk - tr) * lgb_h)], axis=1)
    k_dec_f = jnp.exp((t_blk - 1.0 - tr) * lgf_h)
    k_dec_b = jnp.exp(tr * lgb_h)
    blk_f = jnp.exp(t_blk * lgf_h)
    blk_b = jnp.exp(t_blk * lgb_h)
    tn_dims = (((0,), (0,)), ((), ()))
    nt_dims = (((1,), (1,)), ((), ()))
    n_ctx = cq_ref.shape[1] // t_blk
    n_lat = q_ref.shape[1] // t_blk
    n_tot = n_ctx + n_lat
    fwd = slice(0, dh)
    bwd = slice(dh, 2 * dh)

    def rows(i):
        return pl.ds(i * t_blk, t_blk)

    def kv_block(kr, vr, i, j):
        r = rows(i)
        v = vr[0, r, :].astype(F32)
        vv = jnp.concatenate([(v * k_dec_f).astype(BF16), (v * k_dec_b).astype(BF16)], axis=1)
        kv_ref[j] = lax.dot_general(kr[0, r, :], vv, tn_dims, preferred_element_type=F32)

    def score_block(qr, kr, i, j):
        r = rows(i)
        a = lax.dot_general(qr[0, r, :], kr[0, r, :], nt_dims, preferred_element_type=F32) * mask
        sc_ref[j] = a.astype(BF16)

    def out_block(qr, vr, gr, yr, i, j):
        r = rows(i)
        q, v = qr[0, r, :], vr[0, r, :]
        o = jnp.dot(sc_ref[j], v, preferred_element_type=F32)
        inter = jnp.dot(q, st_ref[j], preferred_element_type=F32) * q_dec
        o = o + inter[:, fwd] + inter[:, bwd]
        mu = jnp.mean(o, axis=-1, keepdims=True)
        dlt = o - mu
        var = jnp.mean(dlt * dlt, axis=-1, keepdims=True)
        yr[0, r, :] = (dlt * lax.rsqrt(var + EPS) * gr[0, r, :].astype(F32)).astype(BF16)

    def for_blocks(n, fn):
        for i in range(n):
            fn(i)

    for_blocks(n_ctx, lambda i: kv_block(ck_ref, cv_ref, i, i))
    for_blocks(n_lat, lambda i: kv_block(k_ref, v_ref, i, n_ctx + i))

    def step_f(j, s):
        st_ref[j, :, fwd] = s.astype(BF16)
        return blk_f * s + kv_ref[j, :, fwd]

    def step_b(j, s):
        st_ref[j, :, bwd] = s.astype(BF16)
        return blk_b * s + kv_ref[j, :, bwd]

    zero = jnp.zeros((dh, dh), F32)
    lax.fori_loop(0, n_tot, step_f, zero)
    s_b = lax.fori_loop(0, n_ctx, lambda t, s: step_b(n_ctx - 1 - t, s), zero)
    lax.fori_loop(0, n_lat, lambda t, s: step_b(n_tot - 1 - t, s), s_b)

    for_blocks(n_ctx, lambda i: score_block(cq_ref, ck_ref, i, i))
    for_blocks(n_lat, lambda i: score_block(q_ref, k_ref, i, n_ctx + i))
    for_blocks(n_ctx, lambda i: out_block(cq_ref, cv_ref, cg_ref, cy_ref, i, i))
    for_blocks(n_lat, lambda i: out_block(q_ref, v_ref, g_ref, y_ref, i, n_ctx + i))


def _retention(rd, layer, lat, ctx):
    rq, rk, rv, g = lat
    crq, crk, crv, cg = ctx
    b, seq, _ = rq.shape
    cl = crq.shape[1]
    dh = RET_HEAD_DIM
    n_blocks = (seq + cl) // RET_BLOCK
    head = lambda n: pl.BlockSpec((1, n, dh), lambda i, h: (i, 0, h))
    return pl.pallas_call(
        _ret_kernel,
        out_shape=(jax.ShapeDtypeStruct((b, seq, RET_W), BF16), jax.ShapeDtypeStruct((b, cl, RET_W), BF16)),
        grid=(b, N_RET_HEADS),
        in_specs=[pl.BlockSpec((None, 1, 2, RET_BLOCK), lambda i, h: (layer, h, 0, 0)),
                  head(seq), head(seq), head(seq), head(seq), head(cl), head(cl), head(cl), head(cl)],
        out_specs=(head(seq), head(cl)),
        scratch_shapes=[pltpu.VMEM((n_blocks, dh, 2 * dh), F32), pltpu.VMEM((n_blocks, dh, 2 * dh), BF16),
                        pltpu.VMEM((n_blocks, RET_BLOCK, RET_BLOCK), BF16)],
        compiler_params=pltpu.CompilerParams(dimension_semantics=("parallel", "parallel"),
                                             vmem_limit_bytes=VMEM_LIMIT),
        name="retention_scan",
    )(rd, rq, rk, rv, g, crq, crk, crv, cg)


def _gla_kernel(gn_ref, q_ref, k_ref, v_ref, g_ref, af_ref, ab_ref, cq_ref, ck_ref, cv_ref, cg_ref,
                caf_ref, cab_ref, y_ref, cy_ref, c_ref, kv_ref, dec_ref, st_ref, sc_ref):
    t_blk = GLA_BLOCK
    dv = GLA_VAL_DIM
    dk2 = 2 * GLA_KEY_DIM
    assert t_blk == dk2 == dv
    ti = lax.broadcasted_iota(jnp.int32, (t_blk, t_blk), 0)
    si = lax.broadcasted_iota(jnp.int32, (t_blk, t_blk), 1)
    lower = si <= ti
    cum_f = lower.astype(BF16)
    cum_b = (si >= ti).astype(BF16)
    head0 = si < GLA_KEY_DIM
    tn_dims = (((0,), (0,)), ((), ()))
    nt_dims = (((1,), (1,)), ((), ()))

    def rows(i):
        return pl.ds(i * t_blk, t_blk)

    def cumulate(cum_mat, a):
        a1 = a.astype(BF16)
        r1 = a - a1.astype(F32)
        a2 = r1.astype(BF16)
        a3 = (r1 - a2.astype(F32)).astype(BF16)
        z = jnp.dot(cum_mat, jnp.concatenate([a1, a2, a3], axis=1), preferred_element_type=F32)
        return z[:, 0:dk2] + z[:, dk2:2 * dk2] + z[:, 2 * dk2:3 * dk2]

    dirs = ((cum_f, GLA_HALF - 1, t_blk - 1), (cum_b, GLA_HALF, 0))

    def cum_block(ars, i, row0):
        for d, (cum_mat, _, _) in enumerate(dirs):
            c_ref[d, rows(row0 + i), :] = cumulate(cum_mat, ars[d][0, rows(i), :])

    def kv_block(kr, vr, i, j, row0):
        r = rows(i)
        cr = rows(row0 + i)
        k = kr[0, r, :].astype(F32)
        v = vr[0, r, :]
        for d, (_, _, edge) in enumerate(dirs):
            c = c_ref[d, cr, :]
            c_edge = c[edge:edge + 1, :]
            k_end = (k * jnp.exp(c_edge - c)).astype(BF16)
            kv = lax.dot_general(v, k_end, tn_dims, preferred_element_type=F32)
            kv_ref[d, j] = jnp.where(head0, kv[0:dv, :], kv[dv:2 * dv, :])
            dec_ref[d, j] = jnp.broadcast_to(jnp.exp(c_edge), (8, dk2))

    def score_block(qr, kr, i, j, row0):
        r = rows(i)
        cr = rows(row0 + i)
        q = qr[0, r, :].astype(F32)
        k = kr[0, r, :].astype(F32)
        q_mid, k_mid = [], []
        for d, (_, mid, _) in enumerate(dirs):
            c = c_ref[d, cr, :]
            c_mid = c[mid:mid + 1, :]
            q_mid.append(q * jnp.exp(c - c_mid))
            k_mid.append((k * jnp.exp(c_mid - c)).astype(BF16))
        for hd, sel in enumerate((head0, jnp.logical_not(head0))):
            sc_f = lax.dot_general(jnp.where(sel, q_mid[0], 0.0).astype(BF16), k_mid[0], nt_dims,
                                   preferred_element_type=F32)
            sc_b = lax.dot_general(jnp.where(sel, q_mid[1], 0.0).astype(BF16), k_mid[1], nt_dims,
                                   preferred_element_type=F32)
            sc_ref[hd, j] = jnp.where(lower, sc_f, sc_b).astype(BF16)

    def out_block(qr, vr, gr, yr, i, j, row0):
        r = rows(i)
        cr = rows(row0 + i)
        q = qr[0, r, :].astype(F32)
        v = vr[0, r, :]
        q_in, s_exp = [], []
        for d in range(2):
            q_in.append((q * jnp.exp(c_ref[d, cr, :])).astype(BF16))
            s = st_ref[d, j]
            zero = jnp.zeros_like(s)
            s_exp.append(jnp.concatenate([jnp.where(head0, s, zero), jnp.where(head0, zero, s)], axis=0))
        o = jnp.concatenate([jnp.dot(sc_ref[hd, j], v[:, hd * dv:(hd + 1) * dv], preferred_element_type=F32)
                             for hd in range(2)], axis=1)
        o = o + lax.dot_general(jnp.concatenate(q_in, axis=1), jnp.concatenate(s_exp, axis=1), nt_dims,
                                preferred_element_type=F32)
        gate = gr[0, r, :].astype(F32)
        gn = gn_ref[...]
        for hd in range(2):
            sl = slice(hd * dv, (hd + 1) * dv)
            oh = o[:, sl]
            ms = jnp.mean(oh * oh, axis=-1, keepdims=True)
            yr[0, r, sl] = (oh * lax.rsqrt(ms + EPS) * gn * gate[:, sl]).astype(BF16)

    def for_blocks(n, fn):
        for i in range(n):
            fn(i)

    n_ctx = cq_ref.shape[1] // t_blk
    n_lat = q_ref.shape[1] // t_blk
    n_tot = n_ctx + n_lat
    for_blocks(n_ctx, lambda i: cum_block((caf_ref, cab_ref), i, 0))
    for_blocks(n_lat, lambda i: cum_block((af_ref, ab_ref), i, n_ctx))
    for_blocks(n_ctx, lambda i: kv_block(ck_ref, cv_ref, i, i, 0))
    for_blocks(n_lat, lambda i: kv_block(k_ref, v_ref, i, n_ctx + i, n_ctx))

    def step(d, j, s):
        st_ref[d, j] = s.astype(BF16)
        return dec_ref[d, j, 0:1, :] * s + kv_ref[d, j]

    zero = jnp.zeros((dv, dk2), F32)
    lax.fori_loop(0, n_tot, lambda j, s: step(0, j, s), zero)
    s_b = lax.fori_loop(0, n_ctx, lambda t, s: step(1, n_ctx - 1 - t, s), zero)
    lax.fori_loop(0, n_lat, lambda t, s: step(1, n_tot - 1 - t, s), s_b)

    for_blocks(n_ctx, lambda i: score_block(cq_ref, ck_ref, i, i, 0))
    for_blocks(n_lat, lambda i: score_block(q_ref, k_ref, i, n_ctx + i, n_ctx))
    for_blocks(n_ctx, lambda i: out_block(cq_ref, cv_ref, cg_ref, cy_ref, i, i, 0))
    for_blocks(n_lat, lambda i: out_block(q_ref, v_ref, g_ref, y_ref, i, n_ctx + i, n_ctx))


def _gla(gn, layer, lat, ctx):
    gq, gk, gv, g, ga = lat
    cgq, cgk, cgv, cg, cga = ctx
    b, seq, _ = gq.shape
    cl = cgq.shape[1]
    pair_k = 2 * GLA_KEY_DIM
    pair_v = 2 * GLA_VAL_DIM
    n_pairs = N_GLA_HEADS // 2
    n_blocks = (seq + cl) // GLA_BLOCK
    gate_off = RET_W // pair_v
    key = lambda n: pl.BlockSpec((1, n, pair_k), lambda i, p: (i, 0, p))
    key_b = lambda n: pl.BlockSpec((1, n, pair_k), lambda i, p: (i, 0, n_pairs + p))
    val = lambda n: pl.BlockSpec((1, n, pair_v), lambda i, p: (i, 0, p))
    mix = lambda n: pl.BlockSpec((1, n, pair_v), lambda i, p: (i, 0, gate_off + p))
    return pl.pallas_call(
        _gla_kernel,
        out_shape=(jax.ShapeDtypeStruct((b, seq, GLA_V), BF16), jax.ShapeDtypeStruct((b, cl, GLA_V), BF16)),
        grid=(b, n_pairs),
        in_specs=[_layer_spec(gn.shape[1:], layer),
                  key(seq), key(seq), val(seq), mix(seq), key(seq), key_b(seq),
                  key(cl), key(cl), val(cl), mix(cl), key(cl), key_b(cl)],
        out_specs=(val(seq), val(cl)),
        scratch_shapes=[pltpu.VMEM((2, seq + cl, pair_k), F32),
                        pltpu.VMEM((2, n_blocks, GLA_VAL_DIM, pair_k), F32),
                        pltpu.VMEM((2, n_blocks, 8, pair_k), F32),
                        pltpu.VMEM((2, n_blocks, GLA_VAL_DIM, pair_k), BF16),
                        pltpu.VMEM((2, n_blocks, GLA_BLOCK, GLA_BLOCK), BF16)],
        compiler_params=pltpu.CompilerParams(dimension_semantics=("parallel", "parallel"),
                                             vmem_limit_bytes=VMEM_LIMIT),
        name="gla_scan",
    )(gn, gq, gk, gv, g, ga, ga, cgq, cgk, cgv, cg, cga, cga)


def _mlp_kernel(x_ref, yr_ref, yg_ref, mod_ref, n2_ref, wo_ref, w1_ref, w2_ref, fg_ref, o_ref, *,
                final, ff_chunk):
    x = x_ref[0]
    mod = mod_ref[0]
    mix = (jnp.dot(yr_ref[0], wo_ref[0:RET_W, :], preferred_element_type=F32)
           + jnp.dot(yg_ref[0], wo_ref[RET_W:MIX_W, :], preferred_element_type=F32))
    x1 = x + mod[2:3] * mix
    ms = jnp.mean(x1 * x1, axis=-1, keepdims=True)
    h = x1 * lax.rsqrt(ms + EPS) * n2_ref[...]
    hb = (h * (1.0 + mod[4:5]) + mod[3:4]).astype(BF16)
    acc = jnp.zeros(x.shape, F32)
    for c in range(w1_ref.shape[1] // ff_chunk):
        sl = slice(c * ff_chunk, (c + 1) * ff_chunk)
        a = jnp.maximum(jnp.dot(hb, w1_ref[:, sl], preferred_element_type=F32), 0.0)
        acc = acc + jnp.dot((a * a).astype(BF16), w2_ref[sl, :], preferred_element_type=F32)
    x2 = x1 + mod[5:6] * acc
    if final:
        ms = jnp.mean(x2 * x2, axis=-1, keepdims=True)
        x2 = x2 * lax.rsqrt(ms + EPS) * fg_ref[...]
    o_ref[0] = x2


def _out_mlp(xa, yr, yg, mod, mod_index, layer, n2, wo, w1, w2, fg, tm, final):
    b, seq, d = xa.shape
    tok = lambda n: pl.BlockSpec((1, tm, n), lambda i, j: (i, j, 0))
    return pl.pallas_call(
        functools.partial(_mlp_kernel, final=final, ff_chunk=1024),
        out_shape=jax.ShapeDtypeStruct(xa.shape, F32),
        grid=(b, seq // tm),
        in_specs=[tok(d), tok(RET_W), tok(GLA_V), pl.BlockSpec((1, 6, d), lambda i, j: (mod_index(i), 0, 0)),
                  _layer_spec(n2.shape[1:], layer), _layer_spec(wo.shape[1:], layer),
                  _layer_spec(w1.shape[1:], layer), _layer_spec(w2.shape[1:], layer), _const_spec((1, d))],
        out_specs=tok(d),
        compiler_params=pltpu.CompilerParams(dimension_semantics=("parallel", "parallel"),
                                             vmem_limit_bytes=VMEM_LIMIT),
        name="out_mlp",
    )(xa, yr, yg, mod, n2, wo, w1, w2, fg)


def kernel(x, c, ctx, c_ctx, ada_w, ada_b, norm1_g, w_in, ret_decay, gla_gate_up, gla_gate_b, gla_norm_g,
           w_out, norm2_g, w_mlp1, w_mlp2, final_g):
    batch, seq, d = x.shape
    cl = ctx.shape[1]
    depth = ada_w.shape[0]
    assert w_in.shape[2] == IN_W
    mod_rows = 16
    cs = jnp.concatenate([c, c_ctx[None, :], jnp.zeros((mod_rows - batch - 1, d), c.dtype)], axis=0)
    mod = _modulation(cs, ada_w, ada_b).reshape(depth * mod_rows, 6, d)

    lat_tables = _rope_tables(seq)
    scale = RET_HEAD_DIM ** -0.5
    ones = jnp.ones((cl, RET_HEAD_DIM), F32)
    ctx_tables = (ones * scale, ones * 0.0, ones, ones * 0.0)

    w_in_b = w_in.astype(BF16)
    zero = jnp.zeros((depth, GLA_GATE_RANK, GLA_K), gla_gate_up.dtype)
    up = jnp.concatenate([jnp.concatenate([gla_gate_up[:, 0], zero], axis=2),
                          jnp.concatenate([zero, gla_gate_up[:, 1]], axis=2)], axis=1).astype(BF16)
    ub = gla_gate_b.reshape(depth, 1, 2 * GLA_K)
    wo = w_out.astype(BF16)
    w1 = w_mlp1.astype(BF16)
    w2 = w_mlp2.astype(BF16)
    n1 = norm1_g.reshape(depth, 1, d)
    n2 = norm2_g.reshape(depth, 1, d)
    gn = gla_norm_g.reshape(depth, 1, GLA_VAL_DIM)
    fg = final_g.reshape(1, d)
    rd = jnp.broadcast_to(jnp.swapaxes(ret_decay, 1, 2)[..., None], (depth, N_RET_HEADS, 2, RET_BLOCK))

    for layer in range(depth):
        last = layer == depth - 1
        lat_mod = lambda i, layer=layer: layer * mod_rows + i
        ctx_mod = lambda i, layer=layer: layer * mod_rows + batch
        lat = _project(x, mod, lat_mod, layer, n1, w_in_b, up, ub, lat_tables, 512)
        cx = _project(ctx, mod, ctx_mod, layer, n1, w_in_b, up, ub, ctx_tables, cl)
        rq, rk, rv, g, gq, gk, gv, ga = lat
        crq, crk, crv, cg, cgq, cgk, cgv, cga = cx
        yr, cyr = _retention(rd, layer, (rq, rk, rv, g), (crq, crk, crv, cg))
        yg, cyg = _gla(gn, layer, (gq, gk, gv, g, ga), (cgq, cgk, cgv, cg, cga))
        x = _out_mlp(x, yr, yg, mod, lat_mod, layer, n2, wo, w1, w2, fg, 512, last)
        if not last:
            ctx = _out_mlp(ctx, cyr, cyg, mod, ctx_mod, layer, n2, wo, w1, w2, fg, cl, False)
    return x
```

```python
import functools
import math

import jax
import jax.numpy as jnp
from jax import lax
from jax.experimental import pallas as pl
from jax.experimental.pallas import tpu as pltpu

F32 = jnp.float32
BF16 = jnp.bfloat16

GRID_W = 64
N_RET_HEADS = 4
RET_HEAD_DIM = 128
N_GLA_HEADS = 4
GLA_KEY_DIM = 64
GLA_VAL_DIM = 128
GLA_GATE_RANK = 16
GLA_GATE_NORM = 16.0
ROPE_BASE = 10000.0
EPS = 1e-6

RET_W = N_RET_HEADS * RET_HEAD_DIM
GLA_K = N_GLA_HEADS * GLA_KEY_DIM
GLA_V = N_GLA_HEADS * GLA_VAL_DIM
MIX_W = RET_W + GLA_V

OFF_RV = 2 * RET_W
OFF_RG = 3 * RET_W
OFF_GQ = 4 * RET_W
OFF_GK = OFF_GQ + GLA_K
OFF_GV = OFF_GK + GLA_K
OFF_GG = OFF_GV + GLA_V
OFF_D = OFF_GG + GLA_V
IN_W = OFF_D + 2 * GLA_GATE_RANK

RET_BLOCK = 256
GLA_BLOCK = 128
GLA_HALF = GLA_BLOCK // 2

ROW_TILE = 512
LOG2E = 1.4426950408889634

VMEM_LIMIT = 56 * 1024 * 1024


def _layer_spec(shape, layer):
    zeros = (0,) * len(shape)
    return pl.BlockSpec((None,) + tuple(shape), lambda *_: (layer,) + zeros, pipeline_mode=pl.Buffered(1))


def _const_spec(shape):
    zeros = (0,) * len(shape)
    return pl.BlockSpec(shape, lambda *_: zeros, pipeline_mode=pl.Buffered(1))


def _mod_kernel(cs_ref, w_ref, b_ref, o_ref):
    cs = cs_ref[...]
    s = cs * jax.nn.sigmoid(cs)
    o_ref[0] = jnp.dot(s.astype(BF16), w_ref[0].astype(BF16), preferred_element_type=F32) + b_ref[0]


def _modulation(cs, ada_w, ada_b):
    depth, d, n = ada_w.shape
    rows = cs.shape[0]
    tn = 1024
    return pl.pallas_call(
        _mod_kernel,
        out_shape=jax.ShapeDtypeStruct((depth, rows, n), F32),
        grid=(depth, n // tn),
        in_specs=[
            pl.BlockSpec((rows, d), lambda l, j: (0, 0)),
            pl.BlockSpec((1, d, tn), lambda l, j: (l, 0, j)),
            pl.BlockSpec((1, 1, tn), lambda l, j: (l, 0, j)),
        ],
        out_specs=pl.BlockSpec((1, rows, tn), lambda l, j: (l, 0, j)),
        compiler_params=pltpu.CompilerParams(dimension_semantics=("parallel", "parallel"),
                                             vmem_limit_bytes=VMEM_LIMIT),
        name="adaln_modulation",
    )(cs, ada_w, ada_b.reshape(depth, 1, n))


_QUARTER = RET_HEAD_DIM // 4


def _rope_kernel(cq_ref, sq_ref, ck_ref, sk_ref):
    shape = (GRID_W, RET_HEAD_DIM)
    p = lax.broadcasted_iota(jnp.int32, shape, 0).astype(F32)
    lane = lax.broadcasted_iota(jnp.int32, shape, 1)
    freq = (lane & (_QUARTER - 1)).astype(F32)
    ang = p * jnp.exp(freq * (-math.log(ROPE_BASE) / _QUARTER))
    cos = jnp.cos(ang)
    sin = jnp.where((lane & (2 * _QUARTER - 1)) < _QUARTER, -jnp.sin(ang), jnp.sin(ang))
    by_row = lane < 2 * _QUARTER
    scale = RET_HEAD_DIM ** -0.5
    for r in range(cq_ref.shape[0] // GRID_W):
        c_blk = jnp.where(by_row, cos[r:r + 1, :], cos)
        s_blk = jnp.where(by_row, sin[r:r + 1, :], sin)
        rs = slice(r * GRID_W, (r + 1) * GRID_W)
        cq_ref[rs, :] = c_blk * scale
        sq_ref[rs, :] = s_blk * scale
        ck_ref[rs, :] = c_blk
        sk_ref[rs, :] = s_blk


def _rope_tables(seq):
    sds = jax.ShapeDtypeStruct((seq, RET_HEAD_DIM), F32)
    return pl.pallas_call(_rope_kernel, out_shape=(sds, sds, sds, sds), name="rope_tables")()


def _silu(z):
    hz = 0.5 * z
    return hz * (1.0 + jnp.tanh(hz))


def _proj_kernel(x_ref, mod_ref, n1_ref, w_ref, up_ref, ub_ref, cq_ref, sq_ref, ck_ref, sk_ref,
                 rq_ref, rk_ref, rv_ref, g_ref, gq_ref, gk_ref, gv_ref, ga_ref):
    x = x_ref[0]
    mod = mod_ref[0]
    gain = n1_ref[...] * (1.0 + mod[1:2])
    ms = jnp.mean(x * x, axis=-1, keepdims=True)
    hb = (x * lax.rsqrt(ms + EPS) * gain + mod[0:1]).astype(BF16)

    def mm(off, n):
        return jnp.dot(hb, w_ref[:, off:off + n], preferred_element_type=F32)

    lane = lax.broadcasted_iota(jnp.int32, (x.shape[0], RET_HEAD_DIM), 1)
    first = (lane & (2 * _QUARTER - 1)) < _QUARTER

    def rotary(z, c_ref, s_ref, o_ref):
        c = c_ref[...]
        s = s_ref[...]
        for hd in range(N_RET_HEADS):
            sl = slice(hd * RET_HEAD_DIM, (hd + 1) * RET_HEAD_DIM)
            zh = z[:, sl]
            partner = jnp.where(first, pltpu.roll(zh, RET_HEAD_DIM - _QUARTER, axis=1),
                                pltpu.roll(zh, _QUARTER, axis=1))
            o_ref[0, :, sl] = (zh * c + partner * s).astype(BF16)

    d = mm(OFF_D, 2 * GLA_GATE_RANK).astype(BF16)
    zg = jnp.dot(d, up_ref[...], preferred_element_type=F32) + ub_ref[...]
    soft = jnp.log2(1.0 + jnp.exp2(jnp.abs(zg) * -LOG2E))
    ga_ref[0] = jnp.minimum(zg, 0.0) * (LOG2E / GLA_GATE_NORM) - soft * (1.0 / GLA_GATE_NORM)
    rv_ref[0] = mm(OFF_RV, RET_W).astype(BF16)
    rotary(mm(0, RET_W), cq_ref, sq_ref, rq_ref)
    gk_ref[0] = mm(OFF_GK, GLA_K).astype(BF16)
    rotary(mm(RET_W, RET_W), ck_ref, sk_ref, rk_ref)
    gv_ref[0] = mm(OFF_GV, GLA_V).astype(BF16)
    g_ref[0, :, 0:RET_W] = _silu(mm(OFF_RG, RET_W)).astype(BF16)
    gq_ref[0] = (mm(OFF_GQ, GLA_K) * (GLA_KEY_DIM ** -0.5)).astype(BF16)
    g_ref[0, :, RET_W:MIX_W] = _silu(mm(OFF_GG, GLA_V)).astype(BF16)


def _project(xa, mod, mod_index, layer, n1, w, up, ub, tables, tm):
    b, seq, d = xa.shape
    tok = lambda n: pl.BlockSpec((1, tm, n), lambda i, j: (i, j, 0))
    tab = pl.BlockSpec((tm, RET_HEAD_DIM), lambda i, j: (j, 0))
    sd = lambda n, dt: jax.ShapeDtypeStruct((b, seq, n), dt)
    return pl.pallas_call(
        _proj_kernel,
        out_shape=(sd(RET_W, BF16), sd(RET_W, BF16), sd(RET_W, BF16), sd(MIX_W, BF16),
                   sd(GLA_K, BF16), sd(GLA_K, BF16), sd(GLA_V, BF16), sd(2 * GLA_K, F32)),
        grid=(b, seq // tm),
        in_specs=[tok(d), pl.BlockSpec((1, 6, d), lambda i, j: (mod_index(i), 0, 0)),
                  _layer_spec(n1.shape[1:], layer),
                  _layer_spec(w.shape[1:], layer), _layer_spec(up.shape[1:], layer),
                  _layer_spec(ub.shape[1:], layer), tab, tab, tab, tab],
        out_specs=(tok(RET_W), tok(RET_W), tok(RET_W), tok(MIX_W), tok(GLA_K), tok(GLA_K), tok(GLA_V),
                   tok(2 * GLA_K)),
        compiler_params=pltpu.CompilerParams(dimension_semantics=("parallel", "parallel"),
                                             vmem_limit_bytes=VMEM_LIMIT),
        name="norm_project",
    )(xa, mod, n1, w, up, ub, *tables)


def _ret_kernel(rd_ref, q_ref, k_ref, v_ref, g_ref, cq_ref, ck_ref, cv_ref, cg_ref,
                y_ref, cy_ref, kv_ref, st_ref, sc_ref):
    t_blk = RET_BLOCK
    dh = RET_HEAD_DIM
    lg = jnp.log1p(-jnp.exp(rd_ref[0]))
    lgf, lgb = lg[0:1, :], lg[1:2, :]
    lgf_h, lgb_h = lgf[:, :dh], lgb[:, :dh]
    ti = lax.broadcasted_iota(jnp.int32, (t_blk, t_blk), 0)
    si = lax.broadcasted_iota(jnp.int32, (t_blk, t_blk), 1)
    diff = (ti - si).astype(F32)
    mask = jnp.exp(jnp.where(diff >= 0, diff * lgf, -diff * lgb))
    tr = lax.broadcasted_iota(jnp.int32, (t_blk, dh), 0).astype(F32)
    q_dec = jnp.concatenate([jnp.exp((tr + 1.0) * lgf_h), jnp.exp((t_blk - tr) * lgb_h)], axis=1)
    k_dec_f = jnp.exp((t_blk - 1.0 - tr) * lgf_h)
    k_dec_b = jnp.exp(tr * lgb_h)
    blk_f = jnp.exp(t_blk * lgf_h)
    blk_b = jnp.exp(t_blk * lgb_h)
    tn_dims = (((0,), (0,)), ((), ()))
    nt_dims = (((1,), (1,)), ((), ()))
    n_ctx = cq_ref.shape[1] // t_blk
    n_lat = q_ref.shape[1] // t_blk
    n_tot = n_ctx + n_lat
    fwd = slice(0, dh)
    bwd = slice(dh, 2 * dh)

    def rows(i):
        return pl.ds(i * t_blk, t_blk)

    def kv_block(kr, vr, i, j):
        r = rows(i)
        v = vr[0, r, :].astype(F32)
        vv = jnp.concatenate([(v * k_dec_f).astype(BF16), (v * k_dec_b).astype(BF16)], axis=1)
        kv_ref[j] = lax.dot_general(kr[0, r, :], vv, tn_dims, preferred_element_type=F32)

    def score_block(qr, kr, i, j):
        r = rows(i)
        a = lax.dot_general(qr[0, r, :], kr[0, r, :], nt_dims, preferred_element_type=F32) * mask
        sc_ref[j] = a.astype(BF16)

    def out_block(qr, vr, gr, yr, i, j):
        r = rows(i)
        q, v = qr[0, r, :], vr[0, r, :]
        o = jnp.dot(sc_ref[j], v, preferred_element_type=F32)
        inter = jnp.dot(q, st_ref[j], preferred_element_type=F32) * q_dec
        o = o + inter[:, fwd] + inter[:, bwd]
        mu = jnp.mean(o, axis=-1, keepdims=True)
        dlt = o - mu
        var = jnp.mean(dlt * dlt, axis=-1, keepdims=True)
        yr[0, r, :] = (dlt * lax.rsqrt(var + EPS) * gr[0, r, :].astype(F32)).astype(BF16)

    def for_blocks(n, fn):
        for i in range(n):
            fn(i)

    for_blocks(n_ctx, lambda i: kv_block(ck_ref, cv_ref, i, i))
    for_blocks(n_lat, lambda i: kv_block(k_ref, v_ref, i, n_ctx + i))

    def step_f(j, s):
        st_ref[j, :, fwd] = s.astype(BF16)
        return blk_f * s + kv_ref[j, :, fwd]

    def step_b(j, s):
        st_ref[j, :, bwd] = s.astype(BF16)
        return blk_b * s + kv_ref[j, :, bwd]

    zero = jnp.zeros((dh, dh), F32)
    lax.fori_loop(0, n_tot, step_f, zero)
    s_b = lax.fori_loop(0, n_ctx, lambda t, s: step_b(n_ctx - 1 - t, s), zero)
    lax.fori_loop(0, n_lat, lambda t, s: step_b(n_tot - 1 - t, s), s_b)

    for_blocks(n_ctx, lambda i: score_block(cq_ref, ck_ref, i, i))
    for_blocks(n_lat, lambda i: score_block(q_ref, k_ref, i, n_ctx + i))
    for_blocks(n_ctx, lambda i: out_block(cq_ref, cv_ref, cg_ref, cy_ref, i, i))
    for_blocks(n_lat, lambda i: out_block(q_ref, v_ref, g_ref, y_ref, i, n_ctx + i))


def _retention(rd, layer, lat, ctx):
    rq, rk, rv, g = lat
    crq, crk, crv, cg = ctx
    b, seq, _ = rq.shape
    cl = crq.shape[1]
    dh = RET_HEAD_DIM
    n_blocks = (seq + cl) // RET_BLOCK
    head = lambda n: pl.BlockSpec((1, n, dh), lambda i, h: (i, 0, h))
    return pl.pallas_call(
        _ret_kernel,
        out_shape=(jax.ShapeDtypeStruct((b, seq, RET_W), BF16), jax.ShapeDtypeStruct((b, cl, RET_W), BF16)),
        grid=(b, N_RET_HEADS),
        in_specs=[pl.BlockSpec((None, 1, 2, RET_BLOCK), lambda i, h: (layer, h, 0, 0)),
                  head(seq), head(seq), head(seq), head(seq), head(cl), head(cl), head(cl), head(cl)],
        out_specs=(head(seq), head(cl)),
        scratch_shapes=[pltpu.VMEM((n_blocks, dh, 2 * dh), F32), pltpu.VMEM((n_blocks, dh, 2 * dh), BF16),
                        pltpu.VMEM((n_blocks, RET_BLOCK, RET_BLOCK), BF16)],
        compiler_params=pltpu.CompilerParams(dimension_semantics=("parallel", "parallel"),
                                             vmem_limit_bytes=VMEM_LIMIT),
        name="retention_scan",
    )(rd, rq, rk, rv, g, crq, crk, crv, cg)


def _gla_kernel(gn_ref, q_ref, k_ref, v_ref, g_ref, af_ref, ab_ref, cq_ref, ck_ref, cv_ref, cg_ref,
                caf_ref, cab_ref, y_ref, cy_ref, c_ref, kv_ref, dec_ref, st_ref, sc_ref, qin_ref):
    t_blk = GLA_BLOCK
    dv = GLA_VAL_DIM
    dk2 = 2 * GLA_KEY_DIM
    assert t_blk == dk2 == dv
    ti = lax.broadcasted_iota(jnp.int32, (t_blk, t_blk), 0)
    si = lax.broadcasted_iota(jnp.int32, (t_blk, t_blk), 1)
    lower = si <= ti
    cum_f = lower.astype(BF16)
    cum_b = (si >= ti).astype(BF16)
    head0 = si < GLA_KEY_DIM
    tn_dims = (((0,), (0,)), ((), ()))
    nt_dims = (((1,), (1,)), ((), ()))

    def rows(i):
        return pl.ds(i * t_blk, t_blk)

    def cumulate(cum_mat, a):
        a1 = a.astype(BF16)
        r1 = a - a1.astype(F32)
        a2 = r1.astype(BF16)
        a3 = (r1 - a2.astype(F32)).astype(BF16)
        z = jnp.dot(cum_mat, jnp.concatenate([a1, a2, a3], axis=1), preferred_element_type=F32)
        return z[:, 0:dk2] + z[:, dk2:2 * dk2] + z[:, 2 * dk2:3 * dk2]

    dirs = ((cum_f, GLA_HALF - 1, t_blk - 1), (cum_b, GLA_HALF, 0))

    def cum_block(ars, i, row0):
        for d, (cum_mat, _, _) in enumerate(dirs):
            c_ref[d, rows(row0 + i), :] = cumulate(cum_mat, ars[d][0, rows(i), :])

    def mid_block(qr, kr, vr, i, j, row0):
        r = rows(i)
        cr = rows(row0 + i)
        q = qr[0, r, :].astype(F32)
        k = kr[0, r, :].astype(F32)
        v = vr[0, r, :]
        sc = [[None, None], [None, None]]
        for d, (_, mid, edge) in enumerate(dirs):
            c = c_ref[d, cr, :]
            c_mid = c[mid:mid + 1, :]
            c_edge = c[edge:edge + 1, :]
            q_mid = q * jnp.exp2(c - c_mid)
            k_mid = k * jnp.exp2(c_mid - c)
            k_mid_b = k_mid.astype(BF16)
            k_end = (k_mid * jnp.exp2(c_edge - c_mid)).astype(BF16)
            qin_ref[d, cr, :] = (q_mid * jnp.exp2(c_mid)).astype(BF16)
            kv = lax.dot_general(v, k_end, tn_dims, preferred_element_type=F32)
            kv_ref[d, j] = jnp.where(head0, kv[0:dv, :], kv[dv:2 * dv, :])
            dec_ref[d, j] = jnp.broadcast_to(jnp.exp2(c_edge), (8, dk2))
            q_mid_b = q_mid.astype(BF16)
            zero = jnp.zeros_like(q_mid_b)
            for hd in range(2):
                q_head = jnp.where(head0, q_mid_b, zero) if hd == 0 else jnp.where(head0, zero, q_mid_b)
                sc[hd][d] = lax.dot_general(q_head, k_mid_b, nt_dims, preferred_element_type=F32)
        for hd in range(2):
            sc_ref[hd, j] = jnp.where(lower, sc[hd][0], sc[hd][1]).astype(BF16)

    def out_block(qr, vr, gr, yr, i, j, row0):
        r = rows(i)
        cr = rows(row0 + i)
        v = vr[0, r, :]
        s_exp = []
        for d in range(2):
            s = st_ref[d, j]
            zero = jnp.zeros_like(s)
            s_exp.append(jnp.concatenate([jnp.where(head0, s, zero), jnp.where(head0, zero, s)], axis=0))
        q_in = jnp.concatenate([qin_ref[0, cr, :], qin_ref[1, cr, :]], axis=1)
        o = jnp.concatenate([jnp.dot(sc_ref[hd, j], v[:, hd * dv:(hd + 1) * dv], preferred_element_type=F32)
                             for hd in range(2)], axis=1)
        o = o + lax.dot_general(q_in, jnp.concatenate(s_exp, axis=1), nt_dims, preferred_element_type=F32)
        gate = gr[0, r, :].astype(F32)
        gn = gn_ref[...]
        for hd in range(2):
            sl = slice(hd * dv, (hd + 1) * dv)
            oh = o[:, sl]
            ms = jnp.mean(oh * oh, axis=-1, keepdims=True)
            yr[0, r, sl] = (oh * lax.rsqrt(ms + EPS) * gn * gate[:, sl]).astype(BF16)

    def for_blocks(n, fn):
        for i in range(n):
            fn(i)

    n_ctx = cq_ref.shape[1] // t_blk
    n_lat = q_ref.shape[1] // t_blk
    n_tot = n_ctx + n_lat
    for_blocks(n_ctx, lambda i: cum_block((caf_ref, cab_ref), i, 0))
    for_blocks(n_lat, lambda i: cum_block((af_ref, ab_ref), i, n_ctx))
    for_blocks(n_ctx, lambda i: mid_block(cq_ref, ck_ref, cv_ref, i, i, 0))
    for_blocks(n_lat, lambda i: mid_block(q_ref, k_ref, v_ref, i, n_ctx + i, n_ctx))

    def step(d, j, s):
        st_ref[d, j] = s.astype(BF16)
        return dec_ref[d, j, 0:1, :] * s + kv_ref[d, j]

    zero = jnp.zeros((dv, dk2), F32)
    lax.fori_loop(0, n_tot, lambda j, s: step(0, j, s), zero)
    s_b = lax.fori_loop(0, n_ctx, lambda t, s: step(1, n_ctx - 1 - t, s), zero)
    lax.fori_loop(0, n_lat, lambda t, s: step(1, n_tot - 1 - t, s), s_b)

    for_blocks(n_ctx, lambda i: out_block(cq_ref, cv_ref, cg_ref, cy_ref, i, i, 0))
    for_blocks(n_lat, lambda i: out_block(q_ref, v_ref, g_ref, y_ref, i, n_ctx + i, n_ctx))


def _gla(gn, layer, lat, ctx):
    gq, gk, gv, g, ga = lat
    cgq, cgk, cgv, cg, cga = ctx
    b, seq, _ = gq.shape
    cl = cgq.shape[1]
    pair_k = 2 * GLA_KEY_DIM
    pair_v = 2 * GLA_VAL_DIM
    n_pairs = N_GLA_HEADS // 2
    n_blocks = (seq + cl) // GLA_BLOCK
    gate_off = RET_W // pair_v
    key = lambda n: pl.BlockSpec((1, n, pair_k), lambda i, p: (i, 0, p))
    key_b = lambda n: pl.BlockSpec((1, n, pair_k), lambda i, p: (i, 0, n_pairs + p))
    val = lambda n: pl.BlockSpec((1, n, pair_v), lambda i, p: (i, 0, p))
    mix = lambda n: pl.BlockSpec((1, n, pair_v), lambda i, p: (i, 0, gate_off + p))
    return pl.pallas_call(
        _gla_kernel,
        out_shape=(jax.ShapeDtypeStruct((b, seq, GLA_V), BF16), jax.ShapeDtypeStruct((b, cl, GLA_V), BF16)),
        grid=(b, n_pairs),
        in_specs=[_layer_spec(gn.shape[1:], layer),
                  key(seq), key(seq), val(seq), mix(seq), key(seq), key_b(seq),
                  key(cl), key(cl), val(cl), mix(cl), key(cl), key_b(cl)],
        out_specs=(val(seq), val(cl)),
        scratch_shapes=[pltpu.VMEM((2, seq + cl, pair_k), F32),
                        pltpu.VMEM((2, n_blocks, GLA_VAL_DIM, pair_k), F32),
                        pltpu.VMEM((2, n_blocks, 8, pair_k), F32),
                        pltpu.VMEM((2, n_blocks, GLA_VAL_DIM, pair_k), BF16),
                        pltpu.VMEM((2, n_blocks, GLA_BLOCK, GLA_BLOCK), BF16),
                        pltpu.VMEM((2, seq + cl, pair_k), BF16)],
        compiler_params=pltpu.CompilerParams(dimension_semantics=("parallel", "parallel"),
                                             vmem_limit_bytes=VMEM_LIMIT),
        name="gla_scan",
    )(gn, gq, gk, gv, g, ga, ga, cgq, cgk, cgv, cg, cga, cga)


def _mlp_kernel(x_ref, yr_ref, yg_ref, mod_ref, n2_ref, wo_ref, w1_ref, w2_ref, fg_ref, o_ref, *,
                final, ff_chunk):
    mod = mod_ref[0]
    mix = (jnp.dot(yr_ref[0], wo_ref[0:RET_W, :], preferred_element_type=F32)
           + jnp.dot(yg_ref[0], wo_ref[RET_W:MIX_W, :], preferred_element_type=F32))
    x1 = x_ref[0] + mod[2:3] * mix
    gain = n2_ref[...] * (1.0 + mod[4:5])
    ms = jnp.mean(x1 * x1, axis=-1, keepdims=True)
    hb = (x1 * lax.rsqrt(ms + EPS) * gain + mod[3:4]).astype(BF16)
    acc = jnp.zeros(x1.shape, F32)
    for c in range(w1_ref.shape[1] // ff_chunk):
        sl = slice(c * ff_chunk, (c + 1) * ff_chunk)
        a = jnp.maximum(jnp.dot(hb, w1_ref[:, sl], preferred_element_type=F32), 0.0)
        acc = acc + jnp.dot((a * a).astype(BF16), w2_ref[sl, :], preferred_element_type=F32)
    x2 = x1 + mod[5:6] * acc
    if final:
        ms = jnp.mean(x2 * x2, axis=-1, keepdims=True)
        x2 = x2 * lax.rsqrt(ms + EPS) * fg_ref[...]
    o_ref[0] = x2


def _out_mlp(xa, yr, yg, mod, mod_index, layer, n2, wo, w1, w2, fg, tm, final):
    b, seq, d = xa.shape
    tok = lambda n: pl.BlockSpec((1, tm, n), lambda i, j: (i, j, 0))
    return pl.pallas_call(
        functools.partial(_mlp_kernel, final=final, ff_chunk=1024),
        out_shape=jax.ShapeDtypeStruct(xa.shape, F32),
        grid=(b, seq // tm),
        in_specs=[tok(d), tok(RET_W), tok(GLA_V), pl.BlockSpec((1, 6, d), lambda i, j: (mod_index(i), 0, 0)),
                  _layer_spec(n2.shape[1:], layer), _layer_spec(wo.shape[1:], layer),
                  _layer_spec(w1.shape[1:], layer), _layer_spec(w2.shape[1:], layer), _const_spec((1, d))],
        out_specs=tok(d),
        compiler_params=pltpu.CompilerParams(dimension_semantics=("parallel", "parallel"),
                                             vmem_limit_bytes=VMEM_LIMIT),
        name="out_mlp",
    )(xa, yr, yg, mod, n2, wo, w1, w2, fg)


def kernel(x, c, ctx, c_ctx, ada_w, ada_b, norm1_g, w_in, ret_decay, gla_gate_up, gla_gate_b, gla_norm_g,
           w_out, norm2_g, w_mlp1, w_mlp2, final_g):
    batch, seq, d = x.shape
    cl = ctx.shape[1]
    depth = ada_w.shape[0]
    assert w_in.shape[2] == IN_W
    mod_rows = 16
    cs = jnp.concatenate([c, c_ctx[None, :], jnp.zeros((mod_rows - batch - 1, d), c.dtype)], axis=0)
    mod = _modulation(cs, ada_w, ada_b).reshape(depth * mod_rows, 6, d)

    lat_tables = _rope_tables(seq)
    scale = RET_HEAD_DIM ** -0.5
    ones = jnp.ones((cl, RET_HEAD_DIM), F32)
    ctx_tables = (ones * scale, ones * 0.0, ones, ones * 0.0)

    w_in_b = w_in.astype(BF16)
    zero = jnp.zeros((depth, GLA_GATE_RANK, GLA_K), gla_gate_up.dtype)
    up = jnp.concatenate([jnp.concatenate([gla_gate_up[:, 0], zero], axis=2),
                          jnp.concatenate([zero, gla_gate_up[:, 1]], axis=2)], axis=1).astype(BF16)
    ub = gla_gate_b.reshape(depth, 1, 2 * GLA_K)
    wo = w_out.astype(BF16)
    w1 = w_mlp1.astype(BF16)
    w2 = w_mlp2.astype(BF16)
    n1 = norm1_g.reshape(depth, 1, d)
    n2 = norm2_g.reshape(depth, 1, d)
    gn = gla_norm_g.reshape(depth, 1, GLA_VAL_DIM)
    fg = final_g.reshape(1, d)
    rd = jnp.broadcast_to(jnp.swapaxes(ret_decay, 1, 2)[..., None], (depth, N_RET_HEADS, 2, RET_BLOCK))

    tm = math.gcd(seq, ROW_TILE)
    for layer in range(depth):
        last = layer == depth - 1
        lat_mod = lambda i, layer=layer: layer * mod_rows + i
        ctx_mod = lambda i, layer=layer: layer * mod_rows + batch
        lat = _project(x, mod, lat_mod, layer, n1, w_in_b, up, ub, lat_tables, tm)
        cx = _project(ctx, mod, ctx_mod, layer, n1, w_in_b, up, ub, ctx_tables, cl)
        rq, rk, rv, g, gq, gk, gv, ga = lat
        crq, crk, crv, cg, cgq, cgk, cgv, cga = cx
        yr, cyr = _retention(rd, layer, (rq, rk, rv, g), (crq, crk, crv, cg))
        yg, cyg = _gla(gn, layer, (gq, gk, gv, g, ga), (cgq, cgk, cgv, cg, cga))
        x = _out_mlp(x, yr, yg, mod, lat_mod, layer, n2, wo, w1, w2, fg, tm, last)
        if not last:
            ctx = _out_mlp(ctx, cyr, cyg, mod, ctx_mod, layer, n2, wo, w1, w2, fg, cl, False)
    return x
```

```python
import functools
import math

import jax
import jax.numpy as jnp
from jax import lax
from jax.experimental import pallas as pl
from jax.experimental.pallas import tpu as pltpu

F32 = jnp.float32
BF16 = jnp.bfloat16

GRID_W = 64
N_RET_HEADS = 4
RET_HEAD_DIM = 128
N_GLA_HEADS = 4
GLA_KEY_DIM = 64
GLA_VAL_DIM = 128
GLA_GATE_RANK = 16
GLA_GATE_NORM = 16.0
ROPE_BASE = 10000.0
EPS = 1e-6

RET_W = N_RET_HEADS * RET_HEAD_DIM
GLA_K = N_GLA_HEADS * GLA_KEY_DIM
GLA_V = N_GLA_HEADS * GLA_VAL_DIM
MIX_W = RET_W + GLA_V

OFF_RV = 2 * RET_W
OFF_RG = 3 * RET_W
OFF_GQ = 4 * RET_W
OFF_GK = OFF_GQ + GLA_K
OFF_GV = OFF_GK + GLA_K
OFF_GG = OFF_GV + GLA_V
OFF_D = OFF_GG + GLA_V
IN_W = OFF_D + 2 * GLA_GATE_RANK

RET_BLOCK = 256
GLA_BLOCK = 128
GLA_HALF = GLA_BLOCK // 2

ROW_TILE = 1024
SUB_ROWS = 512
LOG2E = 1.4426950408889634

VMEM_LIMIT = 56 * 1024 * 1024


def _layer_spec(shape, layer):
    zeros = (0,) * len(shape)
    return pl.BlockSpec((None,) + tuple(shape), lambda *_: (layer,) + zeros, pipeline_mode=pl.Buffered(1))


def _const_spec(shape):
    zeros = (0,) * len(shape)
    return pl.BlockSpec(shape, lambda *_: zeros, pipeline_mode=pl.Buffered(1))


def _mod_kernel(cs_ref, w_ref, b_ref, o_ref):
    cs = cs_ref[...]
    s = cs * jax.nn.sigmoid(cs)
    o_ref[0] = jnp.dot(s.astype(BF16), w_ref[0].astype(BF16), preferred_element_type=F32) + b_ref[0]


def _modulation(cs, ada_w, ada_b):
    depth, d, n = ada_w.shape
    rows = cs.shape[0]
    tn = 1024
    return pl.pallas_call(
        _mod_kernel,
        out_shape=jax.ShapeDtypeStruct((depth, rows, n), F32),
        grid=(depth, n // tn),
        in_specs=[
            pl.BlockSpec((rows, d), lambda l, j: (0, 0)),
            pl.BlockSpec((1, d, tn), lambda l, j: (l, 0, j)),
            pl.BlockSpec((1, 1, tn), lambda l, j: (l, 0, j)),
        ],
        out_specs=pl.BlockSpec((1, rows, tn), lambda l, j: (l, 0, j)),
        compiler_params=pltpu.CompilerParams(dimension_semantics=("parallel", "parallel"),
                                             vmem_limit_bytes=VMEM_LIMIT),
        name="adaln_modulation",
    )(cs, ada_w, ada_b.reshape(depth, 1, n))


_QUARTER = RET_HEAD_DIM // 4


def _rope_kernel(cq_ref, sq_ref, ck_ref, sk_ref):
    shape = (GRID_W, RET_HEAD_DIM)
    p = lax.broadcasted_iota(jnp.int32, shape, 0).astype(F32)
    lane = lax.broadcasted_iota(jnp.int32, shape, 1)
    freq = (lane & (_QUARTER - 1)).astype(F32)
    ang = p * jnp.exp(freq * (-math.log(ROPE_BASE) / _QUARTER))
    cos = jnp.cos(ang)
    sin = jnp.where((lane & (2 * _QUARTER - 1)) < _QUARTER, -jnp.sin(ang), jnp.sin(ang))
    by_row = lane < 2 * _QUARTER
    scale = RET_HEAD_DIM ** -0.5
    for r in range(cq_ref.shape[0] // GRID_W):
        c_blk = jnp.where(by_row, cos[r:r + 1, :], cos)
        s_blk = jnp.where(by_row, sin[r:r + 1, :], sin)
        rs = slice(r * GRID_W, (r + 1) * GRID_W)
        cq_ref[rs, :] = c_blk * scale
        sq_ref[rs, :] = s_blk * scale
        ck_ref[rs, :] = c_blk
        sk_ref[rs, :] = s_blk


def _rope_tables(seq):
    sds = jax.ShapeDtypeStruct((seq, RET_HEAD_DIM), F32)
    return pl.pallas_call(_rope_kernel, out_shape=(sds, sds, sds, sds), name="rope_tables")()


def _silu(z):
    hz = 0.5 * z
    return hz * (1.0 + jnp.tanh(hz))


def _proj_kernel(x_ref, mod_ref, n1_ref, w_ref, up_ref, ub_ref, cq_ref, sq_ref, ck_ref, sk_ref,
                 rq_ref, rk_ref, rv_ref, g_ref, gq_ref, gk_ref, gv_ref, ga_ref):
    mod = mod_ref[0]
    gain = n1_ref[...] * (1.0 + mod[1:2])
    sub = math.gcd(x_ref.shape[1], SUB_ROWS)
    lane = lax.broadcasted_iota(jnp.int32, (sub, RET_HEAD_DIM), 1)
    first = (lane & (2 * _QUARTER - 1)) < _QUARTER

    for t in range(x_ref.shape[1] // sub):
        rs = slice(t * sub, (t + 1) * sub)
        x = x_ref[0, rs, :]
        ms = jnp.mean(x * x, axis=-1, keepdims=True)
        hb = (x * lax.rsqrt(ms + EPS) * gain + mod[0:1]).astype(BF16)

        def mm(off, n, hb=hb):
            return jnp.dot(hb, w_ref[:, off:off + n], preferred_element_type=F32)

        def rotary(z, c_ref, s_ref, o_ref, rs=rs):
            c = c_ref[rs, :]
            s = s_ref[rs, :]
            for hd in range(N_RET_HEADS):
                sl = slice(hd * RET_HEAD_DIM, (hd + 1) * RET_HEAD_DIM)
                zh = z[:, sl]
                partner = jnp.where(first, pltpu.roll(zh, RET_HEAD_DIM - _QUARTER, axis=1),
                                    pltpu.roll(zh, _QUARTER, axis=1))
                o_ref[0, rs, sl] = (zh * c + partner * s).astype(BF16)

        d = mm(OFF_D, 2 * GLA_GATE_RANK).astype(BF16)
        zg = jnp.dot(d, up_ref[...], preferred_element_type=F32) + ub_ref[...]
        soft = jnp.log2(1.0 + jnp.exp2(jnp.abs(zg) * -LOG2E))
        ga_ref[0, rs, :] = jnp.minimum(zg, 0.0) * (LOG2E / GLA_GATE_NORM) - soft * (1.0 / GLA_GATE_NORM)
        rv_ref[0, rs, :] = mm(OFF_RV, RET_W).astype(BF16)
        rotary(mm(0, RET_W), cq_ref, sq_ref, rq_ref)
        gk_ref[0, rs, :] = mm(OFF_GK, GLA_K).astype(BF16)
        rotary(mm(RET_W, RET_W), ck_ref, sk_ref, rk_ref)
        gv_ref[0, rs, :] = mm(OFF_GV, GLA_V).astype(BF16)
        g_ref[0, rs, 0:RET_W] = _silu(mm(OFF_RG, RET_W)).astype(BF16)
        gq_ref[0, rs, :] = (mm(OFF_GQ, GLA_K) * (GLA_KEY_DIM ** -0.5)).astype(BF16)
        g_ref[0, rs, RET_W:MIX_W] = _silu(mm(OFF_GG, GLA_V)).astype(BF16)


def _project(xa, mod, mod_index, layer, n1, w, up, ub, tables, tm):
    b, seq, d = xa.shape
    tok = lambda n: pl.BlockSpec((1, tm, n), lambda i, j: (i, j, 0))
    tab = pl.BlockSpec((tm, RET_HEAD_DIM), lambda i, j: (j, 0))
    sd = lambda n, dt: jax.ShapeDtypeStruct((b, seq, n), dt)
    return pl.pallas_call(
        _proj_kernel,
        out_shape=(sd(RET_W, BF16), sd(RET_W, BF16), sd(RET_W, BF16), sd(MIX_W, BF16),
                   sd(GLA_K, BF16), sd(GLA_K, BF16), sd(GLA_V, BF16), sd(2 * GLA_K, F32)),
        grid=(b, seq // tm),
        in_specs=[tok(d), pl.BlockSpec((1, 6, d), lambda i, j: (mod_index(i), 0, 0)),
                  _layer_spec(n1.shape[1:], layer),
                  _layer_spec(w.shape[1:], layer), _layer_spec(up.shape[1:], layer),
                  _layer_spec(ub.shape[1:], layer), tab, tab, tab, tab],
        out_specs=(tok(RET_W), tok(RET_W), tok(RET_W), tok(MIX_W), tok(GLA_K), tok(GLA_K), tok(GLA_V),
                   tok(2 * GLA_K)),
        compiler_params=pltpu.CompilerParams(dimension_semantics=("parallel", "parallel"),
                                             vmem_limit_bytes=VMEM_LIMIT),
        name="norm_project",
    )(xa, mod, n1, w, up, ub, *tables)


def _ret_kernel(rd_ref, q_ref, k_ref, v_ref, g_ref, cq_ref, ck_ref, cv_ref, cg_ref,
                y_ref, cy_ref, kv_ref, st_ref, sc_ref):
    t_blk = RET_BLOCK
    dh = RET_HEAD_DIM
    lg = jnp.log1p(-jnp.exp(rd_ref[0]))
    lgf, lgb = lg[0:1, :], lg[1:2, :]
    lgf_h, lgb_h = lgf[:, :dh], lgb[:, :dh]
    ti = lax.broadcasted_iota(jnp.int32, (t_blk, t_blk), 0)
    si = lax.broadcasted_iota(jnp.int32, (t_blk, t_blk), 1)
    diff = (ti - si).astype(F32)
    mask = jnp.exp(jnp.where(diff >= 0, diff * lgf, -diff * lgb))
    tr = lax.broadcasted_iota(jnp.int32, (t_blk, dh), 0).astype(F32)
    q_dec = jnp.concatenate([jnp.exp((tr + 1.0) * lgf_h), jnp.exp((t_blk - tr) * lgb_h)], axis=1)
    k_dec_f = jnp.exp((t_blk - 1.0 - tr) * lgf_h)
    k_dec_b = jnp.exp(tr * lgb_h)
    blk_f = jnp.exp(t_blk * lgf_h)
    blk_b = jnp.exp(t_blk * lgb_h)
    tn_dims = (((0,), (0,)), ((), ()))
    nt_dims = (((1,), (1,)), ((), ()))
    n_ctx = cq_ref.shape[1] // t_blk
    n_lat = q_ref.shape[1] // t_blk
    n_tot = n_ctx + n_lat
    fwd = slice(0, dh)
    bwd = slice(dh, 2 * dh)

    def rows(i):
        return pl.ds(i * t_blk, t_blk)

    def kv_block(kr, vr, i, j):
        r = rows(i)
        v = vr[0, r, :].astype(F32)
        vv = jnp.concatenate([(v * k_dec_f).astype(BF16), (v * k_dec_b).astype(BF16)], axis=1)
        kv_ref[j] = lax.dot_general(kr[0, r, :], vv, tn_dims, preferred_element_type=F32)

    def score_block(qr, kr, i, j):
        r = rows(i)
        a = lax.dot_general(qr[0, r, :], kr[0, r, :], nt_dims, preferred_element_type=F32) * mask
        sc_ref[j] = a.astype(BF16)

    def out_block(qr, vr, gr, yr, i, j):
        r = rows(i)
        q, v = qr[0, r, :], vr[0, r, :]
        o = jnp.dot(sc_ref[j], v, preferred_element_type=F32)
        inter = jnp.dot(q, st_ref[j], preferred_element_type=F32) * q_dec
        o = o + inter[:, fwd] + inter[:, bwd]
        mu = jnp.mean(o, axis=-1, keepdims=True)
        dlt = o - mu
        var = jnp.mean(dlt * dlt, axis=-1, keepdims=True)
        yr[0, r, :] = (dlt * lax.rsqrt(var + EPS) * gr[0, r, :].astype(F32)).astype(BF16)

    def for_blocks(n, fn):
        for i in range(n):
            fn(i)

    for_blocks(n_ctx, lambda i: kv_block(ck_ref, cv_ref, i, i))
    for_blocks(n_lat, lambda i: kv_block(k_ref, v_ref, i, n_ctx + i))

    def step_f(j, s):
        st_ref[j, :, fwd] = s.astype(BF16)
        return blk_f * s + kv_ref[j, :, fwd]

    def step_b(j, s):
        st_ref[j, :, bwd] = s.astype(BF16)
        return blk_b * s + kv_ref[j, :, bwd]

    zero = jnp.zeros((dh, dh), F32)
    lax.fori_loop(0, n_tot, step_f, zero)
    s_b = lax.fori_loop(0, n_ctx, lambda t, s: step_b(n_ctx - 1 - t, s), zero)
    lax.fori_loop(0, n_lat, lambda t, s: step_b(n_tot - 1 - t, s), s_b)

    for_blocks(n_ctx, lambda i: score_block(cq_ref, ck_ref, i, i))
    for_blocks(n_lat, lambda i: score_block(q_ref, k_ref, i, n_ctx + i))
    for_blocks(n_ctx, lambda i: out_block(cq_ref, cv_ref, cg_ref, cy_ref, i, i))
    for_blocks(n_lat, lambda i: out_block(q_ref, v_ref, g_ref, y_ref, i, n_ctx + i))


def _retention(rd, layer, lat, ctx):
    rq, rk, rv, g = lat
    crq, crk, crv, cg = ctx
    b, seq, _ = rq.shape
    cl = crq.shape[1]
    dh = RET_HEAD_DIM
    n_blocks = (seq + cl) // RET_BLOCK
    head = lambda n: pl.BlockSpec((1, n, dh), lambda i, h: (i, 0, h))
    return pl.pallas_call(
        _ret_kernel,
        out_shape=(jax.ShapeDtypeStruct((b, seq, RET_W), BF16), jax.ShapeDtypeStruct((b, cl, RET_W), BF16)),
        grid=(b, N_RET_HEADS),
        in_specs=[pl.BlockSpec((None, 1, 2, RET_BLOCK), lambda i, h: (layer, h, 0, 0)),
                  head(seq), head(seq), head(seq), head(seq), head(cl), head(cl), head(cl), head(cl)],
        out_specs=(head(seq), head(cl)),
        scratch_shapes=[pltpu.VMEM((n_blocks, dh, 2 * dh), F32), pltpu.VMEM((n_blocks, dh, 2 * dh), BF16),
                        pltpu.VMEM((n_blocks, RET_BLOCK, RET_BLOCK), BF16)],
        compiler_params=pltpu.CompilerParams(dimension_semantics=("parallel", "parallel"),
                                             vmem_limit_bytes=VMEM_LIMIT),
        name="retention_scan",
    )(rd, rq, rk, rv, g, crq, crk, crv, cg)


def _gla_kernel(gn_ref, q_ref, k_ref, v_ref, g_ref, af_ref, ab_ref, cq_ref, ck_ref, cv_ref, cg_ref,
                caf_ref, cab_ref, y_ref, cy_ref, c_ref, kv_ref, dec_ref, st_ref, sc_ref, qin_ref):
    t_blk = GLA_BLOCK
    dv = GLA_VAL_DIM
    dk2 = 2 * GLA_KEY_DIM
    assert t_blk == dk2 == dv
    ti = lax.broadcasted_iota(jnp.int32, (t_blk, t_blk), 0)
    si = lax.broadcasted_iota(jnp.int32, (t_blk, t_blk), 1)
    lower = si <= ti
    cum_f = lower.astype(BF16)
    cum_b = (si >= ti).astype(BF16)
    head0 = si < GLA_KEY_DIM
    tn_dims = (((0,), (0,)), ((), ()))
    nt_dims = (((1,), (1,)), ((), ()))

    def rows(i):
        return pl.ds(i * t_blk, t_blk)

    def cumulate(cum_mat, a):
        a1 = a.astype(BF16)
        r1 = a - a1.astype(F32)
        a2 = r1.astype(BF16)
        a3 = (r1 - a2.astype(F32)).astype(BF16)
        z = jnp.dot(cum_mat, jnp.concatenate([a1, a2, a3], axis=1), preferred_element_type=F32)
        return z[:, 0:dk2] + z[:, dk2:2 * dk2] + z[:, 2 * dk2:3 * dk2]

    dirs = ((cum_f, GLA_HALF - 1, t_blk - 1), (cum_b, GLA_HALF, 0))

    def cum_block(ars, i, row0):
        for d, (cum_mat, _, _) in enumerate(dirs):
            c_ref[d, rows(row0 + i), :] = cumulate(cum_mat, ars[d][0, rows(i), :])

    def mid_block(qr, kr, vr, i, j, row0):
        r = rows(i)
        cr = rows(row0 + i)
        q = qr[0, r, :].astype(F32)
        k = kr[0, r, :].astype(F32)
        v = vr[0, r, :]
        sc, k_end = [], []
        for d, (_, mid, edge) in enumerate(dirs):
            c = c_ref[d, cr, :]
            c_mid = c[mid:mid + 1, :]
            c_edge = c[edge:edge + 1, :]
            q_mid = q * jnp.exp2(c - c_mid)
            k_mid = k * jnp.exp2(c_mid - c)
            k_end.append((k_mid * jnp.exp2(c_edge - c_mid)).astype(BF16))
            qin_ref[d, cr, :] = (q_mid * jnp.exp2(c_mid)).astype(BF16)
            dec_ref[d, j] = jnp.broadcast_to(jnp.exp2(c_edge), (8, dk2))
            k_mid_b = k_mid.astype(BF16)
            zero = jnp.zeros_like(k_mid_b)
            k_heads = jnp.concatenate([jnp.where(head0, k_mid_b, zero), jnp.where(head0, zero, k_mid_b)], axis=0)
            sc.append(lax.dot_general(q_mid.astype(BF16), k_heads, nt_dims, preferred_element_type=F32))
        kv = lax.dot_general(v, jnp.concatenate(k_end, axis=1), tn_dims, preferred_element_type=F32)
        for d in range(2):
            kvd = kv[:, d * dk2:(d + 1) * dk2]
            kv_ref[d, j] = jnp.where(head0, kvd[0:dv, :], kvd[dv:2 * dv, :])
        for hd in range(2):
            hs = slice(hd * t_blk, (hd + 1) * t_blk)
            sc_ref[hd, j] = jnp.where(lower, sc[0][:, hs], sc[1][:, hs]).astype(BF16)

    def out_block(qr, vr, gr, yr, i, j, row0):
        r = rows(i)
        cr = rows(row0 + i)
        v = vr[0, r, :]
        s_exp = []
        for d in range(2):
            s = st_ref[d, j]
            zero = jnp.zeros_like(s)
            s_exp.append(jnp.concatenate([jnp.where(head0, s, zero), jnp.where(head0, zero, s)], axis=0))
        q_in = jnp.concatenate([qin_ref[0, cr, :], qin_ref[1, cr, :]], axis=1)
        o = jnp.concatenate([jnp.dot(sc_ref[hd, j], v[:, hd * dv:(hd + 1) * dv], preferred_element_type=F32)
                             for hd in range(2)], axis=1)
        o = o + lax.dot_general(q_in, jnp.concatenate(s_exp, axis=1), nt_dims, preferred_element_type=F32)
        gate = gr[0, r, :].astype(F32)
        gn = gn_ref[...]
        for hd in range(2):
            sl = slice(hd * dv, (hd + 1) * dv)
            oh = o[:, sl]
            ms = jnp.mean(oh * oh, axis=-1, keepdims=True)
            yr[0, r, sl] = (oh * lax.rsqrt(ms + EPS) * gn * gate[:, sl]).astype(BF16)

    def for_blocks(n, fn):
        for i in range(n):
            fn(i)

    n_ctx = cq_ref.shape[1] // t_blk
    n_lat = q_ref.shape[1] // t_blk
    n_tot = n_ctx + n_lat
    for_blocks(n_ctx, lambda i: cum_block((caf_ref, cab_ref), i, 0))
    for_blocks(n_lat, lambda i: cum_block((af_ref, ab_ref), i, n_ctx))
    for_blocks(n_ctx, lambda i: mid_block(cq_ref, ck_ref, cv_ref, i, i, 0))
    for_blocks(n_lat, lambda i: mid_block(q_ref, k_ref, v_ref, i, n_ctx + i, n_ctx))

    def step(d, j, s):
        st_ref[d, j] = s.astype(BF16)
        return dec_ref[d, j, 0:1, :] * s + kv_ref[d, j]

    zero = jnp.zeros((dv, dk2), F32)
    lax.fori_loop(0, n_tot, lambda j, s: step(0, j, s), zero)
    s_b = lax.fori_loop(0, n_ctx, lambda t, s: step(1, n_ctx - 1 - t, s), zero)
    lax.fori_loop(0, n_lat, lambda t, s: step(1, n_tot - 1 - t, s), s_b)

    for_blocks(n_ctx, lambda i: out_block(cq_ref, cv_ref, cg_ref, cy_ref, i, i, 0))
    for_blocks(n_lat, lambda i: out_block(q_ref, v_ref, g_ref, y_ref, i, n_ctx + i, n_ctx))


def _gla(gn, layer, lat, ctx):
    gq, gk, gv, g, ga = lat
    cgq, cgk, cgv, cg, cga = ctx
    b, seq, _ = gq.shape
    cl = cgq.shape[1]
    pair_k = 2 * GLA_KEY_DIM
    pair_v = 2 * GLA_VAL_DIM
    n_pairs = N_GLA_HEADS // 2
    n_blocks = (seq + cl) // GLA_BLOCK
    gate_off = RET_W // pair_v
    key = lambda n: pl.BlockSpec((1, n, pair_k), lambda i, p: (i, 0, p))
    key_b = lambda n: pl.BlockSpec((1, n, pair_k), lambda i, p: (i, 0, n_pairs + p))
    val = lambda n: pl.BlockSpec((1, n, pair_v), lambda i, p: (i, 0, p))
    mix = lambda n: pl.BlockSpec((1, n, pair_v), lambda i, p: (i, 0, gate_off + p))
    return pl.pallas_call(
        _gla_kernel,
        out_shape=(jax.ShapeDtypeStruct((b, seq, GLA_V), BF16), jax.ShapeDtypeStruct((b, cl, GLA_V), BF16)),
        grid=(b, n_pairs),
        in_specs=[_layer_spec(gn.shape[1:], layer),
                  key(seq), key(seq), val(seq), mix(seq), key(seq), key_b(seq),
                  key(cl), key(cl), val(cl), mix(cl), key(cl), key_b(cl)],
        out_specs=(val(seq), val(cl)),
        scratch_shapes=[pltpu.VMEM((2, seq + cl, pair_k), F32),
                        pltpu.VMEM((2, n_blocks, GLA_VAL_DIM, pair_k), F32),
                        pltpu.VMEM((2, n_blocks, 8, pair_k), F32),
                        pltpu.VMEM((2, n_blocks, GLA_VAL_DIM, pair_k), BF16),
                        pltpu.VMEM((2, n_blocks, GLA_BLOCK, GLA_BLOCK), BF16),
                        pltpu.VMEM((2, seq + cl, pair_k), BF16)],
        compiler_params=pltpu.CompilerParams(dimension_semantics=("parallel", "parallel"),
                                             vmem_limit_bytes=VMEM_LIMIT),
        name="gla_scan",
    )(gn, gq, gk, gv, g, ga, ga, cgq, cgk, cgv, cg, cga, cga)


def _mlp_kernel(x_ref, yr_ref, yg_ref, mod_ref, n2_ref, wo_ref, w1_ref, w2_ref, fg_ref, o_ref, *,
                final, ff_chunk):
    mod = mod_ref[0]
    gain = n2_ref[...] * (1.0 + mod[4:5])
    sub = math.gcd(x_ref.shape[1], SUB_ROWS)
    for t in range(x_ref.shape[1] // sub):
        rs = slice(t * sub, (t + 1) * sub)
        mix = (jnp.dot(yr_ref[0, rs, :], wo_ref[0:RET_W, :], preferred_element_type=F32)
               + jnp.dot(yg_ref[0, rs, :], wo_ref[RET_W:MIX_W, :], preferred_element_type=F32))
        x1 = x_ref[0, rs, :] + mod[2:3] * mix
        ms = jnp.mean(x1 * x1, axis=-1, keepdims=True)
        hb = (x1 * lax.rsqrt(ms + EPS) * gain + mod[3:4]).astype(BF16)
        acc = jnp.zeros(x1.shape, F32)
        for c in range(w1_ref.shape[1] // ff_chunk):
            sl = slice(c * ff_chunk, (c + 1) * ff_chunk)
            a = jnp.maximum(jnp.dot(hb, w1_ref[:, sl], preferred_element_type=F32), 0.0)
            acc = acc + jnp.dot((a * a).astype(BF16), w2_ref[sl, :], preferred_element_type=F32)
        x2 = x1 + mod[5:6] * acc
        if final:
            ms = jnp.mean(x2 * x2, axis=-1, keepdims=True)
            x2 = x2 * lax.rsqrt(ms + EPS) * fg_ref[...]
        o_ref[0, rs, :] = x2


def _out_mlp(xa, yr, yg, mod, mod_index, layer, n2, wo, w1, w2, fg, tm, final):
    b, seq, d = xa.shape
    tok = lambda n: pl.BlockSpec((1, tm, n), lambda i, j: (i, j, 0))
    return pl.pallas_call(
        functools.partial(_mlp_kernel, final=final, ff_chunk=1024),
        out_shape=jax.ShapeDtypeStruct(xa.shape, F32),
        grid=(b, seq // tm),
        in_specs=[tok(d), tok(RET_W), tok(GLA_V), pl.BlockSpec((1, 6, d), lambda i, j: (mod_index(i), 0, 0)),
                  _layer_spec(n2.shape[1:], layer), _layer_spec(wo.shape[1:], layer),
                  _layer_spec(w1.shape[1:], layer), _layer_spec(w2.shape[1:], layer), _const_spec((1, d))],
        out_specs=tok(d),
        compiler_params=pltpu.CompilerParams(dimension_semantics=("parallel", "parallel"),
                                             vmem_limit_bytes=VMEM_LIMIT),
        name="out_mlp",
    )(xa, yr, yg, mod, n2, wo, w1, w2, fg)


def kernel(x, c, ctx, c_ctx, ada_w, ada_b, norm1_g, w_in, ret_decay, gla_gate_up, gla_gate_b, gla_norm_g,
           w_out, norm2_g, w_mlp1, w_mlp2, final_g):
    batch, seq, d = x.shape
    cl = ctx.shape[1]
    depth = ada_w.shape[0]
    assert w_in.shape[2] == IN_W
    mod_rows = 16
    cs = jnp.concatenate([c, c_ctx[None, :], jnp.zeros((mod_rows - batch - 1, d), c.dtype)], axis=0)
    mod = _modulation(cs, ada_w, ada_b).reshape(depth * mod_rows, 6, d)

    lat_tables = _rope_tables(seq)
    scale = RET_HEAD_DIM ** -0.5
    ones = jnp.ones((cl, RET_HEAD_DIM), F32)
    ctx_tables = (ones * scale, ones * 0.0, ones, ones * 0.0)

    w_in_b = w_in.astype(BF16)
    zero = jnp.zeros((depth, GLA_GATE_RANK, GLA_K), gla_gate_up.dtype)
    up = jnp.concatenate([jnp.concatenate([gla_gate_up[:, 0], zero], axis=2),
                          jnp.concatenate([zero, gla_gate_up[:, 1]], axis=2)], axis=1).astype(BF16)
    ub = gla_gate_b.reshape(depth, 1, 2 * GLA_K)
    wo = w_out.astype(BF16)
    w1 = w_mlp1.astype(BF16)
    w2 = w_mlp2.astype(BF16)
    n1 = norm1_g.reshape(depth, 1, d)
    n2 = norm2_g.reshape(depth, 1, d)
    gn = gla_norm_g.reshape(depth, 1, GLA_VAL_DIM)
    fg = final_g.reshape(1, d)
    rd = jnp.broadcast_to(jnp.swapaxes(ret_decay, 1, 2)[..., None], (depth, N_RET_HEADS, 2, RET_BLOCK))

    tm = math.gcd(seq, ROW_TILE)
    for layer in range(depth):
        last = layer == depth - 1
        lat_mod = lambda i, layer=layer: layer * mod_rows + i
        ctx_mod = lambda i, layer=layer: layer * mod_rows + batch
        lat = _project(x, mod, lat_mod, layer, n1, w_in_b, up, ub, lat_tables, tm)
        cx = _project(ctx, mod, ctx_mod, layer, n1, w_in_b, up, ub, ctx_tables, cl)
        rq, rk, rv, g, gq, gk, gv, ga = lat
        crq, crk, crv, cg, cgq, cgk, cgv, cga = cx
        yr, cyr = _retention(rd, layer, (rq, rk, rv, g), (crq, crk, crv, cg))
        yg, cyg = _gla(gn, layer, (gq, gk, gv, g, ga), (cgq, cgk, cgv, cg, cga))
        x = _out_mlp(x, yr, yg, mod, lat_mod, layer, n2, wo, w1, w2, fg, tm, last)
        if not last:
            ctx = _out_mlp(ctx, cyr, cyg, mod, ctx_mod, layer, n2, wo, w1, w2, fg, cl, False)
    return x
```

```python
import functools
import math

import jax
import jax.numpy as jnp
from jax import lax
from jax.experimental import pallas as pl
from jax.experimental.pallas import tpu as pltpu

F32 = jnp.float32
BF16 = jnp.bfloat16

GRID_W = 64
N_RET_HEADS = 4
RET_HEAD_DIM = 128
N_GLA_HEADS = 4
GLA_KEY_DIM = 64
GLA_VAL_DIM = 128
GLA_GATE_RANK = 16
GLA_GATE_NORM = 16.0
ROPE_BASE = 10000.0
EPS = 1e-6

RET_W = N_RET_HEADS * RET_HEAD_DIM
GLA_K = N_GLA_HEADS * GLA_KEY_DIM
GLA_V = N_GLA_HEADS * GLA_VAL_DIM
MIX_W = RET_W + GLA_V

OFF_RV = 2 * RET_W
OFF_RG = 3 * RET_W
OFF_GQ = 4 * RET_W
OFF_GK = OFF_GQ + GLA_K
OFF_GV = OFF_GK + GLA_K
OFF_GG = OFF_GV + GLA_V
OFF_D = OFF_GG + GLA_V
IN_W = OFF_D + 2 * GLA_GATE_RANK

RET_BLOCK = 256
GLA_BLOCK = 128
GLA_HALF = GLA_BLOCK // 2

ROW_TILE = 1024
SUB_ROWS = 512
LOG2E = 1.4426950408889634

VMEM_LIMIT = 56 * 1024 * 1024


def _layer_spec(shape, layer):
    zeros = (0,) * len(shape)
    return pl.BlockSpec((None,) + tuple(shape), lambda *_: (layer,) + zeros, pipeline_mode=pl.Buffered(1))


def _const_spec(shape):
    zeros = (0,) * len(shape)
    return pl.BlockSpec(shape, lambda *_: zeros, pipeline_mode=pl.Buffered(1))


def _mod_kernel(cs_ref, w_ref, b_ref, o_ref):
    cs = cs_ref[...]
    s = cs * jax.nn.sigmoid(cs)
    o_ref[0] = jnp.dot(s.astype(BF16), w_ref[0].astype(BF16), preferred_element_type=F32) + b_ref[0]


def _modulation(cs, ada_w, ada_b):
    depth, d, n = ada_w.shape
    rows = cs.shape[0]
    tn = 1024
    return pl.pallas_call(
        _mod_kernel,
        out_shape=jax.ShapeDtypeStruct((depth, rows, n), F32),
        grid=(depth, n // tn),
        in_specs=[
            pl.BlockSpec((rows, d), lambda l, j: (0, 0)),
            pl.BlockSpec((1, d, tn), lambda l, j: (l, 0, j)),
            pl.BlockSpec((1, 1, tn), lambda l, j: (l, 0, j)),
        ],
        out_specs=pl.BlockSpec((1, rows, tn), lambda l, j: (l, 0, j)),
        compiler_params=pltpu.CompilerParams(dimension_semantics=("parallel", "parallel"),
                                             vmem_limit_bytes=VMEM_LIMIT),
        name="adaln_modulation",
    )(cs, ada_w, ada_b.reshape(depth, 1, n))


_QUARTER = RET_HEAD_DIM // 4


def _rope_kernel(cq_ref, sq_ref, ck_ref, sk_ref):
    shape = (GRID_W, RET_HEAD_DIM)
    p = lax.broadcasted_iota(jnp.int32, shape, 0).astype(F32)
    lane = lax.broadcasted_iota(jnp.int32, shape, 1)
    freq = (lane & (_QUARTER - 1)).astype(F32)
    ang = p * jnp.exp(freq * (-math.log(ROPE_BASE) / _QUARTER))
    cos = jnp.cos(ang)
    sin = jnp.where((lane & (2 * _QUARTER - 1)) < _QUARTER, -jnp.sin(ang), jnp.sin(ang))
    by_row = lane < 2 * _QUARTER
    scale = RET_HEAD_DIM ** -0.5
    for r in range(cq_ref.shape[0] // GRID_W):
        c_blk = jnp.where(by_row, cos[r:r + 1, :], cos)
        s_blk = jnp.where(by_row, sin[r:r + 1, :], sin)
        rs = slice(r * GRID_W, (r + 1) * GRID_W)
        cq_ref[rs, :] = c_blk * scale
        sq_ref[rs, :] = s_blk * scale
        ck_ref[rs, :] = c_blk
        sk_ref[rs, :] = s_blk


def _rope_tables(seq):
    sds = jax.ShapeDtypeStruct((seq, RET_HEAD_DIM), F32)
    return pl.pallas_call(_rope_kernel, out_shape=(sds, sds, sds, sds), name="rope_tables")()


def _silu(z):
    hz = 0.5 * z
    return hz * (1.0 + jnp.tanh(hz))


def _proj_kernel(x_ref, mod_ref, n1_ref, w_ref, wdt_ref, up_ref, ub_ref, cq_ref, sq_ref, ck_ref, sk_ref,
                 rq_ref, rk_ref, rv_ref, g_ref, gq_ref, gk_ref, gv_ref, ga_ref):
    mod = mod_ref[0]
    gain = n1_ref[...] * (1.0 + mod[1:2])
    sub = math.gcd(x_ref.shape[1], SUB_ROWS)
    lane = lax.broadcasted_iota(jnp.int32, (sub, RET_HEAD_DIM), 1)
    first = (lane & (2 * _QUARTER - 1)) < _QUARTER

    for t in range(x_ref.shape[1] // sub):
        rs = slice(t * sub, (t + 1) * sub)
        x = x_ref[0, rs, :]
        ms = jnp.mean(x * x, axis=-1, keepdims=True)
        hb = (x * lax.rsqrt(ms + EPS) * gain + mod[0:1]).astype(BF16)

        def mm(off, n, hb=hb):
            return jnp.dot(hb, w_ref[:, off:off + n], preferred_element_type=F32)

        def rotary(z, c_ref, s_ref, o_ref, rs=rs):
            c = c_ref[rs, :]
            s = s_ref[rs, :]
            for hd in range(N_RET_HEADS):
                sl = slice(hd * RET_HEAD_DIM, (hd + 1) * RET_HEAD_DIM)
                zh = z[:, sl]
                partner = jnp.where(first, pltpu.roll(zh, RET_HEAD_DIM - _QUARTER, axis=1),
                                    pltpu.roll(zh, _QUARTER, axis=1))
                o_ref[0, rs, sl] = (zh * c + partner * s).astype(BF16)

        dt = lax.dot_general(wdt_ref[...], hb, (((1,), (1,)), ((), ())), preferred_element_type=F32)
        zg = lax.dot_general(dt.astype(BF16), up_ref[...], (((0,), (0,)), ((), ())),
                             preferred_element_type=F32) + ub_ref[...]
        soft = jnp.log2(1.0 + jnp.exp2(jnp.abs(zg) * -LOG2E))
        ga_ref[0, rs, :] = jnp.minimum(zg, 0.0) * (LOG2E / GLA_GATE_NORM) - soft * (1.0 / GLA_GATE_NORM)
        rv_ref[0, rs, :] = mm(OFF_RV, RET_W).astype(BF16)
        rotary(mm(0, RET_W), cq_ref, sq_ref, rq_ref)
        gk_ref[0, rs, :] = mm(OFF_GK, GLA_K).astype(BF16)
        rotary(mm(RET_W, RET_W), ck_ref, sk_ref, rk_ref)
        gv_ref[0, rs, :] = mm(OFF_GV, GLA_V).astype(BF16)
        g_ref[0, rs, 0:RET_W] = _silu(mm(OFF_RG, RET_W)).astype(BF16)
        gq_ref[0, rs, :] = (mm(OFF_GQ, GLA_K) * (GLA_KEY_DIM ** -0.5)).astype(BF16)
        g_ref[0, rs, RET_W:MIX_W] = _silu(mm(OFF_GG, GLA_V)).astype(BF16)


def _project(xa, mod, mod_index, layer, n1, w, wdt, up, ub, tables, tm):
    b, seq, d = xa.shape
    tok = lambda n: pl.BlockSpec((1, tm, n), lambda i, j: (i, j, 0))
    tab = pl.BlockSpec((tm, RET_HEAD_DIM), lambda i, j: (j, 0))
    sd = lambda n, dt: jax.ShapeDtypeStruct((b, seq, n), dt)
    return pl.pallas_call(
        _proj_kernel,
        out_shape=(sd(RET_W, BF16), sd(RET_W, BF16), sd(RET_W, BF16), sd(MIX_W, BF16),
                   sd(GLA_K, BF16), sd(GLA_K, BF16), sd(GLA_V, BF16), sd(2 * GLA_K, F32)),
        grid=(b, seq // tm),
        in_specs=[tok(d), pl.BlockSpec((1, 6, d), lambda i, j: (mod_index(i), 0, 0)),
                  _layer_spec(n1.shape[1:], layer),
                  _layer_spec(w.shape[1:], layer), _layer_spec(wdt.shape[1:], layer),
                  _layer_spec(up.shape[1:], layer), _layer_spec(ub.shape[1:], layer), tab, tab, tab, tab],
        out_specs=(tok(RET_W), tok(RET_W), tok(RET_W), tok(MIX_W), tok(GLA_K), tok(GLA_K), tok(GLA_V),
                   tok(2 * GLA_K)),
        compiler_params=pltpu.CompilerParams(dimension_semantics=("parallel", "parallel"),
                                             vmem_limit_bytes=VMEM_LIMIT),
        name="norm_project",
    )(xa, mod, n1, w, wdt, up, ub, *tables)


def _ret_kernel(rd_ref, q_ref, k_ref, v_ref, g_ref, cq_ref, ck_ref, cv_ref, cg_ref,
                y_ref, cy_ref, kv_ref, st_ref, sc_ref):
    t_blk = RET_BLOCK
    dh = RET_HEAD_DIM
    lg = jnp.log1p(-jnp.exp(rd_ref[0]))
    lgf, lgb = lg[0:1, :], lg[1:2, :]
    lgf_h, lgb_h = lgf[:, :dh], lgb[:, :dh]
    ti = lax.broadcasted_iota(jnp.int32, (t_blk, t_blk), 0)
    si = lax.broadcasted_iota(jnp.int32, (t_blk, t_blk), 1)
    diff = (ti - si).astype(F32)
    mask = jnp.exp(jnp.where(diff >= 0, diff * lgf, -diff * lgb))
    tr = lax.broadcasted_iota(jnp.int32, (t_blk, dh), 0).astype(F32)
    q_dec = jnp.concatenate([jnp.exp((tr + 1.0) * lgf_h), jnp.exp((t_blk - tr) * lgb_h)], axis=1)
    k_dec_f = jnp.exp((t_blk - 1.0 - tr) * lgf_h)
    k_dec_b = jnp.exp(tr * lgb_h)
    blk_f = jnp.exp(t_blk * lgf_h)
    blk_b = jnp.exp(t_blk * lgb_h)
    tn_dims = (((0,), (0,)), ((), ()))
    nt_dims = (((1,), (1,)), ((), ()))
    n_ctx = cq_ref.shape[1] // t_blk
    n_lat = q_ref.shape[1] // t_blk
    n_tot = n_ctx + n_lat
    fwd = slice(0, dh)
    bwd = slice(dh, 2 * dh)

    def rows(i):
        return pl.ds(i * t_blk, t_blk)

    def kv_block(kr, vr, i, j):
        r = rows(i)
        v = vr[0, r, :].astype(F32)
        vv = jnp.concatenate([(v * k_dec_f).astype(BF16), (v * k_dec_b).astype(BF16)], axis=1)
        kv_ref[j] = lax.dot_general(kr[0, r, :], vv, tn_dims, preferred_element_type=F32)

    def score_block(qr, kr, i, j):
        r = rows(i)
        a = lax.dot_general(qr[0, r, :], kr[0, r, :], nt_dims, preferred_element_type=F32) * mask
        sc_ref[j] = a.astype(BF16)

    def out_block(qr, vr, gr, yr, i, j):
        r = rows(i)
        q, v = qr[0, r, :], vr[0, r, :]
        o = jnp.dot(sc_ref[j], v, preferred_element_type=F32)
        inter = jnp.dot(q, st_ref[j], preferred_element_type=F32) * q_dec
        o = o + inter[:, fwd] + inter[:, bwd]
        mu = jnp.mean(o, axis=-1, keepdims=True)
        dlt = o - mu
        var = jnp.mean(dlt * dlt, axis=-1, keepdims=True)
        yr[0, r, :] = (dlt * lax.rsqrt(var + EPS) * gr[0, r, :].astype(F32)).astype(BF16)

    def for_blocks(n, fn):
        for i in range(n):
            fn(i)

    for_blocks(n_ctx, lambda i: kv_block(ck_ref, cv_ref, i, i))
    for_blocks(n_lat, lambda i: kv_block(k_ref, v_ref, i, n_ctx + i))

    def step_f(j, s):
        st_ref[j, :, fwd] = s.astype(BF16)
        return blk_f * s + kv_ref[j, :, fwd]

    def step_b(j, s):
        st_ref[j, :, bwd] = s.astype(BF16)
        return blk_b * s + kv_ref[j, :, bwd]

    zero = jnp.zeros((dh, dh), F32)
    lax.fori_loop(0, n_tot, step_f, zero)
    s_b = lax.fori_loop(0, n_ctx, lambda t, s: step_b(n_ctx - 1 - t, s), zero)
    lax.fori_loop(0, n_lat, lambda t, s: step_b(n_tot - 1 - t, s), s_b)

    for_blocks(n_ctx, lambda i: score_block(cq_ref, ck_ref, i, i))
    for_blocks(n_lat, lambda i: score_block(q_ref, k_ref, i, n_ctx + i))
    for_blocks(n_ctx, lambda i: out_block(cq_ref, cv_ref, cg_ref, cy_ref, i, i))
    for_blocks(n_lat, lambda i: out_block(q_ref, v_ref, g_ref, y_ref, i, n_ctx + i))


def _retention(rd, layer, lat, ctx):
    rq, rk, rv, g = lat
    crq, crk, crv, cg = ctx
    b, seq, _ = rq.shape
    cl = crq.shape[1]
    dh = RET_HEAD_DIM
    n_blocks = (seq + cl) // RET_BLOCK
    head = lambda n: pl.BlockSpec((1, n, dh), lambda i, h: (i, 0, h))
    return pl.pallas_call(
        _ret_kernel,
        out_shape=(jax.ShapeDtypeStruct((b, seq, RET_W), BF16), jax.ShapeDtypeStruct((b, cl, RET_W), BF16)),
        grid=(b, N_RET_HEADS),
        in_specs=[pl.BlockSpec((None, 1, 2, RET_BLOCK), lambda i, h: (layer, h, 0, 0)),
                  head(seq), head(seq), head(seq), head(seq), head(cl), head(cl), head(cl), head(cl)],
        out_specs=(head(seq), head(cl)),
        scratch_shapes=[pltpu.VMEM((n_blocks, dh, 2 * dh), F32), pltpu.VMEM((n_blocks, dh, 2 * dh), BF16),
                        pltpu.VMEM((n_blocks, RET_BLOCK, RET_BLOCK), BF16)],
        compiler_params=pltpu.CompilerParams(dimension_semantics=("parallel", "parallel"),
                                             vmem_limit_bytes=VMEM_LIMIT),
        name="retention_scan",
    )(rd, rq, rk, rv, g, crq, crk, crv, cg)


def _gla_kernel(gn_ref, q_ref, k_ref, v_ref, g_ref, af_ref, ab_ref, cq_ref, ck_ref, cv_ref, cg_ref,
                caf_ref, cab_ref, y_ref, cy_ref, c_ref, kv_ref, dec_ref, st_ref, sc_ref, qin_ref):
    t_blk = GLA_BLOCK
    dv = GLA_VAL_DIM
    dk2 = 2 * GLA_KEY_DIM
    assert t_blk == dk2 == dv
    ti = lax.broadcasted_iota(jnp.int32, (t_blk, t_blk), 0)
    si = lax.broadcasted_iota(jnp.int32, (t_blk, t_blk), 1)
    lower = si <= ti
    cum_f = lower.astype(BF16)
    cum_b = (si >= ti).astype(BF16)
    head0 = si < GLA_KEY_DIM
    tn_dims = (((0,), (0,)), ((), ()))
    nt_dims = (((1,), (1,)), ((), ()))

    def rows(i):
        return pl.ds(i * t_blk, t_blk)

    def cumulate(cum_mat, a):
        a1 = a.astype(BF16)
        r1 = a - a1.astype(F32)
        a2 = r1.astype(BF16)
        a3 = (r1 - a2.astype(F32)).astype(BF16)
        z = jnp.dot(cum_mat, jnp.concatenate([a1, a2, a3], axis=1), preferred_element_type=F32)
        return z[:, 0:dk2] + z[:, dk2:2 * dk2] + z[:, 2 * dk2:3 * dk2]

    dirs = ((cum_f, GLA_HALF - 1, t_blk - 1), (cum_b, GLA_HALF, 0))

    def cum_block(ars, i, row0):
        for d, (cum_mat, _, _) in enumerate(dirs):
            c_ref[d, rows(row0 + i), :] = cumulate(cum_mat, ars[d][0, rows(i), :])

    def mid_block(qr, kr, vr, i, j, row0):
        r = rows(i)
        cr = rows(row0 + i)
        q = qr[0, r, :].astype(F32)
        k = kr[0, r, :].astype(F32)
        v = vr[0, r, :]
        sc, k_end = [], []
        for d, (_, mid, edge) in enumerate(dirs):
            c = c_ref[d, cr, :]
            c_mid = c[mid:mid + 1, :]
            c_edge = c[edge:edge + 1, :]
            q_mid = q * jnp.exp2(c - c_mid)
            k_mid = k * jnp.exp2(c_mid - c)
            k_end.append((k_mid * jnp.exp2(c_edge - c_mid)).astype(BF16))
            qin_ref[d, cr, :] = (q_mid * jnp.exp2(c_mid)).astype(BF16)
            dec_ref[d, j] = jnp.broadcast_to(jnp.exp2(c_edge), (8, dk2))
            k_mid_b = k_mid.astype(BF16)
            zero = jnp.zeros_like(k_mid_b)
            k_heads = jnp.concatenate([jnp.where(head0, k_mid_b, zero), jnp.where(head0, zero, k_mid_b)], axis=0)
            sc.append(lax.dot_general(q_mid.astype(BF16), k_heads, nt_dims, preferred_element_type=F32))
        kv = lax.dot_general(v, jnp.concatenate(k_end, axis=1), tn_dims, preferred_element_type=F32)
        for d in range(2):
            kvd = kv[:, d * dk2:(d + 1) * dk2]
            kv_ref[d, j] = jnp.where(head0, kvd[0:dv, :], kvd[dv:2 * dv, :])
        for hd in range(2):
            hs = slice(hd * t_blk, (hd + 1) * t_blk)
            sc_ref[hd, j] = jnp.where(lower, sc[0][:, hs], sc[1][:, hs]).astype(BF16)

    def out_block(qr, vr, gr, yr, i, j, row0):
        r = rows(i)
        cr = rows(row0 + i)
        v = vr[0, r, :]
        s_exp = []
        for d in range(2):
            s = st_ref[d, j]
            zero = jnp.zeros_like(s)
            s_exp.append(jnp.concatenate([jnp.where(head0, s, zero), jnp.where(head0, zero, s)], axis=0))
        q_in = jnp.concatenate([qin_ref[0, cr, :], qin_ref[1, cr, :]], axis=1)
        o = jnp.concatenate([jnp.dot(sc_ref[hd, j], v[:, hd * dv:(hd + 1) * dv], preferred_element_type=F32)
                             for hd in range(2)], axis=1)
        o = o + lax.dot_general(q_in, jnp.concatenate(s_exp, axis=1), nt_dims, preferred_element_type=F32)
        gate = gr[0, r, :].astype(F32)
        gn = gn_ref[...]
        for hd in range(2):
            sl = slice(hd * dv, (hd + 1) * dv)
            oh = o[:, sl]
            ms = jnp.mean(oh * oh, axis=-1, keepdims=True)
            yr[0, r, sl] = (oh * lax.rsqrt(ms + EPS) * gn * gate[:, sl]).astype(BF16)

    def for_blocks(n, fn):
        for i in range(n):
            fn(i)

    n_ctx = cq_ref.shape[1] // t_blk
    n_lat = q_ref.shape[1] // t_blk
    n_tot = n_ctx + n_lat
    for_blocks(n_ctx, lambda i: cum_block((caf_ref, cab_ref), i, 0))
    for_blocks(n_lat, lambda i: cum_block((af_ref, ab_ref), i, n_ctx))
    for_blocks(n_ctx, lambda i: mid_block(cq_ref, ck_ref, cv_ref, i, i, 0))
    for_blocks(n_lat, lambda i: mid_block(q_ref, k_ref, v_ref, i, n_ctx + i, n_ctx))

    def step(d, j, s):
        st_ref[d, j] = s.astype(BF16)
        return dec_ref[d, j, 0:1, :] * s + kv_ref[d, j]

    zero = jnp.zeros((dv, dk2), F32)
    lax.fori_loop(0, n_tot, lambda j, s: step(0, j, s), zero)
    s_b = lax.fori_loop(0, n_ctx, lambda t, s: step(1, n_ctx - 1 - t, s), zero)
    lax.fori_loop(0, n_lat, lambda t, s: step(1, n_tot - 1 - t, s), s_b)

    for_blocks(n_ctx, lambda i: out_block(cq_ref, cv_ref, cg_ref, cy_ref, i, i, 0))
    for_blocks(n_lat, lambda i: out_block(q_ref, v_ref, g_ref, y_ref, i, n_ctx + i, n_ctx))


def _gla(gn, layer, lat, ctx):
    gq, gk, gv, g, ga = lat
    cgq, cgk, cgv, cg, cga = ctx
    b, seq, _ = gq.shape
    cl = cgq.shape[1]
    pair_k = 2 * GLA_KEY_DIM
    pair_v = 2 * GLA_VAL_DIM
    n_pairs = N_GLA_HEADS // 2
    n_blocks = (seq + cl) // GLA_BLOCK
    gate_off = RET_W // pair_v
    key = lambda n: pl.BlockSpec((1, n, pair_k), lambda i, p: (i, 0, p))
    key_b = lambda n: pl.BlockSpec((1, n, pair_k), lambda i, p: (i, 0, n_pairs + p))
    val = lambda n: pl.BlockSpec((1, n, pair_v), lambda i, p: (i, 0, p))
    mix = lambda n: pl.BlockSpec((1, n, pair_v), lambda i, p: (i, 0, gate_off + p))
    return pl.pallas_call(
        _gla_kernel,
        out_shape=(jax.ShapeDtypeStruct((b, seq, GLA_V), BF16), jax.ShapeDtypeStruct((b, cl, GLA_V), BF16)),
        grid=(b, n_pairs),
        in_specs=[_layer_spec(gn.shape[1:], layer),
                  key(seq), key(seq), val(seq), mix(seq), key(seq), key_b(seq),
                  key(cl), key(cl), val(cl), mix(cl), key(cl), key_b(cl)],
        out_specs=(val(seq), val(cl)),
        scratch_shapes=[pltpu.VMEM((2, seq + cl, pair_k), F32),
                        pltpu.VMEM((2, n_blocks, GLA_VAL_DIM, pair_k), F32),
                        pltpu.VMEM((2, n_blocks, 8, pair_k), F32),
                        pltpu.VMEM((2, n_blocks, GLA_VAL_DIM, pair_k), BF16),
                        pltpu.VMEM((2, n_blocks, GLA_BLOCK, GLA_BLOCK), BF16),
                        pltpu.VMEM((2, seq + cl, pair_k), BF16)],
        compiler_params=pltpu.CompilerParams(dimension_semantics=("parallel", "parallel"),
                                             vmem_limit_bytes=VMEM_LIMIT),
        name="gla_scan",
    )(gn, gq, gk, gv, g, ga, ga, cgq, cgk, cgv, cg, cga, cga)


def _mlp_kernel(x_ref, yr_ref, yg_ref, mod_ref, n2_ref, wo_ref, w1_ref, w2_ref, fg_ref, o_ref, *,
                final, ff_chunk):
    mod = mod_ref[0]
    gain = n2_ref[...] * (1.0 + mod[4:5])
    sub = math.gcd(x_ref.shape[1], SUB_ROWS)
    for t in range(x_ref.shape[1] // sub):
        rs = slice(t * sub, (t + 1) * sub)
        mix = (jnp.dot(yr_ref[0, rs, :], wo_ref[0:RET_W, :], preferred_element_type=F32)
               + jnp.dot(yg_ref[0, rs, :], wo_ref[RET_W:MIX_W, :], preferred_element_type=F32))
        x1 = x_ref[0, rs, :] + mod[2:3] * mix
        ms = jnp.mean(x1 * x1, axis=-1, keepdims=True)
        hb = (x1 * lax.rsqrt(ms + EPS) * gain + mod[3:4]).astype(BF16)
        acc = jnp.zeros(x1.shape, F32)
        for c in range(w1_ref.shape[1] // ff_chunk):
            sl = slice(c * ff_chunk, (c + 1) * ff_chunk)
            a = jnp.maximum(jnp.dot(hb, w1_ref[:, sl], preferred_element_type=F32), 0.0)
            acc = acc + jnp.dot((a * a).astype(BF16), w2_ref[sl, :], preferred_element_type=F32)
        x2 = x1 + mod[5:6] * acc
        if final:
            ms = jnp.mean(x2 * x2, axis=-1, keepdims=True)
            x2 = x2 * lax.rsqrt(ms + EPS) * fg_ref[...]
        o_ref[0, rs, :] = x2


def _out_mlp(xa, yr, yg, mod, mod_index, layer, n2, wo, w1, w2, fg, tm, final):
    b, seq, d = xa.shape
    tok = lambda n: pl.BlockSpec((1, tm, n), lambda i, j: (i, j, 0))
    return pl.pallas_call(
        functools.partial(_mlp_kernel, final=final, ff_chunk=1024),
        out_shape=jax.ShapeDtypeStruct(xa.shape, F32),
        grid=(b, seq // tm),
        in_specs=[tok(d), tok(RET_W), tok(GLA_V), pl.BlockSpec((1, 6, d), lambda i, j: (mod_index(i), 0, 0)),
                  _layer_spec(n2.shape[1:], layer), _layer_spec(wo.shape[1:], layer),
                  _layer_spec(w1.shape[1:], layer), _layer_spec(w2.shape[1:], layer), _const_spec((1, d))],
        out_specs=tok(d),
        compiler_params=pltpu.CompilerParams(dimension_semantics=("parallel", "parallel"),
                                             vmem_limit_bytes=VMEM_LIMIT),
        name="out_mlp",
    )(xa, yr, yg, mod, n2, wo, w1, w2, fg)


def kernel(x, c, ctx, c_ctx, ada_w, ada_b, norm1_g, w_in, ret_decay, gla_gate_up, gla_gate_b, gla_norm_g,
           w_out, norm2_g, w_mlp1, w_mlp2, final_g):
    batch, seq, d = x.shape
    cl = ctx.shape[1]
    depth = ada_w.shape[0]
    assert w_in.shape[2] == IN_W
    mod_rows = 16
    cs = jnp.concatenate([c, c_ctx[None, :], jnp.zeros((mod_rows - batch - 1, d), c.dtype)], axis=0)
    mod = _modulation(cs, ada_w, ada_b).reshape(depth * mod_rows, 6, d)

    lat_tables = _rope_tables(seq)
    scale = RET_HEAD_DIM ** -0.5
    ctx_rows = batch * cl
    ones = jnp.ones((ctx_rows, RET_HEAD_DIM), F32)
    ctx_tables = (ones * scale, ones * 0.0, ones, ones * 0.0)
    ctx = ctx.reshape(1, ctx_rows, d)
    ctm = math.gcd(ctx_rows, ROW_TILE)
    per_batch = lambda a: a.reshape(batch, cl, a.shape[-1])
    flat = lambda a: a.reshape(1, ctx_rows, a.shape[-1])

    w_in_b = w_in.astype(BF16)
    wdt = jnp.swapaxes(w_in_b[:, :, OFF_D:], 1, 2)
    zero = jnp.zeros((depth, GLA_GATE_RANK, GLA_K), gla_gate_up.dtype)
    up = jnp.concatenate([jnp.concatenate([gla_gate_up[:, 0], zero], axis=2),
                          jnp.concatenate([zero, gla_gate_up[:, 1]], axis=2)], axis=1).astype(BF16)
    ub = gla_gate_b.reshape(depth, 1, 2 * GLA_K)
    wo = w_out.astype(BF16)
    w1 = w_mlp1.astype(BF16)
    w2 = w_mlp2.astype(BF16)
    n1 = norm1_g.reshape(depth, 1, d)
    n2 = norm2_g.reshape(depth, 1, d)
    gn = gla_norm_g.reshape(depth, 1, GLA_VAL_DIM)
    fg = final_g.reshape(1, d)
    rd = jnp.broadcast_to(jnp.swapaxes(ret_decay, 1, 2)[..., None], (depth, N_RET_HEADS, 2, RET_BLOCK))

    tm = math.gcd(seq, ROW_TILE)
    for layer in range(depth):
        last = layer == depth - 1
        lat_mod = lambda i, layer=layer: layer * mod_rows + i
        ctx_mod = lambda i, layer=layer: layer * mod_rows + batch
        lat = _project(x, mod, lat_mod, layer, n1, w_in_b, wdt, up, ub, lat_tables, tm)
        cx = _project(ctx, mod, ctx_mod, layer, n1, w_in_b, wdt, up, ub, ctx_tables, ctm)
        rq, rk, rv, g, gq, gk, gv, ga = lat
        crq, crk, crv, cg, cgq, cgk, cgv, cga = (per_batch(a) for a in cx)
        yr, cyr = _retention(rd, layer, (rq, rk, rv, g), (crq, crk, crv, cg))
        yg, cyg = _gla(gn, layer, (gq, gk, gv, g, ga), (cgq, cgk, cgv, cg, cga))
        x = _out_mlp(x, yr, yg, mod, lat_mod, layer, n2, wo, w1, w2, fg, tm, last)
        if not last:
            ctx = _out_mlp(ctx, flat(cyr), flat(cyg), mod, ctx_mod, layer, n2, wo, w1, w2, fg, ctm, False)
    return x
```

```python
import functools
import math

import jax
import jax.numpy as jnp
from jax import lax
from jax.experimental import pallas as pl
from jax.experimental.pallas import tpu as pltpu

F32 = jnp.float32
BF16 = jnp.bfloat16

GRID_W = 64
N_RET_HEADS = 4
RET_HEAD_DIM = 128
N_GLA_HEADS = 4
GLA_KEY_DIM = 64
GLA_VAL_DIM = 128
GLA_GATE_RANK = 16
GLA_GATE_NORM = 16.0
ROPE_BASE = 10000.0
EPS = 1e-6

RET_W = N_RET_HEADS * RET_HEAD_DIM
GLA_K = N_GLA_HEADS * GLA_KEY_DIM
GLA_V = N_GLA_HEADS * GLA_VAL_DIM
MIX_W = RET_W + GLA_V

OFF_RV = 2 * RET_W
OFF_RG = 3 * RET_W
OFF_GQ = 4 * RET_W
OFF_GK = OFF_GQ + GLA_K
OFF_GV = OFF_GK + GLA_K
OFF_GG = OFF_GV + GLA_V
OFF_D = OFF_GG + GLA_V
IN_W = OFF_D + 2 * GLA_GATE_RANK

RET_BLOCK = 256
GLA_BLOCK = 128
GLA_HALF = GLA_BLOCK // 2

ROW_TILE = 1024
SUB_ROWS = 512
LOG2E = 1.4426950408889634

VMEM_LIMIT = 56 * 1024 * 1024


def _layer_spec(shape, layer):
    zeros = (0,) * len(shape)
    return pl.BlockSpec((None,) + tuple(shape), lambda *_: (layer,) + zeros, pipeline_mode=pl.Buffered(1))


def _const_spec(shape):
    zeros = (0,) * len(shape)
    return pl.BlockSpec(shape, lambda *_: zeros, pipeline_mode=pl.Buffered(1))


def _mod_kernel(cs_ref, w_ref, b_ref, o_ref):
    cs = cs_ref[...]
    s = cs * jax.nn.sigmoid(cs)
    o_ref[0] = jnp.dot(s.astype(BF16), w_ref[0].astype(BF16), preferred_element_type=F32) + b_ref[0]


def _modulation(cs, ada_w, ada_b):
    depth, d, n = ada_w.shape
    rows = cs.shape[0]
    tn = 1024
    return pl.pallas_call(
        _mod_kernel,
        out_shape=jax.ShapeDtypeStruct((depth, rows, n), F32),
        grid=(depth, n // tn),
        in_specs=[
            pl.BlockSpec((rows, d), lambda l, j: (0, 0)),
            pl.BlockSpec((1, d, tn), lambda l, j: (l, 0, j)),
            pl.BlockSpec((1, 1, tn), lambda l, j: (l, 0, j)),
        ],
        out_specs=pl.BlockSpec((1, rows, tn), lambda l, j: (l, 0, j)),
        compiler_params=pltpu.CompilerParams(dimension_semantics=("parallel", "parallel"),
                                             vmem_limit_bytes=VMEM_LIMIT),
        name="adaln_modulation",
    )(cs, ada_w, ada_b.reshape(depth, 1, n))


_QUARTER = RET_HEAD_DIM // 4


def _rope_kernel(cq_ref, sq_ref, ck_ref, sk_ref):
    shape = (GRID_W, RET_HEAD_DIM)
    p = lax.broadcasted_iota(jnp.int32, shape, 0).astype(F32)
    lane = lax.broadcasted_iota(jnp.int32, shape, 1)
    freq = (lane & (_QUARTER - 1)).astype(F32)
    ang = p * jnp.exp(freq * (-math.log(ROPE_BASE) / _QUARTER))
    cos = jnp.cos(ang)
    sin = jnp.where((lane & (2 * _QUARTER - 1)) < _QUARTER, -jnp.sin(ang), jnp.sin(ang))
    by_row = lane < 2 * _QUARTER
    scale = RET_HEAD_DIM ** -0.5
    for r in range(cq_ref.shape[0] // GRID_W):
        c_blk = jnp.where(by_row, cos[r:r + 1, :], cos)
        s_blk = jnp.where(by_row, sin[r:r + 1, :], sin)
        rs = slice(r * GRID_W, (r + 1) * GRID_W)
        cq_ref[rs, :] = c_blk * scale
        sq_ref[rs, :] = s_blk * scale
        ck_ref[rs, :] = c_blk
        sk_ref[rs, :] = s_blk


def _rope_tables(seq):
    sds = jax.ShapeDtypeStruct((seq, RET_HEAD_DIM), F32)
    return pl.pallas_call(_rope_kernel, out_shape=(sds, sds, sds, sds), name="rope_tables")()


def _silu(z):
    hz = 0.5 * z
    return hz * (1.0 + jnp.tanh(hz))


def _proj_kernel(x_ref, mod_ref, n1_ref, w_ref, up_ref, ub_ref, cq_ref, sq_ref, ck_ref, sk_ref,
                 rq_ref, rk_ref, rv_ref, g_ref, gq_ref, gk_ref, gv_ref, ga_ref):
    mod = mod_ref[0]
    gain = n1_ref[...] * (1.0 + mod[1:2])
    sub = math.gcd(x_ref.shape[1], SUB_ROWS)
    lane = lax.broadcasted_iota(jnp.int32, (sub, RET_HEAD_DIM), 1)
    first = (lane & (2 * _QUARTER - 1)) < _QUARTER

    for t in range(x_ref.shape[1] // sub):
        rs = slice(t * sub, (t + 1) * sub)
        x = x_ref[0, rs, :]
        ms = jnp.mean(x * x, axis=-1, keepdims=True)
        hb = (x * lax.rsqrt(ms + EPS) * gain + mod[0:1]).astype(BF16)

        def mm(off, n, hb=hb):
            return jnp.dot(hb, w_ref[:, off:off + n], preferred_element_type=F32)

        def rotary(z, c_ref, s_ref, o_ref, rs=rs):
            c = c_ref[rs, :]
            s = s_ref[rs, :]
            for hd in range(N_RET_HEADS):
                sl = slice(hd * RET_HEAD_DIM, (hd + 1) * RET_HEAD_DIM)
                zh = z[:, sl]
                partner = jnp.where(first, pltpu.roll(zh, RET_HEAD_DIM - _QUARTER, axis=1),
                                    pltpu.roll(zh, _QUARTER, axis=1))
                o_ref[0, rs, sl] = (zh * c + partner * s).astype(BF16)

        d = mm(OFF_D, 2 * GLA_GATE_RANK).astype(BF16)
        zg = jnp.dot(d, up_ref[...], preferred_element_type=F32) + ub_ref[...]
        soft = jnp.log2(1.0 + jnp.exp2(jnp.abs(zg) * -LOG2E))
        ga_ref[0, rs, :] = jnp.minimum(zg, 0.0) * (LOG2E / GLA_GATE_NORM) - soft * (1.0 / GLA_GATE_NORM)
        rv_ref[0, rs, :] = mm(OFF_RV, RET_W).astype(BF16)
        rotary(mm(0, RET_W), cq_ref, sq_ref, rq_ref)
        gk_ref[0, rs, :] = mm(OFF_GK, GLA_K).astype(BF16)
        rotary(mm(RET_W, RET_W), ck_ref, sk_ref, rk_ref)
        gv_ref[0, rs, :] = mm(OFF_GV, GLA_V).astype(BF16)
        g_ref[0, rs, 0:RET_W] = _silu(mm(OFF_RG, RET_W)).astype(BF16)
        gq_ref[0, rs, :] = (mm(OFF_GQ, GLA_K) * (GLA_KEY_DIM ** -0.5)).astype(BF16)
        g_ref[0, rs, RET_W:MIX_W] = _silu(mm(OFF_GG, GLA_V)).astype(BF16)


def _project(xa, mod, mod_index, layer, n1, w, up, ub, tables, tm):
    b, seq, d = xa.shape
    tok = lambda n: pl.BlockSpec((1, tm, n), lambda i, j: (i, j, 0))
    tab = pl.BlockSpec((tm, RET_HEAD_DIM), lambda i, j: (j, 0))
    sd = lambda n, dt: jax.ShapeDtypeStruct((b, seq, n), dt)
    return pl.pallas_call(
        _proj_kernel,
        out_shape=(sd(RET_W, BF16), sd(RET_W, BF16), sd(RET_W, BF16), sd(MIX_W, BF16),
                   sd(GLA_K, BF16), sd(GLA_K, BF16), sd(GLA_V, BF16), sd(2 * GLA_K, F32)),
        grid=(b, seq // tm),
        in_specs=[tok(d), pl.BlockSpec((1, 6, d), lambda i, j: (mod_index(i), 0, 0)),
                  _layer_spec(n1.shape[1:], layer),
                  _layer_spec(w.shape[1:], layer), _layer_spec(up.shape[1:], layer),
                  _layer_spec(ub.shape[1:], layer), tab, tab, tab, tab],
        out_specs=(tok(RET_W), tok(RET_W), tok(RET_W), tok(MIX_W), tok(GLA_K), tok(GLA_K), tok(GLA_V),
                   tok(2 * GLA_K)),
        compiler_params=pltpu.CompilerParams(dimension_semantics=("parallel", "parallel"),
                                             vmem_limit_bytes=VMEM_LIMIT),
        name="norm_project",
    )(xa, mod, n1, w, up, ub, *tables)


def _ret_kernel(rd_ref, q_ref, k_ref, v_ref, g_ref, cq_ref, ck_ref, cv_ref, cg_ref,
                y_ref, cy_ref, kv_ref, st_ref, sc_ref):
    t_blk = RET_BLOCK
    dh = RET_HEAD_DIM
    lg = jnp.log1p(-jnp.exp(rd_ref[0]))
    lgf, lgb = lg[0:1, :], lg[1:2, :]
    lgf_h, lgb_h = lgf[:, :dh], lgb[:, :dh]
    ti = lax.broadcasted_iota(jnp.int32, (t_blk, t_blk), 0)
    si = lax.broadcasted_iota(jnp.int32, (t_blk, t_blk), 1)
    diff = (ti - si).astype(F32)
    mask = jnp.exp(jnp.where(diff >= 0, diff * lgf, -diff * lgb))
    tr = lax.broadcasted_iota(jnp.int32, (t_blk, dh), 0).astype(F32)
    q_dec = jnp.concatenate([jnp.exp((tr + 1.0) * lgf_h), jnp.exp((t_blk - tr) * lgb_h)], axis=1)
    k_dec_f = jnp.exp((t_blk - 1.0 - tr) * lgf_h)
    k_dec_b = jnp.exp(tr * lgb_h)
    blk_f = jnp.exp(t_blk * lgf_h)
    blk_b = jnp.exp(t_blk * lgb_h)
    tn_dims = (((0,), (0,)), ((), ()))
    nt_dims = (((1,), (1,)), ((), ()))
    n_ctx = cq_ref.shape[1] // t_blk
    n_lat = q_ref.shape[1] // t_blk
    n_tot = n_ctx + n_lat
    fwd = slice(0, dh)
    bwd = slice(dh, 2 * dh)

    def rows(i):
        return pl.ds(i * t_blk, t_blk)

    def kv_block(kr, vr, i, j):
        r = rows(i)
        v = vr[0, r, :].astype(F32)
        vv = jnp.concatenate([(v * k_dec_f).astype(BF16), (v * k_dec_b).astype(BF16)], axis=1)
        kv_ref[j] = lax.dot_general(kr[0, r, :], vv, tn_dims, preferred_element_type=F32)

    def score_block(qr, kr, i, j):
        r = rows(i)
        a = lax.dot_general(qr[0, r, :], kr[0, r, :], nt_dims, preferred_element_type=F32) * mask
        sc_ref[j] = a.astype(BF16)

    def out_block(qr, vr, gr, yr, i, j):
        r = rows(i)
        q, v = qr[0, r, :], vr[0, r, :]
        o = jnp.dot(sc_ref[j], v, preferred_element_type=F32)
        inter = jnp.dot(q, st_ref[j], preferred_element_type=F32) * q_dec
        o = o + inter[:, fwd] + inter[:, bwd]
        mu = jnp.mean(o, axis=-1, keepdims=True)
        dlt = o - mu
        var = jnp.mean(dlt * dlt, axis=-1, keepdims=True)
        yr[0, r, :] = (dlt * lax.rsqrt(var + EPS) * gr[0, r, :].astype(F32)).astype(BF16)

    def for_blocks(n, fn):
        for i in range(n):
            fn(i)

    for_blocks(n_ctx, lambda i: kv_block(ck_ref, cv_ref, i, i))
    for_blocks(n_lat, lambda i: kv_block(k_ref, v_ref, i, n_ctx + i))

    def step_f(j, s):
        st_ref[j, :, fwd] = s.astype(BF16)
        return blk_f * s + kv_ref[j, :, fwd]

    def step_b(j, s):
        st_ref[j, :, bwd] = s.astype(BF16)
        return blk_b * s + kv_ref[j, :, bwd]

    zero = jnp.zeros((dh, dh), F32)
    lax.fori_loop(0, n_tot, step_f, zero)
    s_b = lax.fori_loop(0, n_ctx, lambda t, s: step_b(n_ctx - 1 - t, s), zero)
    lax.fori_loop(0, n_lat, lambda t, s: step_b(n_tot - 1 - t, s), s_b)

    for_blocks(n_ctx, lambda i: score_block(cq_ref, ck_ref, i, i))
    for_blocks(n_lat, lambda i: score_block(q_ref, k_ref, i, n_ctx + i))
    for_blocks(n_ctx, lambda i: out_block(cq_ref, cv_ref, cg_ref, cy_ref, i, i))
    for_blocks(n_lat, lambda i: out_block(q_ref, v_ref, g_ref, y_ref, i, n_ctx + i))


def _retention(rd, layer, lat, ctx):
    rq, rk, rv, g = lat
    crq, crk, crv, cg = ctx
    b, seq, _ = rq.shape
    cl = crq.shape[1]
    dh = RET_HEAD_DIM
    n_blocks = (seq + cl) // RET_BLOCK
    head = lambda n: pl.BlockSpec((1, n, dh), lambda i, h: (i, 0, h))
    return pl.pallas_call(
        _ret_kernel,
        out_shape=(jax.ShapeDtypeStruct((b, seq, RET_W), BF16), jax.ShapeDtypeStruct((b, cl, RET_W), BF16)),
        grid=(b, N_RET_HEADS),
        in_specs=[pl.BlockSpec((None, 1, 2, RET_BLOCK), lambda i, h: (layer, h, 0, 0)),
                  head(seq), head(seq), head(seq), head(seq), head(cl), head(cl), head(cl), head(cl)],
        out_specs=(head(seq), head(cl)),
        scratch_shapes=[pltpu.VMEM((n_blocks, dh, 2 * dh), F32), pltpu.VMEM((n_blocks, dh, 2 * dh), BF16),
                        pltpu.VMEM((n_blocks, RET_BLOCK, RET_BLOCK), BF16)],
        compiler_params=pltpu.CompilerParams(dimension_semantics=("parallel", "parallel"),
                                             vmem_limit_bytes=VMEM_LIMIT),
        name="retention_scan",
    )(rd, rq, rk, rv, g, crq, crk, crv, cg)


def _gla_kernel(gn_ref, q_ref, k_ref, v_ref, g_ref, af_ref, ab_ref, cq_ref, ck_ref, cv_ref, cg_ref,
                caf_ref, cab_ref, y_ref, cy_ref, c_ref, kv_ref, dec_ref, st_ref, sc_ref, qin_ref):
    t_blk = GLA_BLOCK
    dv = GLA_VAL_DIM
    dk2 = 2 * GLA_KEY_DIM
    assert t_blk == dk2 == dv
    ti = lax.broadcasted_iota(jnp.int32, (t_blk, t_blk), 0)
    si = lax.broadcasted_iota(jnp.int32, (t_blk, t_blk), 1)
    lower = si <= ti
    cum_f = lower.astype(BF16)
    head0 = si < GLA_KEY_DIM
    tn_dims = (((0,), (0,)), ((), ()))
    nt_dims = (((1,), (1,)), ((), ()))

    def rows(i):
        return pl.ds(i * t_blk, t_blk)

    def split2(a):
        hi = a.astype(BF16)
        return [hi, (a - hi.astype(F32)).astype(BF16)]

    dirs = ((GLA_HALF - 1, t_blk - 1), (GLA_HALF, 0))

    def cum_block(ars, i, row0):
        af = ars[0][0, rows(i), :]
        ab = ars[1][0, rows(i), :]
        z = jnp.dot(cum_f, jnp.concatenate(split2(af) + split2(ab), axis=1), preferred_element_type=F32)
        c_ref[0, rows(row0 + i), :] = z[:, 0:dk2] + z[:, dk2:2 * dk2]
        pb = z[:, 2 * dk2:3 * dk2] + z[:, 3 * dk2:4 * dk2]
        c_ref[1, rows(row0 + i), :] = pb[t_blk - 1:t_blk, :] - pb + ab

    def mid_block(qr, kr, vr, i, j, row0):
        r = rows(i)
        cr = rows(row0 + i)
        q = qr[0, r, :].astype(F32)
        k = kr[0, r, :].astype(F32)
        v = vr[0, r, :]
        sc, k_end = [], []
        for d, (mid, edge) in enumerate(dirs):
            c = c_ref[d, cr, :]
            c_mid = c[mid:mid + 1, :]
            c_edge = c[edge:edge + 1, :]
            q_mid = q * jnp.exp2(c - c_mid)
            k_mid = k * jnp.exp2(c_mid - c)
            k_end.append((k_mid * jnp.exp2(c_edge - c_mid)).astype(BF16))
            qin_ref[d, cr, :] = (q_mid * jnp.exp2(c_mid)).astype(BF16)
            dec_ref[d, j] = jnp.broadcast_to(jnp.exp2(c_edge), (8, dk2))
            k_mid_b = k_mid.astype(BF16)
            zero = jnp.zeros_like(k_mid_b)
            k_heads = jnp.concatenate([jnp.where(head0, k_mid_b, zero), jnp.where(head0, zero, k_mid_b)], axis=0)
            sc.append(lax.dot_general(q_mid.astype(BF16), k_heads, nt_dims, preferred_element_type=F32))
        kv = lax.dot_general(v, jnp.concatenate(k_end, axis=1), tn_dims, preferred_element_type=F32)
        for d in range(2):
            kvd = kv[:, d * dk2:(d + 1) * dk2]
            kv_ref[d, j] = jnp.where(head0, kvd[0:dv, :], kvd[dv:2 * dv, :])
        for hd in range(2):
            hs = slice(hd * t_blk, (hd + 1) * t_blk)
            sc_ref[hd, j] = jnp.where(lower, sc[0][:, hs], sc[1][:, hs]).astype(BF16)

    def out_block(qr, vr, gr, yr, i, j, row0):
        r = rows(i)
        cr = rows(row0 + i)
        v = vr[0, r, :]
        s_exp = []
        for d in range(2):
            s = st_ref[d, j]
            zero = jnp.zeros_like(s)
            s_exp.append(jnp.concatenate([jnp.where(head0, s, zero), jnp.where(head0, zero, s)], axis=0))
        q_in = jnp.concatenate([qin_ref[0, cr, :], qin_ref[1, cr, :]], axis=1)
        o = jnp.concatenate([jnp.dot(sc_ref[hd, j], v[:, hd * dv:(hd + 1) * dv], preferred_element_type=F32)
                             for hd in range(2)], axis=1)
        o = o + lax.dot_general(q_in, jnp.concatenate(s_exp, axis=1), nt_dims, preferred_element_type=F32)
        gate = gr[0, r, :].astype(F32)
        gn = gn_ref[...]
        for hd in range(2):
            sl = slice(hd * dv, (hd + 1) * dv)
            oh = o[:, sl]
            ms = jnp.mean(oh * oh, axis=-1, keepdims=True)
            yr[0, r, sl] = (oh * lax.rsqrt(ms + EPS) * gn * gate[:, sl]).astype(BF16)

    def for_blocks(n, fn):
        for i in range(n):
            fn(i)

    n_ctx = cq_ref.shape[1] // t_blk
    n_lat = q_ref.shape[1] // t_blk
    n_tot = n_ctx + n_lat
    for_blocks(n_ctx, lambda i: cum_block((caf_ref, cab_ref), i, 0))
    for_blocks(n_lat, lambda i: cum_block((af_ref, ab_ref), i, n_ctx))
    for_blocks(n_ctx, lambda i: mid_block(cq_ref, ck_ref, cv_ref, i, i, 0))
    for_blocks(n_lat, lambda i: mid_block(q_ref, k_ref, v_ref, i, n_ctx + i, n_ctx))

    def step(d, j, s):
        st_ref[d, j] = s.astype(BF16)
        return dec_ref[d, j, 0:1, :] * s + kv_ref[d, j]

    zero = jnp.zeros((dv, dk2), F32)
    lax.fori_loop(0, n_tot, lambda j, s: step(0, j, s), zero)
    s_b = lax.fori_loop(0, n_ctx, lambda t, s: step(1, n_ctx - 1 - t, s), zero)
    lax.fori_loop(0, n_lat, lambda t, s: step(1, n_tot - 1 - t, s), s_b)

    for_blocks(n_ctx, lambda i: out_block(cq_ref, cv_ref, cg_ref, cy_ref, i, i, 0))
    for_blocks(n_lat, lambda i: out_block(q_ref, v_ref, g_ref, y_ref, i, n_ctx + i, n_ctx))


def _gla(gn, layer, lat, ctx):
    gq, gk, gv, g, ga = lat
    cgq, cgk, cgv, cg, cga = ctx
    b, seq, _ = gq.shape
    cl = cgq.shape[1]
    pair_k = 2 * GLA_KEY_DIM
    pair_v = 2 * GLA_VAL_DIM
    n_pairs = N_GLA_HEADS // 2
    n_blocks = (seq + cl) // GLA_BLOCK
    gate_off = RET_W // pair_v
    key = lambda n: pl.BlockSpec((1, n, pair_k), lambda i, p: (i, 0, p))
    key_b = lambda n: pl.BlockSpec((1, n, pair_k), lambda i, p: (i, 0, n_pairs + p))
    val = lambda n: pl.BlockSpec((1, n, pair_v), lambda i, p: (i, 0, p))
    mix = lambda n: pl.BlockSpec((1, n, pair_v), lambda i, p: (i, 0, gate_off + p))
    return pl.pallas_call(
        _gla_kernel,
        out_shape=(jax.ShapeDtypeStruct((b, seq, GLA_V), BF16), jax.ShapeDtypeStruct((b, cl, GLA_V), BF16)),
        grid=(b, n_pairs),
        in_specs=[_layer_spec(gn.shape[1:], layer),
                  key(seq), key(seq), val(seq), mix(seq), key(seq), key_b(seq),
                  key(cl), key(cl), val(cl), mix(cl), key(cl), key_b(cl)],
        out_specs=(val(seq), val(cl)),
        scratch_shapes=[pltpu.VMEM((2, seq + cl, pair_k), F32),
                        pltpu.VMEM((2, n_blocks, GLA_VAL_DIM, pair_k), F32),
                        pltpu.VMEM((2, n_blocks, 8, pair_k), F32),
                        pltpu.VMEM((2, n_blocks, GLA_VAL_DIM, pair_k), BF16),
                        pltpu.VMEM((2, n_blocks, GLA_BLOCK, GLA_BLOCK), BF16),
                        pltpu.VMEM((2, seq + cl, pair_k), BF16)],
        compiler_params=pltpu.CompilerParams(dimension_semantics=("parallel", "parallel"),
                                             vmem_limit_bytes=VMEM_LIMIT),
        name="gla_scan",
    )(gn, gq, gk, gv, g, ga, ga, cgq, cgk, cgv, cg, cga, cga)


def _mlp_kernel(x_ref, yr_ref, yg_ref, mod_ref, n2_ref, wo_ref, w1_ref, w2_ref, fg_ref, o_ref, *,
                final, ff_chunk):
    mod = mod_ref[0]
    gain = n2_ref[...] * (1.0 + mod[4:5])
    sub = math.gcd(x_ref.shape[1], SUB_ROWS)
    for t in range(x_ref.shape[1] // sub):
        rs = slice(t * sub, (t + 1) * sub)
        mix = (jnp.dot(yr_ref[0, rs, :], wo_ref[0:RET_W, :], preferred_element_type=F32)
               + jnp.dot(yg_ref[0, rs, :], wo_ref[RET_W:MIX_W, :], preferred_element_type=F32))
        x1 = x_ref[0, rs, :] + mod[2:3] * mix
        ms = jnp.mean(x1 * x1, axis=-1, keepdims=True)
        hb = (x1 * lax.rsqrt(ms + EPS) * gain + mod[3:4]).astype(BF16)
        acc = jnp.zeros(x1.shape, F32)
        for c in range(w1_ref.shape[1] // ff_chunk):
            sl = slice(c * ff_chunk, (c + 1) * ff_chunk)
            a = jnp.maximum(jnp.dot(hb, w1_ref[:, sl], preferred_element_type=F32), 0.0)
            acc = acc + jnp.dot((a * a).astype(BF16), w2_ref[sl, :], preferred_element_type=F32)
        x2 = x1 + mod[5:6] * acc
        if final:
            ms = jnp.mean(x2 * x2, axis=-1, keepdims=True)
            x2 = x2 * lax.rsqrt(ms + EPS) * fg_ref[...]
        o_ref[0, rs, :] = x2


def _out_mlp(xa, yr, yg, mod, mod_index, layer, n2, wo, w1, w2, fg, tm, final):
    b, seq, d = xa.shape
    tok = lambda n: pl.BlockSpec((1, tm, n), lambda i, j: (i, j, 0))
    return pl.pallas_call(
        functools.partial(_mlp_kernel, final=final, ff_chunk=1024),
        out_shape=jax.ShapeDtypeStruct(xa.shape, F32),
        grid=(b, seq // tm),
        in_specs=[tok(d), tok(RET_W), tok(GLA_V), pl.BlockSpec((1, 6, d), lambda i, j: (mod_index(i), 0, 0)),
                  _layer_spec(n2.shape[1:], layer), _layer_spec(wo.shape[1:], layer),
                  _layer_spec(w1.shape[1:], layer), _layer_spec(w2.shape[1:], layer), _const_spec((1, d))],
        out_specs=tok(d),
        compiler_params=pltpu.CompilerParams(dimension_semantics=("parallel", "parallel"),
                                             vmem_limit_bytes=VMEM_LIMIT),
        name="out_mlp",
    )(xa, yr, yg, mod, n2, wo, w1, w2, fg)


def kernel(x, c, ctx, c_ctx, ada_w, ada_b, norm1_g, w_in, ret_decay, gla_gate_up, gla_gate_b, gla_norm_g,
           w_out, norm2_g, w_mlp1, w_mlp2, final_g):
    batch, seq, d = x.shape
    cl = ctx.shape[1]
    depth = ada_w.shape[0]
    assert w_in.shape[2] == IN_W
    mod_rows = 16
    cs = jnp.concatenate([c, c_ctx[None, :], jnp.zeros((mod_rows - batch - 1, d), c.dtype)], axis=0)
    mod = _modulation(cs, ada_w, ada_b).reshape(depth * mod_rows, 6, d)

    lat_tables = _rope_tables(seq)
    scale = RET_HEAD_DIM ** -0.5
    ones = jnp.ones((cl, RET_HEAD_DIM), F32)
    ctx_tables = (ones * scale, ones * 0.0, ones, ones * 0.0)

    w_in_b = w_in.astype(BF16)
    zero = jnp.zeros((depth, GLA_GATE_RANK, GLA_K), gla_gate_up.dtype)
    up = jnp.concatenate([jnp.concatenate([gla_gate_up[:, 0], zero], axis=2),
                          jnp.concatenate([zero, gla_gate_up[:, 1]], axis=2)], axis=1).astype(BF16)
    ub = gla_gate_b.reshape(depth, 1, 2 * GLA_K)
    wo = w_out.astype(BF16)
    w1 = w_mlp1.astype(BF16)
    w2 = w_mlp2.astype(BF16)
    n1 = norm1_g.reshape(depth, 1, d)
    n2 = norm2_g.reshape(depth, 1, d)
    gn = gla_norm_g.reshape(depth, 1, GLA_VAL_DIM)
    fg = final_g.reshape(1, d)
    rd = jnp.broadcast_to(jnp.swapaxes(ret_decay, 1, 2)[..., None], (depth, N_RET_HEADS, 2, RET_BLOCK))

    tm = math.gcd(seq, ROW_TILE)
    for layer in range(depth):
        last = layer == depth - 1
        lat_mod = lambda i, layer=layer: layer * mod_rows + i
        ctx_mod = lambda i, layer=layer: layer * mod_rows + batch
        lat = _project(x, mod, lat_mod, layer, n1, w_in_b, up, ub, lat_tables, tm)
        cx = _project(ctx, mod, ctx_mod, layer, n1, w_in_b, up, ub, ctx_tables, cl)
        rq, rk, rv, g, gq, gk, gv, ga = lat
        crq, crk, crv, cg, cgq, cgk, cgv, cga = cx
        yr, cyr = _retention(rd, layer, (rq, rk, rv, g), (crq, crk, crv, cg))
        yg, cyg = _gla(gn, layer, (gq, gk, gv, g, ga), (cgq, cgk, cgv, cg, cga))
        x = _out_mlp(x, yr, yg, mod, lat_mod, layer, n2, wo, w1, w2, fg, tm, last)
        if not last:
            ctx = _out_mlp(ctx, cyr, cyg, mod, ctx_mod, layer, n2, wo, w1, w2, fg, cl, False)
    return x
```

```python
import functools
import math

import jax
import jax.numpy as jnp
from jax import lax
from jax.experimental import pallas as pl
from jax.experimental.pallas import tpu as pltpu

F32 = jnp.float32
BF16 = jnp.bfloat16

GRID_W = 64
N_RET_HEADS = 4
RET_HEAD_DIM = 128
N_GLA_HEADS = 4
GLA_KEY_DIM = 64
GLA_VAL_DIM = 128
GLA_GATE_RANK = 16
GLA_GATE_NORM = 16.0
ROPE_BASE = 10000.0
EPS = 1e-6

RET_W = N_RET_HEADS * RET_HEAD_DIM
GLA_K = N_GLA_HEADS * GLA_KEY_DIM
GLA_V = N_GLA_HEADS * GLA_VAL_DIM
MIX_W = RET_W + GLA_V

OFF_RV = 2 * RET_W
OFF_RG = 3 * RET_W
OFF_GQ = 4 * RET_W
OFF_GK = OFF_GQ + GLA_K
OFF_GV = OFF_GK + GLA_K
OFF_GG = OFF_GV + GLA_V
OFF_D = OFF_GG + GLA_V
IN_W = OFF_D + 2 * GLA_GATE_RANK

RET_BLOCK = 256
GLA_BLOCK = 128
GLA_HALF = GLA_BLOCK // 2

ROW_TILE = 1024
SUB_ROWS = 512
LOG2E = 1.4426950408889634

VMEM_LIMIT = 56 * 1024 * 1024


def _layer_spec(shape, layer):
    zeros = (0,) * len(shape)
    return pl.BlockSpec((None,) + tuple(shape), lambda *_: (layer,) + zeros, pipeline_mode=pl.Buffered(1))


def _const_spec(shape):
    zeros = (0,) * len(shape)
    return pl.BlockSpec(shape, lambda *_: zeros, pipeline_mode=pl.Buffered(1))


def _mod_kernel(cs_ref, w_ref, b_ref, o_ref):
    cs = cs_ref[...]
    s = cs * jax.nn.sigmoid(cs)
    o_ref[0] = jnp.dot(s.astype(BF16), w_ref[0].astype(BF16), preferred_element_type=F32) + b_ref[0]


def _modulation(cs, ada_w, ada_b):
    depth, d, n = ada_w.shape
    rows = cs.shape[0]
    tn = 1024
    return pl.pallas_call(
        _mod_kernel,
        out_shape=jax.ShapeDtypeStruct((depth, rows, n), F32),
        grid=(depth, n // tn),
        in_specs=[
            pl.BlockSpec((rows, d), lambda l, j: (0, 0)),
            pl.BlockSpec((1, d, tn), lambda l, j: (l, 0, j)),
            pl.BlockSpec((1, 1, tn), lambda l, j: (l, 0, j)),
        ],
        out_specs=pl.BlockSpec((1, rows, tn), lambda l, j: (l, 0, j)),
        compiler_params=pltpu.CompilerParams(dimension_semantics=("parallel", "parallel"),
                                             vmem_limit_bytes=VMEM_LIMIT),
        name="adaln_modulation",
    )(cs, ada_w, ada_b.reshape(depth, 1, n))


_QUARTER = RET_HEAD_DIM // 4


def _rope_kernel(cq_ref, sq_ref, ck_ref, sk_ref):
    shape = (GRID_W, RET_HEAD_DIM)
    p = lax.broadcasted_iota(jnp.int32, shape, 0).astype(F32)
    lane = lax.broadcasted_iota(jnp.int32, shape, 1)
    freq = (lane & (_QUARTER - 1)).astype(F32)
    ang = p * jnp.exp(freq * (-math.log(ROPE_BASE) / _QUARTER))
    cos = jnp.cos(ang)
    sin = jnp.where((lane & (2 * _QUARTER - 1)) < _QUARTER, -jnp.sin(ang), jnp.sin(ang))
    by_row = lane < 2 * _QUARTER
    scale = RET_HEAD_DIM ** -0.5
    for r in range(cq_ref.shape[0] // GRID_W):
        c_blk = jnp.where(by_row, cos[r:r + 1, :], cos)
        s_blk = jnp.where(by_row, sin[r:r + 1, :], sin)
        rs = slice(r * GRID_W, (r + 1) * GRID_W)
        cq_ref[rs, :] = c_blk * scale
        sq_ref[rs, :] = s_blk * scale
        ck_ref[rs, :] = c_blk
        sk_ref[rs, :] = s_blk


def _rope_tables(seq):
    sds = jax.ShapeDtypeStruct((seq, RET_HEAD_DIM), F32)
    return pl.pallas_call(_rope_kernel, out_shape=(sds, sds, sds, sds), name="rope_tables")()


def _silu(z):
    hz = 0.5 * z
    return hz * (1.0 + jnp.tanh(hz))


def _proj_kernel(x_ref, mod_ref, n1_ref, w_ref, up_ref, ub_ref, cq_ref, sq_ref, ck_ref, sk_ref,
                 rq_ref, rk_ref, rv_ref, g_ref, gq_ref, gk_ref, gv_ref, ga_ref):
    mod = mod_ref[0]
    gain = n1_ref[...] * (1.0 + mod[1:2])
    sub = math.gcd(x_ref.shape[1], SUB_ROWS)
    lane = lax.broadcasted_iota(jnp.int32, (sub, RET_HEAD_DIM), 1)
    first = (lane & (2 * _QUARTER - 1)) < _QUARTER

    for t in range(x_ref.shape[1] // sub):
        rs = slice(t * sub, (t + 1) * sub)
        x = x_ref[0, rs, :]
        ms = jnp.mean(x * x, axis=-1, keepdims=True)
        hb = (x * lax.rsqrt(ms + EPS) * gain + mod[0:1]).astype(BF16)

        def mm(off, n, hb=hb):
            return jnp.dot(hb, w_ref[:, off:off + n], preferred_element_type=F32)

        def rotary(z, c_ref, s_ref, o_ref, rs=rs):
            c = c_ref[rs, :]
            s = s_ref[rs, :]
            for hd in range(N_RET_HEADS):
                sl = slice(hd * RET_HEAD_DIM, (hd + 1) * RET_HEAD_DIM)
                zh = z[:, sl]
                partner = jnp.where(first, pltpu.roll(zh, RET_HEAD_DIM - _QUARTER, axis=1),
                                    pltpu.roll(zh, _QUARTER, axis=1))
                o_ref[0, rs, sl] = (zh * c + partner * s).astype(BF16)

        d = mm(OFF_D, 2 * GLA_GATE_RANK).astype(BF16)
        zg = jnp.dot(d, up_ref[...], preferred_element_type=F32) + ub_ref[...]
        soft = jnp.log2(1.0 + jnp.exp2(jnp.abs(zg) * -LOG2E))
        ga_ref[0, rs, :] = jnp.minimum(zg, 0.0) * (LOG2E / GLA_GATE_NORM) - soft * (1.0 / GLA_GATE_NORM)
        rv_ref[0, rs, :] = mm(OFF_RV, RET_W).astype(BF16)
        rotary(mm(0, RET_W), cq_ref, sq_ref, rq_ref)
        gk_ref[0, rs, :] = mm(OFF_GK, GLA_K).astype(BF16)
        rotary(mm(RET_W, RET_W), ck_ref, sk_ref, rk_ref)
        gv_ref[0, rs, :] = mm(OFF_GV, GLA_V).astype(BF16)
        g_ref[0, rs, 0:RET_W] = _silu(mm(OFF_RG, RET_W)).astype(BF16)
        gq_ref[0, rs, :] = (mm(OFF_GQ, GLA_K) * (GLA_KEY_DIM ** -0.5)).astype(BF16)
        g_ref[0, rs, RET_W:MIX_W] = _silu(mm(OFF_GG, GLA_V)).astype(BF16)


def _project(xa, mod, mod_index, layer, n1, w, up, ub, tables, tm):
    b, seq, d = xa.shape
    tok = lambda n: pl.BlockSpec((1, tm, n), lambda i, j: (i, j, 0))
    tab = pl.BlockSpec((tm, RET_HEAD_DIM), lambda i, j: (j, 0))
    sd = lambda n, dt: jax.ShapeDtypeStruct((b, seq, n), dt)
    return pl.pallas_call(
        _proj_kernel,
        out_shape=(sd(RET_W, BF16), sd(RET_W, BF16), sd(RET_W, BF16), sd(MIX_W, BF16),
                   sd(GLA_K, BF16), sd(GLA_K, BF16), sd(GLA_V, BF16), sd(2 * GLA_K, F32)),
        grid=(b, seq // tm),
        in_specs=[tok(d), pl.BlockSpec((1, 6, d), lambda i, j: (mod_index(i), 0, 0)),
                  _layer_spec(n1.shape[1:], layer),
                  _layer_spec(w.shape[1:], layer), _layer_spec(up.shape[1:], layer),
                  _layer_spec(ub.shape[1:], layer), tab, tab, tab, tab],
        out_specs=(tok(RET_W), tok(RET_W), tok(RET_W), tok(MIX_W), tok(GLA_K), tok(GLA_K), tok(GLA_V),
                   tok(2 * GLA_K)),
        compiler_params=pltpu.CompilerParams(dimension_semantics=("parallel", "parallel"),
                                             vmem_limit_bytes=VMEM_LIMIT),
        name="norm_project",
    )(xa, mod, n1, w, up, ub, *tables)


def _ret_kernel(rd_ref, q_ref, k_ref, v_ref, g_ref, cq_ref, ck_ref, cv_ref, cg_ref,
                y_ref, cy_ref, kv_ref, st_ref, sc_ref):
    t_blk = RET_BLOCK
    dh = RET_HEAD_DIM
    lg = jnp.log1p(-jnp.exp(rd_ref[0]))
    lgf, lgb = lg[0:1, :], lg[1:2, :]
    lgf_h, lgb_h = lgf[:, :dh], lgb[:, :dh]
    ti = lax.broadcasted_iota(jnp.int32, (t_blk, t_blk), 0)
    si = lax.broadcasted_iota(jnp.int32, (t_blk, t_blk), 1)
    diff = (ti - si).astype(F32)
    mask = jnp.exp(jnp.where(diff >= 0, diff * lgf, -diff * lgb))
    tr = lax.broadcasted_iota(jnp.int32, (t_blk, dh), 0).astype(F32)
    q_dec = jnp.concatenate([jnp.exp((tr + 1.0) * lgf_h), jnp.exp((t_blk - tr) * lgb_h)], axis=1)
    k_dec_f = jnp.exp((t_blk - 1.0 - tr) * lgf_h)
    k_dec_b = jnp.exp(tr * lgb_h)
    blk_f = jnp.exp(t_blk * lgf_h)
    blk_b = jnp.exp(t_blk * lgb_h)
    tn_dims = (((0,), (0,)), ((), ()))
    nt_dims = (((1,), (1,)), ((), ()))
    n_ctx = cq_ref.shape[1] // t_blk
    n_lat = q_ref.shape[1] // t_blk
    n_tot = n_ctx + n_lat
    fwd = slice(0, dh)
    bwd = slice(dh, 2 * dh)

    def rows(i):
        return pl.ds(i * t_blk, t_blk)

    def kv_block(kr, vr, i, j):
        r = rows(i)
        v = vr[0, r, :].astype(F32)
        vv = jnp.concatenate([(v * k_dec_f).astype(BF16), (v * k_dec_b).astype(BF16)], axis=1)
        kv_ref[j] = lax.dot_general(kr[0, r, :], vv, tn_dims, preferred_element_type=F32)

    def score_block(qr, kr, i, j):
        r = rows(i)
        a = lax.dot_general(qr[0, r, :], kr[0, r, :], nt_dims, preferred_element_type=F32) * mask
        sc_ref[j] = a.astype(BF16)

    def out_block(qr, vr, gr, yr, i, j):
        r = rows(i)
        q, v = qr[0, r, :], vr[0, r, :]
        half = t_blk // 2
        o = (jnp.dot(sc_ref[j, :, 0:half], v[0:half, :], preferred_element_type=F32)
             + jnp.dot(sc_ref[j, :, half:t_blk], v[half:t_blk, :], preferred_element_type=F32))
        inter = jnp.dot(q, st_ref[j], preferred_element_type=F32) * q_dec
        o = o + inter[:, fwd] + inter[:, bwd]
        mu = jnp.mean(o, axis=-1, keepdims=True)
        dlt = o - mu
        var = jnp.mean(dlt * dlt, axis=-1, keepdims=True)
        yr[0, r, :] = (dlt * lax.rsqrt(var + EPS) * gr[0, r, :].astype(F32)).astype(BF16)

    def for_blocks(n, fn):
        for i in range(n):
            fn(i)

    for_blocks(n_ctx, lambda i: kv_block(ck_ref, cv_ref, i, i))
    for_blocks(n_lat, lambda i: kv_block(k_ref, v_ref, i, n_ctx + i))

    def step_f(j, s):
        st_ref[j, :, fwd] = s.astype(BF16)
        return blk_f * s + kv_ref[j, :, fwd]

    def step_b(j, s):
        st_ref[j, :, bwd] = s.astype(BF16)
        return blk_b * s + kv_ref[j, :, bwd]

    zero = jnp.zeros((dh, dh), F32)
    lax.fori_loop(0, n_tot, step_f, zero)
    s_b = lax.fori_loop(0, n_ctx, lambda t, s: step_b(n_ctx - 1 - t, s), zero)
    lax.fori_loop(0, n_lat, lambda t, s: step_b(n_tot - 1 - t, s), s_b)

    for_blocks(n_ctx, lambda i: score_block(cq_ref, ck_ref, i, i))
    for_blocks(n_lat, lambda i: score_block(q_ref, k_ref, i, n_ctx + i))
    for_blocks(n_ctx, lambda i: out_block(cq_ref, cv_ref, cg_ref, cy_ref, i, i))
    for_blocks(n_lat, lambda i: out_block(q_ref, v_ref, g_ref, y_ref, i, n_ctx + i))


def _retention(rd, layer, lat, ctx):
    rq, rk, rv, g = lat
    crq, crk, crv, cg = ctx
    b, seq, _ = rq.shape
    cl = crq.shape[1]
    dh = RET_HEAD_DIM
    n_blocks = (seq + cl) // RET_BLOCK
    head = lambda n: pl.BlockSpec((1, n, dh), lambda i, h: (i, 0, h))
    return pl.pallas_call(
        _ret_kernel,
        out_shape=(jax.ShapeDtypeStruct((b, seq, RET_W), BF16), jax.ShapeDtypeStruct((b, cl, RET_W), BF16)),
        grid=(b, N_RET_HEADS),
        in_specs=[pl.BlockSpec((None, 1, 2, RET_BLOCK), lambda i, h: (layer, h, 0, 0)),
                  head(seq), head(seq), head(seq), head(seq), head(cl), head(cl), head(cl), head(cl)],
        out_specs=(head(seq), head(cl)),
        scratch_shapes=[pltpu.VMEM((n_blocks, dh, 2 * dh), F32), pltpu.VMEM((n_blocks, dh, 2 * dh), BF16),
                        pltpu.VMEM((n_blocks, RET_BLOCK, RET_BLOCK), BF16)],
        compiler_params=pltpu.CompilerParams(dimension_semantics=("parallel", "parallel"),
                                             vmem_limit_bytes=VMEM_LIMIT),
        name="retention_scan",
    )(rd, rq, rk, rv, g, crq, crk, crv, cg)


def _gla_kernel(gn_ref, q_ref, k_ref, v_ref, g_ref, af_ref, ab_ref, cq_ref, ck_ref, cv_ref, cg_ref,
                caf_ref, cab_ref, y_ref, cy_ref, c_ref, kv_ref, dec_ref, st_ref, sc_ref, qin_ref):
    t_blk = GLA_BLOCK
    dv = GLA_VAL_DIM
    dk2 = 2 * GLA_KEY_DIM
    assert t_blk == dk2 == dv
    ti = lax.broadcasted_iota(jnp.int32, (t_blk, t_blk), 0)
    si = lax.broadcasted_iota(jnp.int32, (t_blk, t_blk), 1)
    lower = si <= ti
    cum_f = lower.astype(BF16)
    head0 = si < GLA_KEY_DIM
    tn_dims = (((0,), (0,)), ((), ()))
    nt_dims = (((1,), (1,)), ((), ()))

    def rows(i):
        return pl.ds(i * t_blk, t_blk)

    def split2(a):
        hi = a.astype(BF16)
        return [hi, (a - hi.astype(F32)).astype(BF16)]

    dirs = ((GLA_HALF - 1, t_blk - 1), (GLA_HALF, 0))

    def cum_block(ars, i, row0):
        af = ars[0][0, rows(i), :]
        ab = ars[1][0, rows(i), :]
        z = jnp.dot(cum_f, jnp.concatenate(split2(af) + split2(ab), axis=1), preferred_element_type=F32)
        c_ref[0, rows(row0 + i), :] = z[:, 0:dk2] + z[:, dk2:2 * dk2]
        pb = z[:, 2 * dk2:3 * dk2] + z[:, 3 * dk2:4 * dk2]
        c_ref[1, rows(row0 + i), :] = pb[t_blk - 1:t_blk, :] - pb + ab

    def mid_block(qr, kr, vr, i, j, row0):
        r = rows(i)
        cr = rows(row0 + i)
        q = qr[0, r, :].astype(F32)
        k = kr[0, r, :].astype(F32)
        v = vr[0, r, :]
        sc, k_end = [], []
        for d, (mid, edge) in enumerate(dirs):
            c = c_ref[d, cr, :]
            c_mid = c[mid:mid + 1, :]
            c_edge = c[edge:edge + 1, :]
            q_mid = q * jnp.exp2(c - c_mid)
            k_mid = k * jnp.exp2(c_mid - c)
            k_end.append((k_mid * jnp.exp2(c_edge - c_mid)).astype(BF16))
            qin_ref[d, cr, :] = (q_mid * jnp.exp2(c_mid)).astype(BF16)
            dec_ref[d, j] = jnp.broadcast_to(jnp.exp2(c_edge), (8, dk2))
            k_mid_b = k_mid.astype(BF16)
            zero = jnp.zeros_like(k_mid_b)
            k_heads = jnp.concatenate([jnp.where(head0, k_mid_b, zero), jnp.where(head0, zero, k_mid_b)], axis=0)
            sc.append(lax.dot_general(q_mid.astype(BF16), k_heads, nt_dims, preferred_element_type=F32))
        kv = lax.dot_general(v, jnp.concatenate(k_end, axis=1), tn_dims, preferred_element_type=F32)
        for d in range(2):
            kvd = kv[:, d * dk2:(d + 1) * dk2]
            kv_ref[d, j] = jnp.where(head0, kvd[0:dv, :], kvd[dv:2 * dv, :])
        for hd in range(2):
            hs = slice(hd * t_blk, (hd + 1) * t_blk)
            sc_ref[hd, j] = jnp.where(lower, sc[0][:, hs], sc[1][:, hs]).astype(BF16)

    def out_block(qr, vr, gr, yr, i, j, row0):
        r = rows(i)
        cr = rows(row0 + i)
        v = vr[0, r, :]
        s_exp = []
        for d in range(2):
            s = st_ref[d, j]
            zero = jnp.zeros_like(s)
            s_exp.append(jnp.concatenate([jnp.where(head0, s, zero), jnp.where(head0, zero, s)], axis=0))
        o = jnp.concatenate([jnp.dot(sc_ref[hd, j], v[:, hd * dv:(hd + 1) * dv], preferred_element_type=F32)
                             for hd in range(2)], axis=1)
        for d in range(2):
            o = o + lax.dot_general(qin_ref[d, cr, :], s_exp[d], nt_dims, preferred_element_type=F32)
        gate = gr[0, r, :].astype(F32)
        gn = gn_ref[...]
        for hd in range(2):
            sl = slice(hd * dv, (hd + 1) * dv)
            oh = o[:, sl]
            ms = jnp.mean(oh * oh, axis=-1, keepdims=True)
            yr[0, r, sl] = (oh * lax.rsqrt(ms + EPS) * gn * gate[:, sl]).astype(BF16)

    def for_blocks(n, fn):
        for i in range(n):
            fn(i)

    n_ctx = cq_ref.shape[1] // t_blk
    n_lat = q_ref.shape[1] // t_blk
    n_tot = n_ctx + n_lat
    for_blocks(n_ctx, lambda i: cum_block((caf_ref, cab_ref), i, 0))
    for_blocks(n_lat, lambda i: cum_block((af_ref, ab_ref), i, n_ctx))
    for_blocks(n_ctx, lambda i: mid_block(cq_ref, ck_ref, cv_ref, i, i, 0))
    for_blocks(n_lat, lambda i: mid_block(q_ref, k_ref, v_ref, i, n_ctx + i, n_ctx))

    def step(d, j, s):
        st_ref[d, j] = s.astype(BF16)
        return dec_ref[d, j, 0:1, :] * s + kv_ref[d, j]

    zero = jnp.zeros((dv, dk2), F32)
    lax.fori_loop(0, n_tot, lambda j, s: step(0, j, s), zero)
    s_b = lax.fori_loop(0, n_ctx, lambda t, s: step(1, n_ctx - 1 - t, s), zero)
    lax.fori_loop(0, n_lat, lambda t, s: step(1, n_tot - 1 - t, s), s_b)

    for_blocks(n_ctx, lambda i: out_block(cq_ref, cv_ref, cg_ref, cy_ref, i, i, 0))
    for_blocks(n_lat, lambda i: out_block(q_ref, v_ref, g_ref, y_ref, i, n_ctx + i, n_ctx))


def _gla(gn, layer, lat, ctx):
    gq, gk, gv, g, ga = lat
    cgq, cgk, cgv, cg, cga = ctx
    b, seq, _ = gq.shape
    cl = cgq.shape[1]
    pair_k = 2 * GLA_KEY_DIM
    pair_v = 2 * GLA_VAL_DIM
    n_pairs = N_GLA_HEADS // 2
    n_blocks = (seq + cl) // GLA_BLOCK
    gate_off = RET_W // pair_v
    key = lambda n: pl.BlockSpec((1, n, pair_k), lambda i, p: (i, 0, p))
    key_b = lambda n: pl.BlockSpec((1, n, pair_k), lambda i, p: (i, 0, n_pairs + p))
    val = lambda n: pl.BlockSpec((1, n, pair_v), lambda i, p: (i, 0, p))
    mix = lambda n: pl.BlockSpec((1, n, pair_v), lambda i, p: (i, 0, gate_off + p))
    return pl.pallas_call(
        _gla_kernel,
        out_shape=(jax.ShapeDtypeStruct((b, seq, GLA_V), BF16), jax.ShapeDtypeStruct((b, cl, GLA_V), BF16)),
        grid=(b, n_pairs),
        in_specs=[_layer_spec(gn.shape[1:], layer),
                  key(seq), key(seq), val(seq), mix(seq), key(seq), key_b(seq),
                  key(cl), key(cl), val(cl), mix(cl), key(cl), key_b(cl)],
        out_specs=(val(seq), val(cl)),
        scratch_shapes=[pltpu.VMEM((2, seq + cl, pair_k), F32),
                        pltpu.VMEM((2, n_blocks, GLA_VAL_DIM, pair_k), F32),
                        pltpu.VMEM((2, n_blocks, 8, pair_k), F32),
                        pltpu.VMEM((2, n_blocks, GLA_VAL_DIM, pair_k), BF16),
                        pltpu.VMEM((2, n_blocks, GLA_BLOCK, GLA_BLOCK), BF16),
                        pltpu.VMEM((2, seq + cl, pair_k), BF16)],
        compiler_params=pltpu.CompilerParams(dimension_semantics=("parallel", "parallel"),
                                             vmem_limit_bytes=VMEM_LIMIT),
        name="gla_scan",
    )(gn, gq, gk, gv, g, ga, ga, cgq, cgk, cgv, cg, cga, cga)


def _mlp_kernel(x_ref, yr_ref, yg_ref, mod_ref, n2_ref, wo_ref, w1_ref, w2_ref, fg_ref, o_ref, *,
                final, ff_chunk):
    mod = mod_ref[0]
    gain = n2_ref[...] * (1.0 + mod[4:5])
    sub = x_ref.shape[1]
    for t in range(x_ref.shape[1] // sub):
        rs = slice(t * sub, (t + 1) * sub)
        mix = (jnp.dot(yr_ref[0, rs, :], wo_ref[0:RET_W, :], preferred_element_type=F32)
               + jnp.dot(yg_ref[0, rs, :], wo_ref[RET_W:MIX_W, :], preferred_element_type=F32))
        x1 = x_ref[0, rs, :] + mod[2:3] * mix
        ms = jnp.mean(x1 * x1, axis=-1, keepdims=True)
        hb = (x1 * lax.rsqrt(ms + EPS) * gain + mod[3:4]).astype(BF16)
        acc = jnp.zeros(x1.shape, F32)
        for c in range(w1_ref.shape[1] // ff_chunk):
            sl = slice(c * ff_chunk, (c + 1) * ff_chunk)
            a = jnp.maximum(jnp.dot(hb, w1_ref[:, sl], preferred_element_type=F32), 0.0)
            acc = acc + jnp.dot((a * a).astype(BF16), w2_ref[sl, :], preferred_element_type=F32)
        x2 = x1 + mod[5:6] * acc
        if final:
            ms = jnp.mean(x2 * x2, axis=-1, keepdims=True)
            x2 = x2 * lax.rsqrt(ms + EPS) * fg_ref[...]
        o_ref[0, rs, :] = x2


def _out_mlp(xa, yr, yg, mod, mod_index, layer, n2, wo, w1, w2, fg, tm, final):
    b, seq, d = xa.shape
    tok = lambda n: pl.BlockSpec((1, tm, n), lambda i, j: (i, j, 0))
    return pl.pallas_call(
        functools.partial(_mlp_kernel, final=final, ff_chunk=1024),
        out_shape=jax.ShapeDtypeStruct(xa.shape, F32),
        grid=(b, seq // tm),
        in_specs=[tok(d), tok(RET_W), tok(GLA_V), pl.BlockSpec((1, 6, d), lambda i, j: (mod_index(i), 0, 0)),
                  _layer_spec(n2.shape[1:], layer), _layer_spec(wo.shape[1:], layer),
                  _layer_spec(w1.shape[1:], layer), _layer_spec(w2.shape[1:], layer), _const_spec((1, d))],
        out_specs=tok(d),
        compiler_params=pltpu.CompilerParams(dimension_semantics=("parallel", "parallel"),
                                             vmem_limit_bytes=VMEM_LIMIT),
        name="out_mlp",
    )(xa, yr, yg, mod, n2, wo, w1, w2, fg)


def kernel(x, c, ctx, c_ctx, ada_w, ada_b, norm1_g, w_in, ret_decay, gla_gate_up, gla_gate_b, gla_norm_g,
           w_out, norm2_g, w_mlp1, w_mlp2, final_g):
    batch, seq, d = x.shape
    cl = ctx.shape[1]
    depth = ada_w.shape[0]
    assert w_in.shape[2] == IN_W
    mod_rows = 16
    cs = jnp.concatenate([c, c_ctx[None, :], jnp.zeros((mod_rows - batch - 1, d), c.dtype)], axis=0)
    mod = _modulation(cs, ada_w, ada_b).reshape(depth * mod_rows, 6, d)

    lat_tables = _rope_tables(seq)
    scale = RET_HEAD_DIM ** -0.5
    ones = jnp.ones((cl, RET_HEAD_DIM), F32)
    ctx_tables = (ones * scale, ones * 0.0, ones, ones * 0.0)

    w_in_b = w_in.astype(BF16)
    zero = jnp.zeros((depth, GLA_GATE_RANK, GLA_K), gla_gate_up.dtype)
    up = jnp.concatenate([jnp.concatenate([gla_gate_up[:, 0], zero], axis=2),
                          jnp.concatenate([zero, gla_gate_up[:, 1]], axis=2)], axis=1).astype(BF16)
    ub = gla_gate_b.reshape(depth, 1, 2 * GLA_K)
    wo = w_out.astype(BF16)
    w1 = w_mlp1.astype(BF16)
    w2 = w_mlp2.astype(BF16)
    n1 = norm1_g.reshape(depth, 1, d)
    n2 = norm2_g.reshape(depth, 1, d)
    gn = gla_norm_g.reshape(depth, 1, GLA_VAL_DIM)
    fg = final_g.reshape(1, d)
    rd = jnp.broadcast_to(jnp.swapaxes(ret_decay, 1, 2)[..., None], (depth, N_RET_HEADS, 2, RET_BLOCK))

    tm = math.gcd(seq, ROW_TILE)
    for layer in range(depth):
        last = layer == depth - 1
        lat_mod = lambda i, layer=layer: layer * mod_rows + i
        ctx_mod = lambda i, layer=layer: layer * mod_rows + batch
        lat = _project(x, mod, lat_mod, layer, n1, w_in_b, up, ub, lat_tables, tm)
        cx = _project(ctx, mod, ctx_mod, layer, n1, w_in_b, up, ub, ctx_tables, cl)
        rq, rk, rv, g, gq, gk, gv, ga = lat
        crq, crk, crv, cg, cgq, cgk, cgv, cga = cx
        yr, cyr = _retention(rd, layer, (rq, rk, rv, g), (crq, crk, crv, cg))
        yg, cyg = _gla(gn, layer, (gq, gk, gv, g, ga), (cgq, cgk, cgv, cg, cga))
        x = _out_mlp(x, yr, yg, mod, lat_mod, layer, n2, wo, w1, w2, fg, tm, last)
        if not last:
            ctx = _out_mlp(ctx, cyr, cyg, mod, ctx_mod, layer, n2, wo, w1, w2, fg, cl, False)
    return x
```

```python
import functools
import math

import jax
import jax.numpy as jnp
from jax import lax
from jax.experimental import pallas as pl
from jax.experimental.pallas import tpu as pltpu

F32 = jnp.float32
BF16 = jnp.bfloat16

GRID_W = 64
N_RET_HEADS = 4
RET_HEAD_DIM = 128
N_GLA_HEADS = 4
GLA_KEY_DIM = 64
GLA_VAL_DIM = 128
GLA_GATE_RANK = 16
GLA_GATE_NORM = 16.0
ROPE_BASE = 10000.0
EPS = 1e-6

RET_W = N_RET_HEADS * RET_HEAD_DIM
GLA_K = N_GLA_HEADS * GLA_KEY_DIM
GLA_V = N_GLA_HEADS * GLA_VAL_DIM
MIX_W = RET_W + GLA_V

OFF_RV = 2 * RET_W
OFF_RG = 3 * RET_W
OFF_GQ = 4 * RET_W
OFF_GK = OFF_GQ + GLA_K
OFF_GV = OFF_GK + GLA_K
OFF_GG = OFF_GV + GLA_V
OFF_D = OFF_GG + GLA_V
IN_W = OFF_D + 2 * GLA_GATE_RANK

RET_BLOCK = 256
GLA_BLOCK = 128
GLA_HALF = GLA_BLOCK // 2

ROW_TILE = 1024
LOG2E = 1.4426950408889634

VMEM_LIMIT = 56 * 1024 * 1024


def _layer_spec(shape, layer):
    zeros = (0,) * len(shape)
    return pl.BlockSpec((None,) + tuple(shape), lambda *_: (layer,) + zeros, pipeline_mode=pl.Buffered(1))


def _const_spec(shape):
    zeros = (0,) * len(shape)
    return pl.BlockSpec(shape, lambda *_: zeros, pipeline_mode=pl.Buffered(1))


def _mod_kernel(cs_ref, w_ref, b_ref, o_ref):
    cs = cs_ref[...]
    s = cs * jax.nn.sigmoid(cs)
    o_ref[0] = jnp.dot(s.astype(BF16), w_ref[0].astype(BF16), preferred_element_type=F32) + b_ref[0]


def _modulation(cs, ada_w, ada_b):
    depth, d, n = ada_w.shape
    rows = cs.shape[0]
    tn = 1024
    return pl.pallas_call(
        _mod_kernel,
        out_shape=jax.ShapeDtypeStruct((depth, rows, n), F32),
        grid=(depth, n // tn),
        in_specs=[
            pl.BlockSpec((rows, d), lambda l, j: (0, 0)),
            pl.BlockSpec((1, d, tn), lambda l, j: (l, 0, j)),
            pl.BlockSpec((1, 1, tn), lambda l, j: (l, 0, j)),
        ],
        out_specs=pl.BlockSpec((1, rows, tn), lambda l, j: (l, 0, j)),
        compiler_params=pltpu.CompilerParams(dimension_semantics=("parallel", "parallel"),
                                             vmem_limit_bytes=VMEM_LIMIT),
        name="adaln_modulation",
    )(cs, ada_w, ada_b.reshape(depth, 1, n))


_QUARTER = RET_HEAD_DIM // 4


def _rope_kernel(cq_ref, sq_ref, ck_ref, sk_ref):
    shape = (GRID_W, RET_HEAD_DIM)
    p = lax.broadcasted_iota(jnp.int32, shape, 0).astype(F32)
    lane = lax.broadcasted_iota(jnp.int32, shape, 1)
    freq = (lane & (_QUARTER - 1)).astype(F32)
    ang = p * jnp.exp(freq * (-math.log(ROPE_BASE) / _QUARTER))
    cos = jnp.cos(ang)
    sin = jnp.where((lane & (2 * _QUARTER - 1)) < _QUARTER, -jnp.sin(ang), jnp.sin(ang))
    by_row = lane < 2 * _QUARTER
    scale = RET_HEAD_DIM ** -0.5
    for r in range(cq_ref.shape[0] // GRID_W):
        c_blk = jnp.where(by_row, cos[r:r + 1, :], cos)
        s_blk = jnp.where(by_row, sin[r:r + 1, :], sin)
        rs = slice(r * GRID_W, (r + 1) * GRID_W)
        cq_ref[rs, :] = c_blk * scale
        sq_ref[rs, :] = s_blk * scale
        ck_ref[rs, :] = c_blk
        sk_ref[rs, :] = s_blk


def _rope_tables(seq):
    sds = jax.ShapeDtypeStruct((seq, RET_HEAD_DIM), F32)
    return pl.pallas_call(_rope_kernel, out_shape=(sds, sds, sds, sds), name="rope_tables")()


def _silu(z):
    hz = 0.5 * z
    return hz * (1.0 + jnp.tanh(hz))


def _proj_kernel(x_ref, mod_ref, n1_ref, w_ref, up_ref, ub_ref, cq_ref, sq_ref, ck_ref, sk_ref,
                 rq_ref, rk_ref, rv_ref, g_ref, gq_ref, gk_ref, gv_ref, ga_ref):
    x = x_ref[0]
    mod = mod_ref[0]
    gain = n1_ref[...] * (1.0 + mod[1:2])
    ms = jnp.mean(x * x, axis=-1, keepdims=True)
    hb = (x * lax.rsqrt(ms + EPS) * gain + mod[0:1]).astype(BF16)

    def mm(off, n):
        return jnp.dot(hb, w_ref[:, off:off + n], preferred_element_type=F32)

    lane = lax.broadcasted_iota(jnp.int32, (x.shape[0], RET_HEAD_DIM), 1)
    first = (lane & (2 * _QUARTER - 1)) < _QUARTER

    def rotary(z, c_ref, s_ref, o_ref):
        c = c_ref[...]
        s = s_ref[...]
        for hd in range(N_RET_HEADS):
            sl = slice(hd * RET_HEAD_DIM, (hd + 1) * RET_HEAD_DIM)
            zh = z[:, sl]
            partner = jnp.where(first, pltpu.roll(zh, RET_HEAD_DIM - _QUARTER, axis=1),
                                pltpu.roll(zh, _QUARTER, axis=1))
            o_ref[0, :, sl] = (zh * c + partner * s).astype(BF16)

    d = mm(OFF_D, 2 * GLA_GATE_RANK).astype(BF16)
    zg = jnp.dot(d, up_ref[...], preferred_element_type=F32) + ub_ref[...]
    soft = jnp.log2(1.0 + jnp.exp2(jnp.abs(zg) * -LOG2E))
    ga_ref[0] = jnp.minimum(zg, 0.0) * (LOG2E / GLA_GATE_NORM) - soft * (1.0 / GLA_GATE_NORM)
    rv_ref[0] = mm(OFF_RV, RET_W).astype(BF16)
    rotary(mm(0, RET_W), cq_ref, sq_ref, rq_ref)
    gk_ref[0] = mm(OFF_GK, GLA_K).astype(BF16)
    rotary(mm(RET_W, RET_W), ck_ref, sk_ref, rk_ref)
    gv_ref[0] = mm(OFF_GV, GLA_V).astype(BF16)
    g_ref[0, :, 0:RET_W] = _silu(mm(OFF_RG, RET_W)).astype(BF16)
    gq_ref[0] = (mm(OFF_GQ, GLA_K) * (GLA_KEY_DIM ** -0.5)).astype(BF16)
    g_ref[0, :, RET_W:MIX_W] = _silu(mm(OFF_GG, GLA_V)).astype(BF16)


def _project(xa, mod, mod_index, layer, n1, w, up, ub, tables, tm):
    b, seq, d = xa.shape
    tok = lambda n: pl.BlockSpec((1, tm, n), lambda i, j: (i, j, 0))
    tab = pl.BlockSpec((tm, RET_HEAD_DIM), lambda i, j: (j, 0))
    sd = lambda n, dt: jax.ShapeDtypeStruct((b, seq, n), dt)
    return pl.pallas_call(
        _proj_kernel,
        out_shape=(sd(RET_W, BF16), sd(RET_W, BF16), sd(RET_W, BF16), sd(MIX_W, BF16),
                   sd(GLA_K, BF16), sd(GLA_K, BF16), sd(GLA_V, BF16), sd(2 * GLA_K, F32)),
        grid=(b, seq // tm),
        in_specs=[tok(d), pl.BlockSpec((1, 6, d), lambda i, j: (mod_index(i), 0, 0)),
                  _layer_spec(n1.shape[1:], layer),
                  _layer_spec(w.shape[1:], layer), _layer_spec(up.shape[1:], layer),
                  _layer_spec(ub.shape[1:], layer), tab, tab, tab, tab],
        out_specs=(tok(RET_W), tok(RET_W), tok(RET_W), tok(MIX_W), tok(GLA_K), tok(GLA_K), tok(GLA_V),
                   tok(2 * GLA_K)),
        compiler_params=pltpu.CompilerParams(dimension_semantics=("parallel", "parallel"),
                                             vmem_limit_bytes=VMEM_LIMIT),
        name="norm_project",
    )(xa, mod, n1, w, up, ub, *tables)


def _ret_kernel(rd_ref, q_ref, k_ref, v_ref, g_ref, cq_ref, ck_ref, cv_ref, cg_ref,
                y_ref, cy_ref, kv_ref, st_ref, sc_ref):
    t_blk = RET_BLOCK
    dh = RET_HEAD_DIM
    lg = jnp.log1p(-jnp.exp(rd_ref[0]))
    lgf, lgb = lg[0:1, :], lg[1:2, :]
    lgf_h, lgb_h = lgf[:, :dh], lgb[:, :dh]
    ti = lax.broadcasted_iota(jnp.int32, (t_blk, t_blk), 0)
    si = lax.broadcasted_iota(jnp.int32, (t_blk, t_blk), 1)
    diff = (ti - si).astype(F32)
    mask = jnp.exp(jnp.where(diff >= 0, diff * lgf, -diff * lgb))
    tr = lax.broadcasted_iota(jnp.int32, (t_blk, dh), 0).astype(F32)
    q_dec = jnp.concatenate([jnp.exp((tr + 1.0) * lgf_h), jnp.exp((t_blk - tr) * lgb_h)], axis=1)
    k_dec_f = jnp.exp((t_blk - 1.0 - tr) * lgf_h)
    k_dec_b = jnp.exp(tr * lgb_h)
    blk_f = jnp.exp(t_blk * lgf_h)
    blk_b = jnp.exp(t_blk * lgb_h)
    tn_dims = (((0,), (0,)), ((), ()))
    nt_dims = (((1,), (1,)), ((), ()))
    n_ctx = cq_ref.shape[1] // t_blk
    n_lat = q_ref.shape[1] // t_blk
    n_tot = n_ctx + n_lat
    fwd = slice(0, dh)
    bwd = slice(dh, 2 * dh)

    def rows(i):
        return pl.ds(i * t_blk, t_blk)

    def kv_block(kr, vr, i, j):
        r = rows(i)
        v = vr[0, r, :].astype(F32)
        vv = jnp.concatenate([(v * k_dec_f).astype(BF16), (v * k_dec_b).astype(BF16)], axis=1)
        kv_ref[j] = lax.dot_general(kr[0, r, :], vv, tn_dims, preferred_element_type=F32)

    def score_block(qr, kr, i, j):
        r = rows(i)
        a = lax.dot_general(qr[0, r, :], kr[0, r, :], nt_dims, preferred_element_type=F32) * mask
        sc_ref[j] = a.astype(BF16)

    def out_block(qr, vr, gr, yr, i, j):
        r = rows(i)
        q, v = qr[0, r, :], vr[0, r, :]
        o = jnp.dot(sc_ref[j], v, preferred_element_type=F32)
        inter = jnp.dot(q, st_ref[j], preferred_element_type=F32) * q_dec
        o = o + inter[:, fwd] + inter[:, bwd]
        mu = jnp.mean(o, axis=-1, keepdims=True)
        dlt = o - mu
        var = jnp.mean(dlt * dlt, axis=-1, keepdims=True)
        yr[0, r, :] = (dlt * lax.rsqrt(var + EPS) * gr[0, r, :].astype(F32)).astype(BF16)

    def for_blocks(n, fn):
        for i in range(n):
            fn(i)

    for_blocks(n_ctx, lambda i: kv_block(ck_ref, cv_ref, i, i))
    for_blocks(n_lat, lambda i: kv_block(k_ref, v_ref, i, n_ctx + i))

    def step_f(j, s):
        st_ref[j, :, fwd] = s.astype(BF16)
        return blk_f * s + kv_ref[j, :, fwd]

    def step_b(j, s):
        st_ref[j, :, bwd] = s.astype(BF16)
        return blk_b * s + kv_ref[j, :, bwd]

    zero = jnp.zeros((dh, dh), F32)
    lax.fori_loop(0, n_tot, step_f, zero)
    s_b = lax.fori_loop(0, n_ctx, lambda t, s: step_b(n_ctx - 1 - t, s), zero)
    lax.fori_loop(0, n_lat, lambda t, s: step_b(n_tot - 1 - t, s), s_b)

    for_blocks(n_ctx, lambda i: score_block(cq_ref, ck_ref, i, i))
    for_blocks(n_lat, lambda i: score_block(q_ref, k_ref, i, n_ctx + i))
    for_blocks(n_ctx, lambda i: out_block(cq_ref, cv_ref, cg_ref, cy_ref, i, i))
    for_blocks(n_lat, lambda i: out_block(q_ref, v_ref, g_ref, y_ref, i, n_ctx + i))


def _retention(rd, layer, lat, ctx):
    rq, rk, rv, g = lat
    crq, crk, crv, cg = ctx
    b, seq, _ = rq.shape
    cl = crq.shape[1]
    dh = RET_HEAD_DIM
    n_blocks = (seq + cl) // RET_BLOCK
    head = lambda n: pl.BlockSpec((1, n, dh), lambda i, h: (i, 0, h))
    return pl.pallas_call(
        _ret_kernel,
        out_shape=(jax.ShapeDtypeStruct((b, seq, RET_W), BF16), jax.ShapeDtypeStruct((b, cl, RET_W), BF16)),
        grid=(b, N_RET_HEADS),
        in_specs=[pl.BlockSpec((None, 1, 2, RET_BLOCK), lambda i, h: (layer, h, 0, 0)),
                  head(seq), head(seq), head(seq), head(seq), head(cl), head(cl), head(cl), head(cl)],
        out_specs=(head(seq), head(cl)),
        scratch_shapes=[pltpu.VMEM((n_blocks, dh, 2 * dh), F32), pltpu.VMEM((n_blocks, dh, 2 * dh), BF16),
                        pltpu.VMEM((n_blocks, RET_BLOCK, RET_BLOCK), BF16)],
        compiler_params=pltpu.CompilerParams(dimension_semantics=("parallel", "parallel"),
                                             vmem_limit_bytes=VMEM_LIMIT),
        name="retention_scan",
    )(rd, rq, rk, rv, g, crq, crk, crv, cg)


def _gla_kernel(gn_ref, q_ref, k_ref, v_ref, g_ref, af_ref, ab_ref, cq_ref, ck_ref, cv_ref, cg_ref,
                caf_ref, cab_ref, y_ref, cy_ref, c_ref, kv_ref, dec_ref, st_ref, sc_ref, qin_ref):
    t_blk = GLA_BLOCK
    dv = GLA_VAL_DIM
    dk2 = 2 * GLA_KEY_DIM
    assert t_blk == dk2 == dv
    ti = lax.broadcasted_iota(jnp.int32, (t_blk, t_blk), 0)
    si = lax.broadcasted_iota(jnp.int32, (t_blk, t_blk), 1)
    lower = si <= ti
    cum_f = lower.astype(BF16)
    head0 = si < GLA_KEY_DIM
    tn_dims = (((0,), (0,)), ((), ()))
    nt_dims = (((1,), (1,)), ((), ()))

    def rows(i):
        return pl.ds(i * t_blk, t_blk)

    def split2(a):
        hi = a.astype(BF16)
        return [hi, (a - hi.astype(F32)).astype(BF16)]

    dirs = ((GLA_HALF - 1, t_blk - 1), (GLA_HALF, 0))

    def cum_block(ars, i, row0):
        af = ars[0][0, rows(i), :]
        ab = ars[1][0, rows(i), :]
        z = jnp.dot(cum_f, jnp.concatenate(split2(af) + split2(ab), axis=1), preferred_element_type=F32)
        c_ref[0, rows(row0 + i), :] = z[:, 0:dk2] + z[:, dk2:2 * dk2]
        pb = z[:, 2 * dk2:3 * dk2] + z[:, 3 * dk2:4 * dk2]
        c_ref[1, rows(row0 + i), :] = pb[t_blk - 1:t_blk, :] - pb + ab

    def mid_block(qr, kr, vr, i, j, row0):
        r = rows(i)
        cr = rows(row0 + i)
        q = qr[0, r, :].astype(F32)
        k = kr[0, r, :].astype(F32)
        v = vr[0, r, :]
        sc, k_end = [], []
        for d, (mid, edge) in enumerate(dirs):
            c = c_ref[d, cr, :]
            c_mid = c[mid:mid + 1, :]
            c_edge = c[edge:edge + 1, :]
            q_mid = q * jnp.exp2(c - c_mid)
            k_mid = k * jnp.exp2(c_mid - c)
            k_end.append((k_mid * jnp.exp2(c_edge - c_mid)).astype(BF16))
            qin_ref[d, cr, :] = (q_mid * jnp.exp2(c_mid)).astype(BF16)
            dec_ref[d, j] = jnp.broadcast_to(jnp.exp2(c_edge), (8, dk2))
            k_mid_b = k_mid.astype(BF16)
            zero = jnp.zeros_like(k_mid_b)
            k_heads = jnp.concatenate([jnp.where(head0, k_mid_b, zero), jnp.where(head0, zero, k_mid_b)], axis=0)
            sc.append(lax.dot_general(q_mid.astype(BF16), k_heads, nt_dims, preferred_element_type=F32))
        kv = lax.dot_general(v, jnp.concatenate(k_end, axis=1), tn_dims, preferred_element_type=F32)
        for d in range(2):
            kvd = kv[:, d * dk2:(d + 1) * dk2]
            kv_ref[d, j] = jnp.where(head0, kvd[0:dv, :], kvd[dv:2 * dv, :])
        for hd in range(2):
            hs = slice(hd * t_blk, (hd + 1) * t_blk)
            sc_ref[hd, j] = jnp.where(lower, sc[0][:, hs], sc[1][:, hs]).astype(BF16)

    def out_block(qr, vr, gr, yr, i, j, row0):
        r = rows(i)
        cr = rows(row0 + i)
        v = vr[0, r, :]
        s_exp = []
        for d in range(2):
            s = st_ref[d, j]
            zero = jnp.zeros_like(s)
            s_exp.append(jnp.concatenate([jnp.where(head0, s, zero), jnp.where(head0, zero, s)], axis=0))
        q_in = jnp.concatenate([qin_ref[0, cr, :], qin_ref[1, cr, :]], axis=1)
        o = jnp.concatenate([jnp.dot(sc_ref[hd, j], v[:, hd * dv:(hd + 1) * dv], preferred_element_type=F32)
                             for hd in range(2)], axis=1)
        o = o + lax.dot_general(q_in, jnp.concatenate(s_exp, axis=1), nt_dims, preferred_element_type=F32)
        gate = gr[0, r, :].astype(F32)
        gn = gn_ref[...]
        for hd in range(2):
            sl = slice(hd * dv, (hd + 1) * dv)
            oh = o[:, sl]
            ms = jnp.mean(oh * oh, axis=-1, keepdims=True)
            yr[0, r, sl] = (oh * lax.rsqrt(ms + EPS) * gn * gate[:, sl]).astype(BF16)

    def for_blocks(n, fn):
        for i in range(n):
            fn(i)

    n_ctx = cq_ref.shape[1] // t_blk
    n_lat = q_ref.shape[1] // t_blk
    n_tot = n_ctx + n_lat
    for_blocks(n_ctx, lambda i: cum_block((caf_ref, cab_ref), i, 0))
    for_blocks(n_lat, lambda i: cum_block((af_ref, ab_ref), i, n_ctx))
    for_blocks(n_ctx, lambda i: mid_block(cq_ref, ck_ref, cv_ref, i, i, 0))
    for_blocks(n_lat, lambda i: mid_block(q_ref, k_ref, v_ref, i, n_ctx + i, n_ctx))

    def step(d, j, s):
        st_ref[d, j] = s.astype(BF16)
        return dec_ref[d, j, 0:1, :] * s + kv_ref[d, j]

    zero = jnp.zeros((dv, dk2), F32)
    lax.fori_loop(0, n_tot, lambda j, s: step(0, j, s), zero)
    s_b = lax.fori_loop(0, n_ctx, lambda t, s: step(1, n_ctx - 1 - t, s), zero)
    lax.fori_loop(0, n_lat, lambda t, s: step(1, n_tot - 1 - t, s), s_b)

    for_blocks(n_ctx, lambda i: out_block(cq_ref, cv_ref, cg_ref, cy_ref, i, i, 0))
    for_blocks(n_lat, lambda i: out_block(q_ref, v_ref, g_ref, y_ref, i, n_ctx + i, n_ctx))


def _gla(gn, layer, lat, ctx):
    gq, gk, gv, g, ga = lat
    cgq, cgk, cgv, cg, cga = ctx
    b, seq, _ = gq.shape
    cl = cgq.shape[1]
    pair_k = 2 * GLA_KEY_DIM
    pair_v = 2 * GLA_VAL_DIM
    n_pairs = N_GLA_HEADS // 2
    n_blocks = (seq + cl) // GLA_BLOCK
    gate_off = RET_W // pair_v
    key = lambda n: pl.BlockSpec((1, n, pair_k), lambda i, p: (i, 0, p))
    key_b = lambda n: pl.BlockSpec((1, n, pair_k), lambda i, p: (i, 0, n_pairs + p))
    val = lambda n: pl.BlockSpec((1, n, pair_v), lambda i, p: (i, 0, p))
    mix = lambda n: pl.BlockSpec((1, n, pair_v), lambda i, p: (i, 0, gate_off + p))
    return pl.pallas_call(
        _gla_kernel,
        out_shape=(jax.ShapeDtypeStruct((b, seq, GLA_V), BF16), jax.ShapeDtypeStruct((b, cl, GLA_V), BF16)),
        grid=(b, n_pairs),
        in_specs=[_layer_spec(gn.shape[1:], layer),
                  key(seq), key(seq), val(seq), mix(seq), key(seq), key_b(seq),
                  key(cl), key(cl), val(cl), mix(cl), key(cl), key_b(cl)],
        out_specs=(val(seq), val(cl)),
        scratch_shapes=[pltpu.VMEM((2, seq + cl, pair_k), F32),
                        pltpu.VMEM((2, n_blocks, GLA_VAL_DIM, pair_k), F32),
                        pltpu.VMEM((2, n_blocks, 8, pair_k), F32),
                        pltpu.VMEM((2, n_blocks, GLA_VAL_DIM, pair_k), BF16),
                        pltpu.VMEM((2, n_blocks, GLA_BLOCK, GLA_BLOCK), BF16),
                        pltpu.VMEM((2, seq + cl, pair_k), BF16)],
        compiler_params=pltpu.CompilerParams(dimension_semantics=("parallel", "parallel"),
                                             vmem_limit_bytes=VMEM_LIMIT),
        name="gla_scan",
    )(gn, gq, gk, gv, g, ga, ga, cgq, cgk, cgv, cg, cga, cga)


def _mlp_kernel(x_ref, yr_ref, yg_ref, mod_ref, n2_ref, wo_ref, w1_ref, w2_ref, fg_ref, o_ref, *,
                final, ff_chunk):
    mod = mod_ref[0]
    mix = (jnp.dot(yr_ref[0], wo_ref[0:RET_W, :], preferred_element_type=F32)
           + jnp.dot(yg_ref[0], wo_ref[RET_W:MIX_W, :], preferred_element_type=F32))
    x1 = x_ref[0] + mod[2:3] * mix
    gain = n2_ref[...] * (1.0 + mod[4:5])
    ms = jnp.mean(x1 * x1, axis=-1, keepdims=True)
    hb = (x1 * lax.rsqrt(ms + EPS) * gain + mod[3:4]).astype(BF16)
    acc = jnp.zeros(x1.shape, F32)
    for c in range(w1_ref.shape[1] // ff_chunk):
        sl = slice(c * ff_chunk, (c + 1) * ff_chunk)
        a = jnp.maximum(jnp.dot(hb, w1_ref[:, sl], preferred_element_type=F32), 0.0)
        acc = acc + jnp.dot((a * a).astype(BF16), w2_ref[sl, :], preferred_element_type=F32)
    x2 = x1 + mod[5:6] * acc
    if final:
        ms = jnp.mean(x2 * x2, axis=-1, keepdims=True)
        x2 = x2 * lax.rsqrt(ms + EPS) * fg_ref[...]
    o_ref[0] = x2


def _out_mlp(xa, yr, yg, mod, mod_index, layer, n2, wo, w1, w2, fg, tm, final):
    b, seq, d = xa.shape
    tok = lambda n: pl.BlockSpec((1, tm, n), lambda i, j: (i, j, 0))
    return pl.pallas_call(
        functools.partial(_mlp_kernel, final=final, ff_chunk=1024),
        out_shape=jax.ShapeDtypeStruct(xa.shape, F32),
        grid=(b, seq // tm),
        in_specs=[tok(d), tok(RET_W), tok(GLA_V), pl.BlockSpec((1, 6, d), lambda i, j: (mod_index(i), 0, 0)),
                  _layer_spec(n2.shape[1:], layer), _layer_spec(wo.shape[1:], layer),
                  _layer_spec(w1.shape[1:], layer), _layer_spec(w2.shape[1:], layer), _const_spec((1, d))],
        out_specs=tok(d),
        compiler_params=pltpu.CompilerParams(dimension_semantics=("parallel", "parallel"),
                                             vmem_limit_bytes=VMEM_LIMIT),
        name="out_mlp",
    )(xa, yr, yg, mod, n2, wo, w1, w2, fg)


def kernel(x, c, ctx, c_ctx, ada_w, ada_b, norm1_g, w_in, ret_decay, gla_gate_up, gla_gate_b, gla_norm_g,
           w_out, norm2_g, w_mlp1, w_mlp2, final_g):
    batch, seq, d = x.shape
    cl = ctx.shape[1]
    depth = ada_w.shape[0]
    assert w_in.shape[2] == IN_W
    mod_rows = 16
    cs = jnp.concatenate([c, c_ctx[None, :], jnp.zeros((mod_rows - batch - 1, d), c.dtype)], axis=0)
    mod = _modulation(cs, ada_w, ada_b).reshape(depth * mod_rows, 6, d)

    lat_tables = _rope_tables(seq)
    scale = RET_HEAD_DIM ** -0.5
    ones = jnp.ones((cl, RET_HEAD_DIM), F32)
    ctx_tables = (ones * scale, ones * 0.0, ones, ones * 0.0)

    w_in_b = w_in.astype(BF16)
    zero = jnp.zeros((depth, GLA_GATE_RANK, GLA_K), gla_gate_up.dtype)
    up = jnp.concatenate([jnp.concatenate([gla_gate_up[:, 0], zero], axis=2),
                          jnp.concatenate([zero, gla_gate_up[:, 1]], axis=2)], axis=1).astype(BF16)
    ub = gla_gate_b.reshape(depth, 1, 2 * GLA_K)
    wo = w_out.astype(BF16)
    w1 = w_mlp1.astype(BF16)
    w2 = w_mlp2.astype(BF16)
    n1 = norm1_g.reshape(depth, 1, d)
    n2 = norm2_g.reshape(depth, 1, d)
    gn = gla_norm_g.reshape(depth, 1, GLA_VAL_DIM)
    fg = final_g.reshape(1, d)
    rd = jnp.broadcast_to(jnp.swapaxes(ret_decay, 1, 2)[..., None], (depth, N_RET_HEADS, 2, RET_BLOCK))

    tm = math.gcd(seq, ROW_TILE)
    for layer in range(depth):
        last = layer == depth - 1
        lat_mod = lambda i, layer=layer: layer * mod_rows + i
        ctx_mod = lambda i, layer=layer: layer * mod_rows + batch
        lat = _project(x, mod, lat_mod, layer, n1, w_in_b, up, ub, lat_tables, tm)
        cx = _project(ctx, mod, ctx_mod, layer, n1, w_in_b, up, ub, ctx_tables, cl)
        rq, rk, rv, g, gq, gk, gv, ga = lat
        crq, crk, crv, cg, cgq, cgk, cgv, cga = cx
        yr, cyr = _retention(rd, layer, (rq, rk, rv, g), (crq, crk, crv, cg))
        yg, cyg = _gla(gn, layer, (gq, gk, gv, g, ga), (cgq, cgk, cgv, cg, cga))
        x = _out_mlp(x, yr, yg, mod, lat_mod, layer, n2, wo, w1, w2, fg, tm, last)
        if not last:
            ctx = _out_mlp(ctx, cyr, cyg, mod, ctx_mod, layer, n2, wo, w1, w2, fg, cl, False)
    return x
```

```python
import functools
import math

import jax
import jax.numpy as jnp
from jax import lax
from jax.experimental import pallas as pl
from jax.experimental.pallas import tpu as pltpu

F32 = jnp.float32
BF16 = jnp.bfloat16

GRID_W = 64
N_RET_HEADS = 4
RET_HEAD_DIM = 128
N_GLA_HEADS = 4
GLA_KEY_DIM = 64
GLA_VAL_DIM = 128
GLA_GATE_RANK = 16
GLA_GATE_NORM = 16.0
ROPE_BASE = 10000.0
EPS = 1e-6

RET_W = N_RET_HEADS * RET_HEAD_DIM
GLA_K = N_GLA_HEADS * GLA_KEY_DIM
GLA_V = N_GLA_HEADS * GLA_VAL_DIM
MIX_W = RET_W + GLA_V

OFF_RV = 2 * RET_W
OFF_RG = 3 * RET_W
OFF_GQ = 4 * RET_W
OFF_GK = OFF_GQ + GLA_K
OFF_GV = OFF_GK + GLA_K
OFF_GG = OFF_GV + GLA_V
OFF_D = OFF_GG + GLA_V
IN_W = OFF_D + 2 * GLA_GATE_RANK

RET_BLOCK = 256
GLA_BLOCK = 128
GLA_HALF = GLA_BLOCK // 2

ROW_TILE = 1024
LOG2E = 1.4426950408889634

VMEM_LIMIT = 56 * 1024 * 1024


def _layer_spec(shape, layer):
    zeros = (0,) * len(shape)
    return pl.BlockSpec((None,) + tuple(shape), lambda *_: (layer,) + zeros, pipeline_mode=pl.Buffered(1))


def _const_spec(shape):
    zeros = (0,) * len(shape)
    return pl.BlockSpec(shape, lambda *_: zeros, pipeline_mode=pl.Buffered(1))


def _mod_kernel(cs_ref, w_ref, b_ref, o_ref):
    cs = cs_ref[...]
    s = cs * jax.nn.sigmoid(cs)
    o_ref[0] = jnp.dot(s.astype(BF16), w_ref[0].astype(BF16), preferred_element_type=F32) + b_ref[0]


def _modulation(cs, ada_w, ada_b):
    depth, d, n = ada_w.shape
    rows = cs.shape[0]
    tn = 1024
    return pl.pallas_call(
        _mod_kernel,
        out_shape=jax.ShapeDtypeStruct((depth, rows, n), F32),
        grid=(depth, n // tn),
        in_specs=[
            pl.BlockSpec((rows, d), lambda l, j: (0, 0)),
            pl.BlockSpec((1, d, tn), lambda l, j: (l, 0, j)),
            pl.BlockSpec((1, 1, tn), lambda l, j: (l, 0, j)),
        ],
        out_specs=pl.BlockSpec((1, rows, tn), lambda l, j: (l, 0, j)),
        compiler_params=pltpu.CompilerParams(dimension_semantics=("parallel", "parallel"),
                                             vmem_limit_bytes=VMEM_LIMIT),
        name="adaln_modulation",
    )(cs, ada_w, ada_b.reshape(depth, 1, n))


_QUARTER = RET_HEAD_DIM // 4


def _rope_kernel(cq_ref, sq_ref, ck_ref, sk_ref):
    shape = (GRID_W, RET_HEAD_DIM)
    p = lax.broadcasted_iota(jnp.int32, shape, 0).astype(F32)
    lane = lax.broadcasted_iota(jnp.int32, shape, 1)
    freq = (lane & (_QUARTER - 1)).astype(F32)
    ang = p * jnp.exp(freq * (-math.log(ROPE_BASE) / _QUARTER))
    cos = jnp.cos(ang)
    sin = jnp.where((lane & (2 * _QUARTER - 1)) < _QUARTER, -jnp.sin(ang), jnp.sin(ang))
    by_row = lane < 2 * _QUARTER
    scale = RET_HEAD_DIM ** -0.5
    for r in range(cq_ref.shape[0] // GRID_W):
        c_blk = jnp.where(by_row, cos[r:r + 1, :], cos)
        s_blk = jnp.where(by_row, sin[r:r + 1, :], sin)
        rs = slice(r * GRID_W, (r + 1) * GRID_W)
        cq_ref[rs, :] = c_blk * scale
        sq_ref[rs, :] = s_blk * scale
        ck_ref[rs, :] = c_blk
        sk_ref[rs, :] = s_blk


def _rope_tables(seq):
    sds = jax.ShapeDtypeStruct((seq, RET_HEAD_DIM), F32)
    return pl.pallas_call(_rope_kernel, out_shape=(sds, sds, sds, sds), name="rope_tables")()


def _silu(z):
    hz = 0.5 * z
    return hz * (1.0 + jnp.tanh(hz))


def _proj_kernel(x_ref, mod_ref, n1_ref, w_ref, up_ref, ub_ref, cq_ref, sq_ref, ck_ref, sk_ref,
                 rq_ref, rk_ref, rv_ref, g_ref, gq_ref, gk_ref, gv_ref, ga_ref):
    x = x_ref[0]
    mod = mod_ref[0]
    gain = n1_ref[...] * (1.0 + mod[1:2])
    ms = jnp.mean(x * x, axis=-1, keepdims=True)
    hb = (x * lax.rsqrt(ms + EPS) * gain + mod[0:1]).astype(BF16)

    def mm(off, n):
        return jnp.dot(hb, w_ref[:, off:off + n], preferred_element_type=F32)

    lane = lax.broadcasted_iota(jnp.int32, (x.shape[0], RET_HEAD_DIM), 1)
    first = (lane & (2 * _QUARTER - 1)) < _QUARTER

    def rotary(z, c_ref, s_ref, o_ref):
        c = c_ref[...]
        s = s_ref[...]
        for hd in range(N_RET_HEADS):
            sl = slice(hd * RET_HEAD_DIM, (hd + 1) * RET_HEAD_DIM)
            zh = z[:, sl]
            partner = jnp.where(first, pltpu.roll(zh, RET_HEAD_DIM - _QUARTER, axis=1),
                                pltpu.roll(zh, _QUARTER, axis=1))
            o_ref[0, :, sl] = (zh * c + partner * s).astype(BF16)

    d = mm(OFF_D, 2 * GLA_GATE_RANK).astype(BF16)
    zg = jnp.dot(d, up_ref[...], preferred_element_type=F32) + ub_ref[...]
    soft = jnp.log2(1.0 + jnp.exp2(jnp.abs(zg) * -LOG2E))
    ga_ref[0] = jnp.minimum(zg, 0.0) * (LOG2E / GLA_GATE_NORM) - soft * (1.0 / GLA_GATE_NORM)
    rv_ref[0] = mm(OFF_RV, RET_W).astype(BF16)
    rotary(mm(0, RET_W), cq_ref, sq_ref, rq_ref)
    gk_ref[0] = mm(OFF_GK, GLA_K).astype(BF16)
    rotary(mm(RET_W, RET_W), ck_ref, sk_ref, rk_ref)
    gv_ref[0] = mm(OFF_GV, GLA_V).astype(BF16)
    g_ref[0, :, 0:RET_W] = _silu(mm(OFF_RG, RET_W)).astype(BF16)
    gq_ref[0] = (mm(OFF_GQ, GLA_K) * (GLA_KEY_DIM ** -0.5)).astype(BF16)
    g_ref[0, :, RET_W:MIX_W] = _silu(mm(OFF_GG, GLA_V)).astype(BF16)


def _project(xa, mod, mod_index, layer, n1, w, up, ub, tables, tm):
    b, seq, d = xa.shape
    tok = lambda n: pl.BlockSpec((1, tm, n), lambda i, j: (i, j, 0))
    tab = pl.BlockSpec((tm, RET_HEAD_DIM), lambda i, j: (j, 0))
    sd = lambda n, dt: jax.ShapeDtypeStruct((b, seq, n), dt)
    return pl.pallas_call(
        _proj_kernel,
        out_shape=(sd(RET_W, BF16), sd(RET_W, BF16), sd(RET_W, BF16), sd(MIX_W, BF16),
                   sd(GLA_K, BF16), sd(GLA_K, BF16), sd(GLA_V, BF16), sd(2 * GLA_K, F32)),
        grid=(b, seq // tm),
        in_specs=[tok(d), pl.BlockSpec((1, 6, d), lambda i, j: (mod_index(i), 0, 0)),
                  _layer_spec(n1.shape[1:], layer),
                  _layer_spec(w.shape[1:], layer), _layer_spec(up.shape[1:], layer),
                  _layer_spec(ub.shape[1:], layer), tab, tab, tab, tab],
        out_specs=(tok(RET_W), tok(RET_W), tok(RET_W), tok(MIX_W), tok(GLA_K), tok(GLA_K), tok(GLA_V),
                   tok(2 * GLA_K)),
        compiler_params=pltpu.CompilerParams(dimension_semantics=("parallel", "parallel"),
                                             vmem_limit_bytes=VMEM_LIMIT),
        name="norm_project",
    )(xa, mod, n1, w, up, ub, *tables)


def _ret_kernel(rd_ref, q_ref, k_ref, v_ref, g_ref, cq_ref, ck_ref, cv_ref, cg_ref,
                y_ref, cy_ref, kv_ref, st_ref, sc_ref, *, need_ctx):
    t_blk = RET_BLOCK
    dh = RET_HEAD_DIM
    lg = jnp.log1p(-jnp.exp(rd_ref[0]))
    lgf, lgb = lg[0:1, :], lg[1:2, :]
    lgf_h, lgb_h = lgf[:, :dh], lgb[:, :dh]
    ti = lax.broadcasted_iota(jnp.int32, (t_blk, t_blk), 0)
    si = lax.broadcasted_iota(jnp.int32, (t_blk, t_blk), 1)
    diff = (ti - si).astype(F32)
    mask = jnp.exp(jnp.where(diff >= 0, diff * lgf, -diff * lgb))
    tr = lax.broadcasted_iota(jnp.int32, (t_blk, dh), 0).astype(F32)
    q_dec = jnp.concatenate([jnp.exp((tr + 1.0) * lgf_h), jnp.exp((t_blk - tr) * lgb_h)], axis=1)
    k_dec_f = jnp.exp((t_blk - 1.0 - tr) * lgf_h)
    k_dec_b = jnp.exp(tr * lgb_h)
    blk_f = jnp.exp(t_blk * lgf_h)
    blk_b = jnp.exp(t_blk * lgb_h)
    tn_dims = (((0,), (0,)), ((), ()))
    nt_dims = (((1,), (1,)), ((), ()))
    n_ctx = cq_ref.shape[1] // t_blk
    n_lat = q_ref.shape[1] // t_blk
    n_tot = n_ctx + n_lat
    fwd = slice(0, dh)
    bwd = slice(dh, 2 * dh)

    def rows(i):
        return pl.ds(i * t_blk, t_blk)

    def kv_block(kr, vr, i, j):
        r = rows(i)
        v = vr[0, r, :].astype(F32)
        vv = jnp.concatenate([(v * k_dec_f).astype(BF16), (v * k_dec_b).astype(BF16)], axis=1)
        kv_ref[j] = lax.dot_general(kr[0, r, :], vv, tn_dims, preferred_element_type=F32)

    def score_block(qr, kr, i, j):
        r = rows(i)
        a = lax.dot_general(qr[0, r, :], kr[0, r, :], nt_dims, preferred_element_type=F32) * mask
        sc_ref[j] = a.astype(BF16)

    def out_block(qr, vr, gr, yr, i, j):
        r = rows(i)
        q, v = qr[0, r, :], vr[0, r, :]
        o = jnp.dot(sc_ref[j], v, preferred_element_type=F32)
        inter = jnp.dot(q, st_ref[j], preferred_element_type=F32) * q_dec
        o = o + inter[:, fwd] + inter[:, bwd]
        mu = jnp.mean(o, axis=-1, keepdims=True)
        dlt = o - mu
        var = jnp.mean(dlt * dlt, axis=-1, keepdims=True)
        yr[0, r, :] = (dlt * lax.rsqrt(var + EPS) * gr[0, r, :].astype(F32)).astype(BF16)

    def for_blocks(n, fn):
        for i in range(n):
            fn(i)

    for_blocks(n_ctx, lambda i: kv_block(ck_ref, cv_ref, i, i))
    for_blocks(n_lat, lambda i: kv_block(k_ref, v_ref, i, n_ctx + i))

    def step_f(j, s):
        st_ref[j, :, fwd] = s.astype(BF16)
        return blk_f * s + kv_ref[j, :, fwd]

    def step_b(j, s):
        st_ref[j, :, bwd] = s.astype(BF16)
        return blk_b * s + kv_ref[j, :, bwd]

    zero = jnp.zeros((dh, dh), F32)
    lax.fori_loop(0, n_tot, step_f, zero)
    s_b = lax.fori_loop(0, n_ctx, lambda t, s: step_b(n_ctx - 1 - t, s), zero)
    lax.fori_loop(0, n_lat, lambda t, s: step_b(n_tot - 1 - t, s), s_b)

    if need_ctx:
        for_blocks(n_ctx, lambda i: score_block(cq_ref, ck_ref, i, i))
    for_blocks(n_lat, lambda i: score_block(q_ref, k_ref, i, n_ctx + i))
    if need_ctx:
        for_blocks(n_ctx, lambda i: out_block(cq_ref, cv_ref, cg_ref, cy_ref, i, i))
    else:
        cy_ref[...] = jnp.zeros(cy_ref.shape, cy_ref.dtype)
    for_blocks(n_lat, lambda i: out_block(q_ref, v_ref, g_ref, y_ref, i, n_ctx + i))


def _retention(rd, layer, lat, ctx, need_ctx):
    rq, rk, rv, g = lat
    crq, crk, crv, cg = ctx
    b, seq, _ = rq.shape
    cl = crq.shape[1]
    dh = RET_HEAD_DIM
    n_blocks = (seq + cl) // RET_BLOCK
    head = lambda n: pl.BlockSpec((1, n, dh), lambda i, h: (i, 0, h))
    return pl.pallas_call(
        functools.partial(_ret_kernel, need_ctx=need_ctx),
        out_shape=(jax.ShapeDtypeStruct((b, seq, RET_W), BF16), jax.ShapeDtypeStruct((b, cl, RET_W), BF16)),
        grid=(b, N_RET_HEADS),
        in_specs=[pl.BlockSpec((None, 1, 2, RET_BLOCK), lambda i, h: (layer, h, 0, 0)),
                  head(seq), head(seq), head(seq), head(seq), head(cl), head(cl), head(cl), head(cl)],
        out_specs=(head(seq), head(cl)),
        scratch_shapes=[pltpu.VMEM((n_blocks, dh, 2 * dh), F32), pltpu.VMEM((n_blocks, dh, 2 * dh), BF16),
                        pltpu.VMEM((n_blocks, RET_BLOCK, RET_BLOCK), BF16)],
        compiler_params=pltpu.CompilerParams(dimension_semantics=("parallel", "parallel"),
                                             vmem_limit_bytes=VMEM_LIMIT),
        name="retention_scan",
    )(rd, rq, rk, rv, g, crq, crk, crv, cg)


def _gla_kernel(gn_ref, q_ref, k_ref, v_ref, g_ref, af_ref, ab_ref, cq_ref, ck_ref, cv_ref, cg_ref,
                caf_ref, cab_ref, y_ref, cy_ref, c_ref, kv_ref, dec_ref, st_ref, sc_ref, qin_ref, *, need_ctx):
    t_blk = GLA_BLOCK
    dv = GLA_VAL_DIM
    dk2 = 2 * GLA_KEY_DIM
    assert t_blk == dk2 == dv
    ti = lax.broadcasted_iota(jnp.int32, (t_blk, t_blk), 0)
    si = lax.broadcasted_iota(jnp.int32, (t_blk, t_blk), 1)
    lower = si <= ti
    cum_f = lower.astype(BF16)
    head0 = si < GLA_KEY_DIM
    tn_dims = (((0,), (0,)), ((), ()))
    nt_dims = (((1,), (1,)), ((), ()))

    def rows(i):
        return pl.ds(i * t_blk, t_blk)

    def split2(a):
        hi = a.astype(BF16)
        return [hi, (a - hi.astype(F32)).astype(BF16)]

    dirs = ((GLA_HALF - 1, t_blk - 1), (GLA_HALF, 0))

    def cum_block(ars, i, row0):
        af = ars[0][0, rows(i), :]
        ab = ars[1][0, rows(i), :]
        z = jnp.dot(cum_f, jnp.concatenate(split2(af) + split2(ab), axis=1), preferred_element_type=F32)
        c_ref[0, rows(row0 + i), :] = z[:, 0:dk2] + z[:, dk2:2 * dk2]
        pb = z[:, 2 * dk2:3 * dk2] + z[:, 3 * dk2:4 * dk2]
        c_ref[1, rows(row0 + i), :] = pb[t_blk - 1:t_blk, :] - pb + ab

    def mid_block(qr, kr, vr, i, j, row0, outputs=True):
        r = rows(i)
        cr = rows(row0 + i)
        q = qr[0, r, :].astype(F32)
        k = kr[0, r, :].astype(F32)
        v = vr[0, r, :]
        sc, k_end = [], []
        for d, (mid, edge) in enumerate(dirs):
            c = c_ref[d, cr, :]
            c_mid = c[mid:mid + 1, :]
            c_edge = c[edge:edge + 1, :]
            k_mid = k * jnp.exp2(c_mid - c)
            k_end.append((k_mid * jnp.exp2(c_edge - c_mid)).astype(BF16))
            dec_ref[d, j] = jnp.broadcast_to(jnp.exp2(c_edge), (8, dk2))
            if not outputs:
                continue
            q_mid = q * jnp.exp2(c - c_mid)
            qin_ref[d, cr, :] = (q_mid * jnp.exp2(c_mid)).astype(BF16)
            k_mid_b = k_mid.astype(BF16)
            zero = jnp.zeros_like(k_mid_b)
            k_heads = jnp.concatenate([jnp.where(head0, k_mid_b, zero), jnp.where(head0, zero, k_mid_b)], axis=0)
            sc.append(lax.dot_general(q_mid.astype(BF16), k_heads, nt_dims, preferred_element_type=F32))
        kv = lax.dot_general(v, jnp.concatenate(k_end, axis=1), tn_dims, preferred_element_type=F32)
        for d in range(2):
            kvd = kv[:, d * dk2:(d + 1) * dk2]
            kv_ref[d, j] = jnp.where(head0, kvd[0:dv, :], kvd[dv:2 * dv, :])
        for hd in range(2 if outputs else 0):
            hs = slice(hd * t_blk, (hd + 1) * t_blk)
            sc_ref[hd, j] = jnp.where(lower, sc[0][:, hs], sc[1][:, hs]).astype(BF16)

    def out_block(qr, vr, gr, yr, i, j, row0):
        r = rows(i)
        cr = rows(row0 + i)
        v = vr[0, r, :]
        s_exp = []
        for d in range(2):
            s = st_ref[d, j]
            zero = jnp.zeros_like(s)
            s_exp.append(jnp.concatenate([jnp.where(head0, s, zero), jnp.where(head0, zero, s)], axis=0))
        q_in = jnp.concatenate([qin_ref[0, cr, :], qin_ref[1, cr, :]], axis=1)
        o = jnp.concatenate([jnp.dot(sc_ref[hd, j], v[:, hd * dv:(hd + 1) * dv], preferred_element_type=F32)
                             for hd in range(2)], axis=1)
        o = o + lax.dot_general(q_in, jnp.concatenate(s_exp, axis=1), nt_dims, preferred_element_type=F32)
        gate = gr[0, r, :].astype(F32)
        gn = gn_ref[...]
        for hd in range(2):
            sl = slice(hd * dv, (hd + 1) * dv)
            oh = o[:, sl]
            ms = jnp.mean(oh * oh, axis=-1, keepdims=True)
            yr[0, r, sl] = (oh * lax.rsqrt(ms + EPS) * gn * gate[:, sl]).astype(BF16)

    def for_blocks(n, fn):
        for i in range(n):
            fn(i)

    n_ctx = cq_ref.shape[1] // t_blk
    n_lat = q_ref.shape[1] // t_blk
    n_tot = n_ctx + n_lat
    for_blocks(n_ctx, lambda i: cum_block((caf_ref, cab_ref), i, 0))
    for_blocks(n_lat, lambda i: cum_block((af_ref, ab_ref), i, n_ctx))
    for_blocks(n_ctx, lambda i: mid_block(cq_ref, ck_ref, cv_ref, i, i, 0, outputs=need_ctx))
    for_blocks(n_lat, lambda i: mid_block(q_ref, k_ref, v_ref, i, n_ctx + i, n_ctx))

    def step(d, j, s):
        st_ref[d, j] = s.astype(BF16)
        return dec_ref[d, j, 0:1, :] * s + kv_ref[d, j]

    zero = jnp.zeros((dv, dk2), F32)
    lax.fori_loop(0, n_tot, lambda j, s: step(0, j, s), zero)
    s_b = lax.fori_loop(0, n_ctx, lambda t, s: step(1, n_ctx - 1 - t, s), zero)
    lax.fori_loop(0, n_lat, lambda t, s: step(1, n_tot - 1 - t, s), s_b)

    if need_ctx:
        for_blocks(n_ctx, lambda i: out_block(cq_ref, cv_ref, cg_ref, cy_ref, i, i, 0))
    else:
        cy_ref[...] = jnp.zeros(cy_ref.shape, cy_ref.dtype)
    for_blocks(n_lat, lambda i: out_block(q_ref, v_ref, g_ref, y_ref, i, n_ctx + i, n_ctx))


def _gla(gn, layer, lat, ctx, need_ctx):
    gq, gk, gv, g, ga = lat
    cgq, cgk, cgv, cg, cga = ctx
    b, seq, _ = gq.shape
    cl = cgq.shape[1]
    pair_k = 2 * GLA_KEY_DIM
    pair_v = 2 * GLA_VAL_DIM
    n_pairs = N_GLA_HEADS // 2
    n_blocks = (seq + cl) // GLA_BLOCK
    gate_off = RET_W // pair_v
    key = lambda n: pl.BlockSpec((1, n, pair_k), lambda i, p: (i, 0, p))
    key_b = lambda n: pl.BlockSpec((1, n, pair_k), lambda i, p: (i, 0, n_pairs + p))
    val = lambda n: pl.BlockSpec((1, n, pair_v), lambda i, p: (i, 0, p))
    mix = lambda n: pl.BlockSpec((1, n, pair_v), lambda i, p: (i, 0, gate_off + p))
    return pl.pallas_call(
        functools.partial(_gla_kernel, need_ctx=need_ctx),
        out_shape=(jax.ShapeDtypeStruct((b, seq, GLA_V), BF16), jax.ShapeDtypeStruct((b, cl, GLA_V), BF16)),
        grid=(b, n_pairs),
        in_specs=[_layer_spec(gn.shape[1:], layer),
                  key(seq), key(seq), val(seq), mix(seq), key(seq), key_b(seq),
                  key(cl), key(cl), val(cl), mix(cl), key(cl), key_b(cl)],
        out_specs=(val(seq), val(cl)),
        scratch_shapes=[pltpu.VMEM((2, seq + cl, pair_k), F32),
                        pltpu.VMEM((2, n_blocks, GLA_VAL_DIM, pair_k), F32),
                        pltpu.VMEM((2, n_blocks, 8, pair_k), F32),
                        pltpu.VMEM((2, n_blocks, GLA_VAL_DIM, pair_k), BF16),
                        pltpu.VMEM((2, n_blocks, GLA_BLOCK, GLA_BLOCK), BF16),
                        pltpu.VMEM((2, seq + cl, pair_k), BF16)],
        compiler_params=pltpu.CompilerParams(dimension_semantics=("parallel", "parallel"),
                                             vmem_limit_bytes=VMEM_LIMIT),
        name="gla_scan",
    )(gn, gq, gk, gv, g, ga, ga, cgq, cgk, cgv, cg, cga, cga)


def _mlp_kernel(x_ref, yr_ref, yg_ref, mod_ref, n2_ref, wo_ref, w1_ref, w2_ref, fg_ref, o_ref, *,
                final, ff_chunk):
    mod = mod_ref[0]
    mix = (jnp.dot(yr_ref[0], wo_ref[0:RET_W, :], preferred_element_type=F32)
           + jnp.dot(yg_ref[0], wo_ref[RET_W:MIX_W, :], preferred_element_type=F32))
    x1 = x_ref[0] + mod[2:3] * mix
    gain = n2_ref[...] * (1.0 + mod[4:5])
    ms = jnp.mean(x1 * x1, axis=-1, keepdims=True)
    hb = (x1 * lax.rsqrt(ms + EPS) * gain + mod[3:4]).astype(BF16)
    acc = jnp.zeros(x1.shape, F32)
    for c in range(w1_ref.shape[1] // ff_chunk):
        sl = slice(c * ff_chunk, (c + 1) * ff_chunk)
        a = jnp.maximum(jnp.dot(hb, w1_ref[:, sl], preferred_element_type=F32), 0.0)
        acc = acc + jnp.dot((a * a).astype(BF16), w2_ref[sl, :], preferred_element_type=F32)
    x2 = x1 + mod[5:6] * acc
    if final:
        ms = jnp.mean(x2 * x2, axis=-1, keepdims=True)
        x2 = x2 * lax.rsqrt(ms + EPS) * fg_ref[...]
    o_ref[0] = x2


def _out_mlp(xa, yr, yg, mod, mod_index, layer, n2, wo, w1, w2, fg, tm, final):
    b, seq, d = xa.shape
    tok = lambda n: pl.BlockSpec((1, tm, n), lambda i, j: (i, j, 0))
    return pl.pallas_call(
        functools.partial(_mlp_kernel, final=final, ff_chunk=1024),
        out_shape=jax.ShapeDtypeStruct(xa.shape, F32),
        grid=(b, seq // tm),
        in_specs=[tok(d), tok(RET_W), tok(GLA_V), pl.BlockSpec((1, 6, d), lambda i, j: (mod_index(i), 0, 0)),
                  _layer_spec(n2.shape[1:], layer), _layer_spec(wo.shape[1:], layer),
                  _layer_spec(w1.shape[1:], layer), _layer_spec(w2.shape[1:], layer), _const_spec((1, d))],
        out_specs=tok(d),
        compiler_params=pltpu.CompilerParams(dimension_semantics=("parallel", "parallel"),
                                             vmem_limit_bytes=VMEM_LIMIT),
        name="out_mlp",
    )(xa, yr, yg, mod, n2, wo, w1, w2, fg)


def kernel(x, c, ctx, c_ctx, ada_w, ada_b, norm1_g, w_in, ret_decay, gla_gate_up, gla_gate_b, gla_norm_g,
           w_out, norm2_g, w_mlp1, w_mlp2, final_g):
    batch, seq, d = x.shape
    cl = ctx.shape[1]
    depth = ada_w.shape[0]
    assert w_in.shape[2] == IN_W
    mod_rows = 16
    cs = jnp.concatenate([c, c_ctx[None, :], jnp.zeros((mod_rows - batch - 1, d), c.dtype)], axis=0)
    mod = _modulation(cs, ada_w, ada_b).reshape(depth * mod_rows, 6, d)

    lat_tables = _rope_tables(seq)
    scale = RET_HEAD_DIM ** -0.5
    ones = jnp.ones((cl, RET_HEAD_DIM), F32)
    ctx_tables = (ones * scale, ones * 0.0, ones, ones * 0.0)

    zero = jnp.zeros((depth, GLA_GATE_RANK, GLA_K), gla_gate_up.dtype)
    up = jnp.concatenate([jnp.concatenate([gla_gate_up[:, 0], zero], axis=2),
                          jnp.concatenate([zero, gla_gate_up[:, 1]], axis=2)], axis=1).astype(BF16)
    ub = gla_gate_b.reshape(depth, 1, 2 * GLA_K)
    wo = w_out
    w1 = w_mlp1.astype(BF16)
    w2 = w_mlp2.astype(BF16)
    n1 = norm1_g.reshape(depth, 1, d)
    n2 = norm2_g.reshape(depth, 1, d)
    gn = gla_norm_g.reshape(depth, 1, GLA_VAL_DIM)
    fg = final_g.reshape(1, d)
    rd = jnp.broadcast_to(jnp.swapaxes(ret_decay, 1, 2)[..., None], (depth, N_RET_HEADS, 2, RET_BLOCK))

    tm = math.gcd(seq, ROW_TILE)
    for layer in range(depth):
        last = layer == depth - 1
        lat_mod = lambda i, layer=layer: layer * mod_rows + i
        ctx_mod = lambda i, layer=layer: layer * mod_rows + batch
        lat = _project(x, mod, lat_mod, layer, n1, w_in, up, ub, lat_tables, tm)
        cx = _project(ctx, mod, ctx_mod, layer, n1, w_in, up, ub, ctx_tables, cl)
        rq, rk, rv, g, gq, gk, gv, ga = lat
        crq, crk, crv, cg, cgq, cgk, cgv, cga = cx
        yr, cyr = _retention(rd, layer, (rq, rk, rv, g), (crq, crk, crv, cg), not last)
        yg, cyg = _gla(gn, layer, (gq, gk, gv, g, ga), (cgq, cgk, cgv, cg, cga), not last)
        x = _out_mlp(x, yr, yg, mod, lat_mod, layer, n2, wo, w1, w2, fg, tm, last)
        if not last:
            ctx = _out_mlp(ctx, cyr, cyg, mod, ctx_mod, layer, n2, wo, w1, w2, fg, cl, False)
    return x
```

```python
import functools
import math

import jax
import jax.numpy as jnp
from jax import lax
from jax.experimental import pallas as pl
from jax.experimental.pallas import tpu as pltpu

F32 = jnp.float32
BF16 = jnp.bfloat16

GRID_W = 64
N_RET_HEADS = 4
RET_HEAD_DIM = 128
N_GLA_HEADS = 4
GLA_KEY_DIM = 64
GLA_VAL_DIM = 128
GLA_GATE_RANK = 16
GLA_GATE_NORM = 16.0
ROPE_BASE = 10000.0
EPS = 1e-6

RET_W = N_RET_HEADS * RET_HEAD_DIM
GLA_K = N_GLA_HEADS * GLA_KEY_DIM
GLA_V = N_GLA_HEADS * GLA_VAL_DIM
MIX_W = RET_W + GLA_V

OFF_RV = 2 * RET_W
OFF_RG = 3 * RET_W
OFF_GQ = 4 * RET_W
OFF_GK = OFF_GQ + GLA_K
OFF_GV = OFF_GK + GLA_K
OFF_GG = OFF_GV + GLA_V
OFF_D = OFF_GG + GLA_V
IN_W = OFF_D + 2 * GLA_GATE_RANK

RET_BLOCK = 256
GLA_BLOCK = 128
GLA_HALF = GLA_BLOCK // 2

ROW_TILE = 1024
LOG2E = 1.4426950408889634

VMEM_LIMIT = 56 * 1024 * 1024


def _layer_spec(shape, layer):
    zeros = (0,) * len(shape)
    return pl.BlockSpec((None,) + tuple(shape), lambda *_: (layer,) + zeros, pipeline_mode=pl.Buffered(1))


def _const_spec(shape):
    zeros = (0,) * len(shape)
    return pl.BlockSpec(shape, lambda *_: zeros, pipeline_mode=pl.Buffered(1))


def _mod_kernel(cs_ref, w_ref, b_ref, o_ref):
    cs = cs_ref[...]
    s = cs * jax.nn.sigmoid(cs)
    o_ref[0] = jnp.dot(s.astype(BF16), w_ref[0].astype(BF16), preferred_element_type=F32) + b_ref[0]


def _modulation(cs, ada_w, ada_b):
    depth, d, n = ada_w.shape
    rows = cs.shape[0]
    tn = 1024
    return pl.pallas_call(
        _mod_kernel,
        out_shape=jax.ShapeDtypeStruct((depth, rows, n), F32),
        grid=(depth, n // tn),
        in_specs=[
            pl.BlockSpec((rows, d), lambda l, j: (0, 0)),
            pl.BlockSpec((1, d, tn), lambda l, j: (l, 0, j)),
            pl.BlockSpec((1, 1, tn), lambda l, j: (l, 0, j)),
        ],
        out_specs=pl.BlockSpec((1, rows, tn), lambda l, j: (l, 0, j)),
        compiler_params=pltpu.CompilerParams(dimension_semantics=("parallel", "parallel"),
                                             vmem_limit_bytes=VMEM_LIMIT),
        name="adaln_modulation",
    )(cs, ada_w, ada_b.reshape(depth, 1, n))


_QUARTER = RET_HEAD_DIM // 4


def _rope_kernel(cq_ref, sq_ref, ck_ref, sk_ref):
    shape = (GRID_W, RET_HEAD_DIM)
    p = lax.broadcasted_iota(jnp.int32, shape, 0).astype(F32)
    lane = lax.broadcasted_iota(jnp.int32, shape, 1)
    freq = (lane & (_QUARTER - 1)).astype(F32)
    ang = p * jnp.exp(freq * (-math.log(ROPE_BASE) / _QUARTER))
    cos = jnp.cos(ang)
    sin = jnp.where((lane & (2 * _QUARTER - 1)) < _QUARTER, -jnp.sin(ang), jnp.sin(ang))
    by_row = lane < 2 * _QUARTER
    scale = RET_HEAD_DIM ** -0.5
    for r in range(cq_ref.shape[0] // GRID_W):
        c_blk = jnp.where(by_row, cos[r:r + 1, :], cos)
        s_blk = jnp.where(by_row, sin[r:r + 1, :], sin)
        rs = slice(r * GRID_W, (r + 1) * GRID_W)
        cq_ref[rs, :] = c_blk * scale
        sq_ref[rs, :] = s_blk * scale
        ck_ref[rs, :] = c_blk
        sk_ref[rs, :] = s_blk


def _rope_tables(seq):
    sds = jax.ShapeDtypeStruct((seq, RET_HEAD_DIM), F32)
    return pl.pallas_call(_rope_kernel, out_shape=(sds, sds, sds, sds), name="rope_tables")()


def _silu(z):
    hz = 0.5 * z
    return hz * (1.0 + jnp.tanh(hz))


def _proj_kernel(x_ref, mod_ref, n1_ref, w_ref, up_ref, ub_ref, cq_ref, sq_ref, ck_ref, sk_ref,
                 rq_ref, rk_ref, rv_ref, g_ref, gq_ref, gk_ref, gv_ref, ga_ref):
    x = x_ref[0]
    mod = mod_ref[0]
    gain = n1_ref[...] * (1.0 + mod[1:2])
    ms = jnp.mean(x * x, axis=-1, keepdims=True)
    hb = (x * lax.rsqrt(ms + EPS) * gain + mod[0:1]).astype(BF16)

    def mm(off, n):
        return jnp.dot(hb, w_ref[:, off:off + n], preferred_element_type=F32)

    lane = lax.broadcasted_iota(jnp.int32, (x.shape[0], RET_HEAD_DIM), 1)
    first = (lane & (2 * _QUARTER - 1)) < _QUARTER

    def rotary(z, c_ref, s_ref, o_ref):
        c = c_ref[...]
        s = s_ref[...]
        for hd in range(N_RET_HEADS):
            sl = slice(hd * RET_HEAD_DIM, (hd + 1) * RET_HEAD_DIM)
            zh = z[:, sl]
            partner = jnp.where(first, pltpu.roll(zh, RET_HEAD_DIM - _QUARTER, axis=1),
                                pltpu.roll(zh, _QUARTER, axis=1))
            o_ref[0, :, sl] = (zh * c + partner * s).astype(BF16)

    d = mm(OFF_D, 2 * GLA_GATE_RANK).astype(BF16)
    zg = jnp.dot(d, up_ref[...], preferred_element_type=F32) + ub_ref[...]
    soft = jnp.log2(1.0 + jnp.exp2(jnp.abs(zg) * -LOG2E))
    ga_ref[0] = jnp.minimum(zg, 0.0) * (LOG2E / GLA_GATE_NORM) - soft * (1.0 / GLA_GATE_NORM)
    rv_ref[0] = mm(OFF_RV, RET_W).astype(BF16)
    rotary(mm(0, RET_W), cq_ref, sq_ref, rq_ref)
    gk_ref[0] = mm(OFF_GK, GLA_K).astype(BF16)
    rotary(mm(RET_W, RET_W), ck_ref, sk_ref, rk_ref)
    gv_ref[0] = mm(OFF_GV, GLA_V).astype(BF16)
    g_ref[0, :, 0:RET_W] = _silu(mm(OFF_RG, RET_W)).astype(BF16)
    gq_ref[0] = (mm(OFF_GQ, GLA_K) * (GLA_KEY_DIM ** -0.5)).astype(BF16)
    g_ref[0, :, RET_W:MIX_W] = _silu(mm(OFF_GG, GLA_V)).astype(BF16)


def _project(xa, mod, mod_index, layer, n1, w, up, ub, tables, tm):
    b, seq, d = xa.shape
    tok = lambda n: pl.BlockSpec((1, tm, n), lambda i, j: (i, j, 0))
    tab = pl.BlockSpec((tm, RET_HEAD_DIM), lambda i, j: (j, 0))
    sd = lambda n, dt: jax.ShapeDtypeStruct((b, seq, n), dt)
    return pl.pallas_call(
        _proj_kernel,
        out_shape=(sd(RET_W, BF16), sd(RET_W, BF16), sd(RET_W, BF16), sd(MIX_W, BF16),
                   sd(GLA_K, BF16), sd(GLA_K, BF16), sd(GLA_V, BF16), sd(2 * GLA_K, F32)),
        grid=(b, seq // tm),
        in_specs=[tok(d), pl.BlockSpec((1, 6, d), lambda i, j: (mod_index(i), 0, 0)),
                  _layer_spec(n1.shape[1:], layer),
                  _layer_spec(w.shape[1:], layer), _layer_spec(up.shape[1:], layer),
                  _layer_spec(ub.shape[1:], layer), tab, tab, tab, tab],
        out_specs=(tok(RET_W), tok(RET_W), tok(RET_W), tok(MIX_W), tok(GLA_K), tok(GLA_K), tok(GLA_V),
                   tok(2 * GLA_K)),
        compiler_params=pltpu.CompilerParams(dimension_semantics=("parallel", "parallel"),
                                             vmem_limit_bytes=VMEM_LIMIT),
        name="norm_project",
    )(xa, mod, n1, w, up, ub, *tables)


def _ret_kernel(rd_ref, q_ref, k_ref, v_ref, g_ref, cq_ref, ck_ref, cv_ref, cg_ref,
                y_ref, cy_ref, kv_ref, st_ref, sc_ref, *, need_ctx):
    t_blk = RET_BLOCK
    dh = RET_HEAD_DIM
    lg = jnp.log1p(-jnp.exp(rd_ref[0]))
    lgf, lgb = lg[0:1, :], lg[1:2, :]
    lgf_h, lgb_h = lgf[:, :dh], lgb[:, :dh]
    ti = lax.broadcasted_iota(jnp.int32, (t_blk, t_blk), 0)
    si = lax.broadcasted_iota(jnp.int32, (t_blk, t_blk), 1)
    diff = (ti - si).astype(F32)
    mask = jnp.exp(jnp.where(diff >= 0, diff * lgf, -diff * lgb))
    tr = lax.broadcasted_iota(jnp.int32, (t_blk, dh), 0).astype(F32)
    q_dec = jnp.concatenate([jnp.exp((tr + 1.0) * lgf_h), jnp.exp((t_blk - tr) * lgb_h)], axis=1)
    k_dec_f = jnp.exp((t_blk - 1.0 - tr) * lgf_h)
    k_dec_b = jnp.exp(tr * lgb_h)
    blk_f = jnp.exp(t_blk * lgf_h)
    blk_b = jnp.exp(t_blk * lgb_h)
    tn_dims = (((0,), (0,)), ((), ()))
    nt_dims = (((1,), (1,)), ((), ()))
    n_ctx = cq_ref.shape[1] // t_blk
    n_lat = q_ref.shape[1] // t_blk
    n_tot = n_ctx + n_lat
    fwd = slice(0, dh)
    bwd = slice(dh, 2 * dh)

    def rows(i):
        return pl.ds(i * t_blk, t_blk)

    def kv_block(kr, vr, i, j):
        r = rows(i)
        v = vr[0, r, :].astype(F32)
        vv = jnp.concatenate([(v * k_dec_f).astype(BF16), (v * k_dec_b).astype(BF16)], axis=1)
        kv_ref[j] = lax.dot_general(kr[0, r, :], vv, tn_dims, preferred_element_type=F32)

    def score_block(qr, kr, i, j):
        r = rows(i)
        a = lax.dot_general(qr[0, r, :], kr[0, r, :], nt_dims, preferred_element_type=F32) * mask
        sc_ref[j] = a.astype(BF16)

    def out_block(qr, vr, gr, yr, i, j):
        r = rows(i)
        q, v = qr[0, r, :], vr[0, r, :]
        o = jnp.dot(sc_ref[j], v, preferred_element_type=F32)
        inter = jnp.dot(q, st_ref[j], preferred_element_type=F32) * q_dec
        o = o + inter[:, fwd] + inter[:, bwd]
        mu = jnp.mean(o, axis=-1, keepdims=True)
        dlt = o - mu
        var = jnp.mean(dlt * dlt, axis=-1, keepdims=True)
        yr[0, r, :] = (dlt * lax.rsqrt(var + EPS) * gr[0, r, :].astype(F32)).astype(BF16)

    def for_blocks(n, fn):
        for i in range(n):
            fn(i)

    for_blocks(n_ctx, lambda i: kv_block(ck_ref, cv_ref, i, i))
    for_blocks(n_lat, lambda i: kv_block(k_ref, v_ref, i, n_ctx + i))

    def step_f(j, s):
        st_ref[j, :, fwd] = s.astype(BF16)
        return blk_f * s + kv_ref[j, :, fwd]

    def step_b(j, s):
        st_ref[j, :, bwd] = s.astype(BF16)
        return blk_b * s + kv_ref[j, :, bwd]

    zero = jnp.zeros((dh, dh), F32)
    lax.fori_loop(0, n_tot, step_f, zero)
    s_b = lax.fori_loop(0, n_ctx, lambda t, s: step_b(n_ctx - 1 - t, s), zero)
    lax.fori_loop(0, n_lat, lambda t, s: step_b(n_tot - 1 - t, s), s_b)

    if need_ctx:
        for_blocks(n_ctx, lambda i: score_block(cq_ref, ck_ref, i, i))
    for_blocks(n_lat, lambda i: score_block(q_ref, k_ref, i, n_ctx + i))
    if need_ctx:
        for_blocks(n_ctx, lambda i: out_block(cq_ref, cv_ref, cg_ref, cy_ref, i, i))
    else:
        cy_ref[...] = jnp.zeros(cy_ref.shape, cy_ref.dtype)
    for_blocks(n_lat, lambda i: out_block(q_ref, v_ref, g_ref, y_ref, i, n_ctx + i))


def _retention(rd, layer, lat, ctx, need_ctx):
    rq, rk, rv, g = lat
    crq, crk, crv, cg = ctx
    b, seq, _ = rq.shape
    cl = crq.shape[1]
    dh = RET_HEAD_DIM
    n_blocks = (seq + cl) // RET_BLOCK
    head = lambda n: pl.BlockSpec((1, n, dh), lambda i, h: (i, 0, h))
    return pl.pallas_call(
        functools.partial(_ret_kernel, need_ctx=need_ctx),
        out_shape=(jax.ShapeDtypeStruct((b, seq, RET_W), BF16), jax.ShapeDtypeStruct((b, cl, RET_W), BF16)),
        grid=(b, N_RET_HEADS),
        in_specs=[pl.BlockSpec((None, 1, 2, RET_BLOCK), lambda i, h: (layer, h, 0, 0)),
                  head(seq), head(seq), head(seq), head(seq), head(cl), head(cl), head(cl), head(cl)],
        out_specs=(head(seq), head(cl)),
        scratch_shapes=[pltpu.VMEM((n_blocks, dh, 2 * dh), F32), pltpu.VMEM((n_blocks, dh, 2 * dh), BF16),
                        pltpu.VMEM((n_blocks, RET_BLOCK, RET_BLOCK), BF16)],
        compiler_params=pltpu.CompilerParams(dimension_semantics=("parallel", "parallel"),
                                             vmem_limit_bytes=VMEM_LIMIT),
        name="retention_scan",
    )(rd, rq, rk, rv, g, crq, crk, crv, cg)


def _gla_kernel(gn_ref, q_ref, k_ref, v_ref, g_ref, af_ref, ab_ref, cq_ref, ck_ref, cv_ref, cg_ref,
                caf_ref, cab_ref, y_ref, cy_ref, c_ref, kv_ref, dec_ref, st_ref, sc_ref, qin_ref, *, need_ctx):
    t_blk = GLA_BLOCK
    dv = GLA_VAL_DIM
    dk2 = 2 * GLA_KEY_DIM
    assert t_blk == dk2 == dv
    ti = lax.broadcasted_iota(jnp.int32, (t_blk, t_blk), 0)
    si = lax.broadcasted_iota(jnp.int32, (t_blk, t_blk), 1)
    lower = si <= ti
    cum_f = lower.astype(BF16)
    head0 = si < GLA_KEY_DIM
    tn_dims = (((0,), (0,)), ((), ()))
    nt_dims = (((1,), (1,)), ((), ()))

    def rows(i):
        return pl.ds(i * t_blk, t_blk)

    def split2(a):
        hi = a.astype(BF16)
        return [hi, (a - hi.astype(F32)).astype(BF16)]

    dirs = ((GLA_HALF - 1, t_blk - 1), (GLA_HALF, 0))

    def cum_block(ars, i, row0):
        af = ars[0][0, rows(i), :]
        ab = ars[1][0, rows(i), :]
        z = jnp.dot(cum_f, jnp.concatenate(split2(af) + split2(ab), axis=1), preferred_element_type=F32)
        c_ref[0, rows(row0 + i), :] = z[:, 0:dk2] + z[:, dk2:2 * dk2]
        pb = z[:, 2 * dk2:3 * dk2] + z[:, 3 * dk2:4 * dk2]
        c_ref[1, rows(row0 + i), :] = pb[t_blk - 1:t_blk, :] - pb + ab

    def mid_block(qr, kr, vr, i, j, row0, outputs=True):
        r = rows(i)
        cr = rows(row0 + i)
        q = qr[0, r, :].astype(F32)
        k = kr[0, r, :].astype(F32)
        v = vr[0, r, :]
        sc, k_end = [], []
        for d, (mid, edge) in enumerate(dirs):
            c = c_ref[d, cr, :]
            c_mid = c[mid:mid + 1, :]
            c_edge = c[edge:edge + 1, :]
            k_mid = k * jnp.exp2(c_mid - c)
            k_end.append((k_mid * jnp.exp2(c_edge - c_mid)).astype(BF16))
            dec_ref[d, j] = jnp.broadcast_to(jnp.exp2(c_edge), (8, dk2))
            if not outputs:
                continue
            q_mid = q * jnp.exp2(c - c_mid)
            qin_ref[d, cr, :] = (q_mid * jnp.exp2(c_mid)).astype(BF16)
            k_mid_b = k_mid.astype(BF16)
            zero = jnp.zeros_like(k_mid_b)
            k_heads = jnp.concatenate([jnp.where(head0, k_mid_b, zero), jnp.where(head0, zero, k_mid_b)], axis=0)
            sc.append(lax.dot_general(q_mid.astype(BF16), k_heads, nt_dims, preferred_element_type=F32))
        kv = lax.dot_general(v, jnp.concatenate(k_end, axis=1), tn_dims, preferred_element_type=F32)
        for d in range(2):
            kvd = kv[:, d * dk2:(d + 1) * dk2]
            kv_ref[d, j] = jnp.where(head0, kvd[0:dv, :], kvd[dv:2 * dv, :])
        for hd in range(2 if outputs else 0):
            hs = slice(hd * t_blk, (hd + 1) * t_blk)
            sc_ref[hd, j] = jnp.where(lower, sc[0][:, hs], sc[1][:, hs]).astype(BF16)

    def out_block(qr, vr, gr, yr, i, j, row0):
        r = rows(i)
        cr = rows(row0 + i)
        v = vr[0, r, :]
        s_exp = []
        for d in range(2):
            s = st_ref[d, j]
            zero = jnp.zeros_like(s)
            s_exp.append(jnp.concatenate([jnp.where(head0, s, zero), jnp.where(head0, zero, s)], axis=0))
        q_in = jnp.concatenate([qin_ref[0, cr, :], qin_ref[1, cr, :]], axis=1)
        o = jnp.concatenate([jnp.dot(sc_ref[hd, j], v[:, hd * dv:(hd + 1) * dv], preferred_element_type=F32)
                             for hd in range(2)], axis=1)
        o = o + lax.dot_general(q_in, jnp.concatenate(s_exp, axis=1), nt_dims, preferred_element_type=F32)
        gate = gr[0, r, :].astype(F32)
        gn = gn_ref[...]
        for hd in range(2):
            sl = slice(hd * dv, (hd + 1) * dv)
            oh = o[:, sl]
            ms = jnp.mean(oh * oh, axis=-1, keepdims=True)
            yr[0, r, sl] = (oh * lax.rsqrt(ms + EPS) * gn * gate[:, sl]).astype(BF16)

    def for_blocks(n, fn):
        for i in range(n):
            fn(i)

    n_ctx = cq_ref.shape[1] // t_blk
    n_lat = q_ref.shape[1] // t_blk
    n_tot = n_ctx + n_lat
    for_blocks(n_ctx, lambda i: cum_block((caf_ref, cab_ref), i, 0))
    for_blocks(n_lat, lambda i: cum_block((af_ref, ab_ref), i, n_ctx))
    for_blocks(n_ctx, lambda i: mid_block(cq_ref, ck_ref, cv_ref, i, i, 0, outputs=need_ctx))
    for_blocks(n_lat, lambda i: mid_block(q_ref, k_ref, v_ref, i, n_ctx + i, n_ctx))

    def step(d, j, s):
        st_ref[d, j] = s.astype(BF16)
        return dec_ref[d, j, 0:1, :] * s + kv_ref[d, j]

    zero = jnp.zeros((dv, dk2), F32)
    lax.fori_loop(0, n_tot, lambda j, s: step(0, j, s), zero)
    s_b = lax.fori_loop(0, n_ctx, lambda t, s: step(1, n_ctx - 1 - t, s), zero)
    lax.fori_loop(0, n_lat, lambda t, s: step(1, n_tot - 1 - t, s), s_b)

    if need_ctx:
        for_blocks(n_ctx, lambda i: out_block(cq_ref, cv_ref, cg_ref, cy_ref, i, i, 0))
    else:
        cy_ref[...] = jnp.zeros(cy_ref.shape, cy_ref.dtype)
    for_blocks(n_lat, lambda i: out_block(q_ref, v_ref, g_ref, y_ref, i, n_ctx + i, n_ctx))


def _gla(gn, layer, lat, ctx, need_ctx):
    gq, gk, gv, g, ga = lat
    cgq, cgk, cgv, cg, cga = ctx
    b, seq, _ = gq.shape
    cl = cgq.shape[1]
    pair_k = 2 * GLA_KEY_DIM
    pair_v = 2 * GLA_VAL_DIM
    n_pairs = N_GLA_HEADS // 2
    n_blocks = (seq + cl) // GLA_BLOCK
    gate_off = RET_W // pair_v
    key = lambda n: pl.BlockSpec((1, n, pair_k), lambda i, p: (i, 0, p))
    key_b = lambda n: pl.BlockSpec((1, n, pair_k), lambda i, p: (i, 0, n_pairs + p))
    val = lambda n: pl.BlockSpec((1, n, pair_v), lambda i, p: (i, 0, p))
    mix = lambda n: pl.BlockSpec((1, n, pair_v), lambda i, p: (i, 0, gate_off + p))
    return pl.pallas_call(
        functools.partial(_gla_kernel, need_ctx=need_ctx),
        out_shape=(jax.ShapeDtypeStruct((b, seq, GLA_V), BF16), jax.ShapeDtypeStruct((b, cl, GLA_V), BF16)),
        grid=(b, n_pairs),
        in_specs=[_layer_spec(gn.shape[1:], layer),
                  key(seq), key(seq), val(seq), mix(seq), key(seq), key_b(seq),
                  key(cl), key(cl), val(cl), mix(cl), key(cl), key_b(cl)],
        out_specs=(val(seq), val(cl)),
        scratch_shapes=[pltpu.VMEM((2, seq + cl, pair_k), F32),
                        pltpu.VMEM((2, n_blocks, GLA_VAL_DIM, pair_k), F32),
                        pltpu.VMEM((2, n_blocks, 8, pair_k), F32),
                        pltpu.VMEM((2, n_blocks, GLA_VAL_DIM, pair_k), BF16),
                        pltpu.VMEM((2, n_blocks, GLA_BLOCK, GLA_BLOCK), BF16),
                        pltpu.VMEM((2, seq + cl, pair_k), BF16)],
        compiler_params=pltpu.CompilerParams(dimension_semantics=("parallel", "parallel"),
                                             vmem_limit_bytes=VMEM_LIMIT),
        name="gla_scan",
    )(gn, gq, gk, gv, g, ga, ga, cgq, cgk, cgv, cg, cga, cga)


def _mlp_kernel(x_ref, yr_ref, yg_ref, mod_ref, n2_ref, wo_ref, w1_ref, w2_ref, fg_ref, o_ref, *,
                final, ff_chunk):
    mod = mod_ref[0]
    mix = (jnp.dot(yr_ref[0], wo_ref[0:RET_W, :], preferred_element_type=F32)
           + jnp.dot(yg_ref[0], wo_ref[RET_W:MIX_W, :], preferred_element_type=F32))
    x1 = x_ref[0] + mod[2:3] * mix
    gain = n2_ref[...] * (1.0 + mod[4:5])
    ms = jnp.mean(x1 * x1, axis=-1, keepdims=True)
    hb = (x1 * lax.rsqrt(ms + EPS) * gain + mod[3:4]).astype(BF16)
    acc = jnp.zeros(x1.shape, F32)
    for c in range(w1_ref.shape[1] // ff_chunk):
        sl = slice(c * ff_chunk, (c + 1) * ff_chunk)
        a = jnp.maximum(jnp.dot(hb, w1_ref[:, sl], preferred_element_type=F32), 0.0)
        acc = acc + jnp.dot((a * a).astype(BF16), w2_ref[sl, :], preferred_element_type=F32)
    x2 = x1 + mod[5:6] * acc
    if final:
        ms = jnp.mean(x2 * x2, axis=-1, keepdims=True)
        x2 = x2 * lax.rsqrt(ms + EPS) * fg_ref[...]
    o_ref[0] = x2


def _out_mlp(xa, yr, yg, mod, mod_index, layer, n2, wo, w1, w2, fg, tm, final):
    b, seq, d = xa.shape
    tok = lambda n: pl.BlockSpec((1, tm, n), lambda i, j: (i, j, 0))
    return pl.pallas_call(
        functools.partial(_mlp_kernel, final=final, ff_chunk=1024),
        out_shape=jax.ShapeDtypeStruct(xa.shape, F32),
        grid=(b, seq // tm),
        in_specs=[tok(d), tok(RET_W), tok(GLA_V), pl.BlockSpec((1, 6, d), lambda i, j: (mod_index(i), 0, 0)),
                  _layer_spec(n2.shape[1:], layer), _layer_spec(wo.shape[1:], layer),
                  _layer_spec(w1.shape[1:], layer), _layer_spec(w2.shape[1:], layer), _const_spec((1, d))],
        out_specs=tok(d),
        compiler_params=pltpu.CompilerParams(dimension_semantics=("parallel", "parallel"),
                                             vmem_limit_bytes=VMEM_LIMIT),
        name="out_mlp",
    )(xa, yr, yg, mod, n2, wo, w1, w2, fg)


def kernel(x, c, ctx, c_ctx, ada_w, ada_b, norm1_g, w_in, ret_decay, gla_gate_up, gla_gate_b, gla_norm_g,
           w_out, norm2_g, w_mlp1, w_mlp2, final_g):
    batch, seq, d = x.shape
    cl = ctx.shape[1]
    depth = ada_w.shape[0]
    assert w_in.shape[2] == IN_W
    mod_rows = 16
    cs = jnp.concatenate([c, c_ctx[None, :], jnp.zeros((mod_rows - batch - 1, d), c.dtype)], axis=0)
    mod = _modulation(cs, ada_w, ada_b).reshape(depth * mod_rows, 6, d)

    lat_tables = _rope_tables(seq)
    scale = RET_HEAD_DIM ** -0.5
    ones = jnp.ones((cl, RET_HEAD_DIM), F32)
    ctx_tables = (ones * scale, ones * 0.0, ones, ones * 0.0)

    w_in_b = w_in.astype(BF16)
    zero = jnp.zeros((depth, GLA_GATE_RANK, GLA_K), gla_gate_up.dtype)
    up = jnp.concatenate([jnp.concatenate([gla_gate_up[:, 0], zero], axis=2),
                          jnp.concatenate([zero, gla_gate_up[:, 1]], axis=2)], axis=1).astype(BF16)
    ub = gla_gate_b.reshape(depth, 1, 2 * GLA_K)
    wo = w_out.astype(BF16)
    w1 = w_mlp1.astype(BF16)
    w2 = w_mlp2.astype(BF16)
    n1 = norm1_g.reshape(depth, 1, d)
    n2 = norm2_g.reshape(depth, 1, d)
    gn = gla_norm_g.reshape(depth, 1, GLA_VAL_DIM)
    fg = final_g.reshape(1, d)
    rd = jnp.broadcast_to(jnp.swapaxes(ret_decay, 1, 2)[..., None], (depth, N_RET_HEADS, 2, RET_BLOCK))

    tm = math.gcd(seq, ROW_TILE)
    for layer in range(depth):
        last = layer == depth - 1
        lat_mod = lambda i, layer=layer: layer * mod_rows + i
        ctx_mod = lambda i, layer=layer: layer * mod_rows + batch
        lat = _project(x, mod, lat_mod, layer, n1, w_in_b, up, ub, lat_tables, tm)
        cx = _project(ctx, mod, ctx_mod, layer, n1, w_in_b, up, ub, ctx_tables, cl)
        rq, rk, rv, g, gq, gk, gv, ga = lat
        crq, crk, crv, cg, cgq, cgk, cgv, cga = cx
        yr, cyr = _retention(rd, layer, (rq, rk, rv, g), (crq, crk, crv, cg), not last)
        yg, cyg = _gla(gn, layer, (gq, gk, gv, g, ga), (cgq, cgk, cgv, cg, cga), not last)
        x = _out_mlp(x, yr, yg, mod, lat_mod, layer, n2, wo, w1, w2, fg, tm, last)
        if not last:
            ctx = _out_mlp(ctx, cyr, cyg, mod, ctx_mod, layer, n2, wo, w1, w2, fg, cl, False)
    return x
```

```python
import functools
import math

import jax
import jax.numpy as jnp
from jax import lax
from jax.experimental import pallas as pl
from jax.experimental.pallas import tpu as pltpu

F32 = jnp.float32
BF16 = jnp.bfloat16

GRID_W = 64
N_RET_HEADS = 4
RET_HEAD_DIM = 128
N_GLA_HEADS = 4
GLA_KEY_DIM = 64
GLA_VAL_DIM = 128
GLA_GATE_RANK = 16
GLA_GATE_NORM = 16.0
ROPE_BASE = 10000.0
EPS = 1e-6

RET_W = N_RET_HEADS * RET_HEAD_DIM
GLA_K = N_GLA_HEADS * GLA_KEY_DIM
GLA_V = N_GLA_HEADS * GLA_VAL_DIM
MIX_W = RET_W + GLA_V

OFF_RV = 2 * RET_W
OFF_RG = 3 * RET_W
OFF_GQ = 4 * RET_W
OFF_GK = OFF_GQ + GLA_K
OFF_GV = OFF_GK + GLA_K
OFF_GG = OFF_GV + GLA_V
OFF_D = OFF_GG + GLA_V
IN_W = OFF_D + 2 * GLA_GATE_RANK

RET_BLOCK = 256
GLA_BLOCK = 128
GLA_HALF = GLA_BLOCK // 2

ROW_TILE = 1024
LOG2E = 1.4426950408889634

VMEM_LIMIT = 56 * 1024 * 1024


def _layer_spec(shape, layer):
    zeros = (0,) * len(shape)
    return pl.BlockSpec((None,) + tuple(shape), lambda *_: (layer,) + zeros, pipeline_mode=pl.Buffered(1))


def _const_spec(shape):
    zeros = (0,) * len(shape)
    return pl.BlockSpec(shape, lambda *_: zeros, pipeline_mode=pl.Buffered(1))


def _mod_kernel(cs_ref, w_ref, b_ref, o_ref):
    cs = cs_ref[...]
    s = cs * jax.nn.sigmoid(cs)
    o_ref[0] = jnp.dot(s.astype(BF16), w_ref[0].astype(BF16), preferred_element_type=F32) + b_ref[0]


def _modulation(cs, ada_w, ada_b):
    depth, d, n = ada_w.shape
    rows = cs.shape[0]
    tn = 1024
    return pl.pallas_call(
        _mod_kernel,
        out_shape=jax.ShapeDtypeStruct((depth, rows, n), F32),
        grid=(depth, n // tn),
        in_specs=[
            pl.BlockSpec((rows, d), lambda l, j: (0, 0)),
            pl.BlockSpec((1, d, tn), lambda l, j: (l, 0, j)),
            pl.BlockSpec((1, 1, tn), lambda l, j: (l, 0, j)),
        ],
        out_specs=pl.BlockSpec((1, rows, tn), lambda l, j: (l, 0, j)),
        compiler_params=pltpu.CompilerParams(dimension_semantics=("parallel", "parallel"),
                                             vmem_limit_bytes=VMEM_LIMIT),
        name="adaln_modulation",
    )(cs, ada_w, ada_b.reshape(depth, 1, n))


_QUARTER = RET_HEAD_DIM // 4


def _rope_kernel(cq_ref, sq_ref, ck_ref, sk_ref):
    shape = (GRID_W, RET_HEAD_DIM)
    p = lax.broadcasted_iota(jnp.int32, shape, 0).astype(F32)
    lane = lax.broadcasted_iota(jnp.int32, shape, 1)
    freq = (lane & (_QUARTER - 1)).astype(F32)
    ang = p * jnp.exp(freq * (-math.log(ROPE_BASE) / _QUARTER))
    cos = jnp.cos(ang)
    sin = jnp.where((lane & (2 * _QUARTER - 1)) < _QUARTER, -jnp.sin(ang), jnp.sin(ang))
    by_row = lane < 2 * _QUARTER
    scale = RET_HEAD_DIM ** -0.5
    for r in range(cq_ref.shape[0] // GRID_W):
        c_blk = jnp.where(by_row, cos[r:r + 1, :], cos)
        s_blk = jnp.where(by_row, sin[r:r + 1, :], sin)
        rs = slice(r * GRID_W, (r + 1) * GRID_W)
        cq_ref[rs, :] = c_blk * scale
        sq_ref[rs, :] = s_blk * scale
        ck_ref[rs, :] = c_blk
        sk_ref[rs, :] = s_blk


def _rope_tables(seq):
    sds = jax.ShapeDtypeStruct((seq, RET_HEAD_DIM), F32)
    return pl.pallas_call(_rope_kernel, out_shape=(sds, sds, sds, sds), name="rope_tables")()


def _silu(z):
    hz = 0.5 * z
    return hz * (1.0 + jnp.tanh(hz))


def _proj_kernel(x_ref, mod_ref, n1_ref, w_ref, up_ref, ub_ref, cq_ref, sq_ref, ck_ref, sk_ref,
                 rq_ref, rk_ref, rv_ref, g_ref, gq_ref, gk_ref, gv_ref, ga_ref, *, states_only):
    x = x_ref[0]
    mod = mod_ref[0]
    gain = n1_ref[...] * (1.0 + mod[1:2])
    ms = jnp.mean(x * x, axis=-1, keepdims=True)
    hb = (x * lax.rsqrt(ms + EPS) * gain + mod[0:1]).astype(BF16)

    def mm(off, n):
        return jnp.dot(hb, w_ref[:, off:off + n], preferred_element_type=F32)

    lane = lax.broadcasted_iota(jnp.int32, (x.shape[0], RET_HEAD_DIM), 1)
    first = (lane & (2 * _QUARTER - 1)) < _QUARTER

    def rotary(z, c_ref, s_ref, o_ref):
        c = c_ref[...]
        s = s_ref[...]
        for hd in range(N_RET_HEADS):
            sl = slice(hd * RET_HEAD_DIM, (hd + 1) * RET_HEAD_DIM)
            zh = z[:, sl]
            partner = jnp.where(first, pltpu.roll(zh, RET_HEAD_DIM - _QUARTER, axis=1),
                                pltpu.roll(zh, _QUARTER, axis=1))
            o_ref[0, :, sl] = (zh * c + partner * s).astype(BF16)

    d = mm(OFF_D, 2 * GLA_GATE_RANK).astype(BF16)
    zg = jnp.dot(d, up_ref[...], preferred_element_type=F32) + ub_ref[...]
    soft = jnp.log2(1.0 + jnp.exp2(jnp.abs(zg) * -LOG2E))
    ga_ref[0] = jnp.minimum(zg, 0.0) * (LOG2E / GLA_GATE_NORM) - soft * (1.0 / GLA_GATE_NORM)
    rv_ref[0] = mm(OFF_RV, RET_W).astype(BF16)
    gk_ref[0] = mm(OFF_GK, GLA_K).astype(BF16)
    rotary(mm(RET_W, RET_W), ck_ref, sk_ref, rk_ref)
    gv_ref[0] = mm(OFF_GV, GLA_V).astype(BF16)
    if states_only:
        for ref in (rq_ref, g_ref, gq_ref):
            ref[...] = jnp.zeros(ref.shape, ref.dtype)
        return
    rotary(mm(0, RET_W), cq_ref, sq_ref, rq_ref)
    g_ref[0, :, 0:RET_W] = _silu(mm(OFF_RG, RET_W)).astype(BF16)
    gq_ref[0] = (mm(OFF_GQ, GLA_K) * (GLA_KEY_DIM ** -0.5)).astype(BF16)
    g_ref[0, :, RET_W:MIX_W] = _silu(mm(OFF_GG, GLA_V)).astype(BF16)


def _project(xa, mod, mod_index, layer, n1, w, up, ub, tables, tm, states_only=False):
    b, seq, d = xa.shape
    tok = lambda n: pl.BlockSpec((1, tm, n), lambda i, j: (i, j, 0))
    tab = pl.BlockSpec((tm, RET_HEAD_DIM), lambda i, j: (j, 0))
    sd = lambda n, dt: jax.ShapeDtypeStruct((b, seq, n), dt)
    return pl.pallas_call(
        functools.partial(_proj_kernel, states_only=states_only),
        out_shape=(sd(RET_W, BF16), sd(RET_W, BF16), sd(RET_W, BF16), sd(MIX_W, BF16),
                   sd(GLA_K, BF16), sd(GLA_K, BF16), sd(GLA_V, BF16), sd(2 * GLA_K, F32)),
        grid=(b, seq // tm),
        in_specs=[tok(d), pl.BlockSpec((1, 6, d), lambda i, j: (mod_index(i), 0, 0)),
                  _layer_spec(n1.shape[1:], layer),
                  _layer_spec(w.shape[1:], layer), _layer_spec(up.shape[1:], layer),
                  _layer_spec(ub.shape[1:], layer), tab, tab, tab, tab],
        out_specs=(tok(RET_W), tok(RET_W), tok(RET_W), tok(MIX_W), tok(GLA_K), tok(GLA_K), tok(GLA_V),
                   tok(2 * GLA_K)),
        compiler_params=pltpu.CompilerParams(dimension_semantics=("parallel", "parallel"),
                                             vmem_limit_bytes=VMEM_LIMIT),
        name="norm_project",
    )(xa, mod, n1, w, up, ub, *tables)


def _ret_kernel(rd_ref, q_ref, k_ref, v_ref, g_ref, cq_ref, ck_ref, cv_ref, cg_ref,
                y_ref, cy_ref, kv_ref, st_ref, sc_ref, cmask_ref, cdec_ref, *, need_ctx):
    t_blk = RET_BLOCK
    dh = RET_HEAD_DIM
    head = pl.program_id(1)

    @pl.when(pl.program_id(0) == 0)
    def _():
        lg = jnp.log1p(-jnp.exp(rd_ref[0]))
        lgf, lgb = lg[0:1, :], lg[1:2, :]
        lgf_h, lgb_h = lgf[:, :dh], lgb[:, :dh]
        ti = lax.broadcasted_iota(jnp.int32, (t_blk, t_blk), 0)
        si = lax.broadcasted_iota(jnp.int32, (t_blk, t_blk), 1)
        diff = (ti - si).astype(F32)
        cmask_ref[head] = jnp.exp(jnp.where(diff >= 0, diff * lgf, -diff * lgb))
        tr = lax.broadcasted_iota(jnp.int32, (t_blk, dh), 0).astype(F32)
        cdec_ref[head, 0:t_blk, 0:dh] = jnp.exp((tr + 1.0) * lgf_h)
        cdec_ref[head, 0:t_blk, dh:2 * dh] = jnp.exp((t_blk - tr) * lgb_h)
        cdec_ref[head, 0:t_blk, 2 * dh:3 * dh] = jnp.exp((t_blk - 1.0 - tr) * lgf_h)
        cdec_ref[head, 0:t_blk, 3 * dh:4 * dh] = jnp.exp(tr * lgb_h)
        cdec_ref[head, t_blk:t_blk + 8, 0:dh] = jnp.broadcast_to(jnp.exp(t_blk * lgf_h), (8, dh))
        cdec_ref[head, t_blk:t_blk + 8, dh:2 * dh] = jnp.broadcast_to(jnp.exp(t_blk * lgb_h), (8, dh))

    mask = cmask_ref[head]
    q_dec = cdec_ref[head, 0:t_blk, 0:2 * dh]
    k_dec_f = cdec_ref[head, 0:t_blk, 2 * dh:3 * dh]
    k_dec_b = cdec_ref[head, 0:t_blk, 3 * dh:4 * dh]
    blk_f = cdec_ref[head, t_blk:t_blk + 1, 0:dh]
    blk_b = cdec_ref[head, t_blk:t_blk + 1, dh:2 * dh]
    tn_dims = (((0,), (0,)), ((), ()))
    nt_dims = (((1,), (1,)), ((), ()))
    n_ctx = cq_ref.shape[1] // t_blk
    n_lat = q_ref.shape[1] // t_blk
    n_tot = n_ctx + n_lat
    fwd = slice(0, dh)
    bwd = slice(dh, 2 * dh)

    def rows(i):
        return pl.ds(i * t_blk, t_blk)

    def kv_block(kr, vr, i, j):
        r = rows(i)
        v = vr[0, r, :].astype(F32)
        vv = jnp.concatenate([(v * k_dec_f).astype(BF16), (v * k_dec_b).astype(BF16)], axis=1)
        kv_ref[j] = lax.dot_general(kr[0, r, :], vv, tn_dims, preferred_element_type=F32)

    def score_block(qr, kr, i, j):
        r = rows(i)
        a = lax.dot_general(qr[0, r, :], kr[0, r, :], nt_dims, preferred_element_type=F32) * mask
        sc_ref[j] = a.astype(BF16)

    def out_block(qr, vr, gr, yr, i, j):
        r = rows(i)
        q, v = qr[0, r, :], vr[0, r, :]
        o = jnp.dot(sc_ref[j], v, preferred_element_type=F32)
        inter = jnp.dot(q, st_ref[j], preferred_element_type=F32) * q_dec
        o = o + inter[:, fwd] + inter[:, bwd]
        mu = jnp.mean(o, axis=-1, keepdims=True)
        dlt = o - mu
        var = jnp.mean(dlt * dlt, axis=-1, keepdims=True)
        yr[0, r, :] = (dlt * lax.rsqrt(var + EPS) * gr[0, r, :].astype(F32)).astype(BF16)

    def for_blocks(n, fn):
        for i in range(n):
            fn(i)

    for_blocks(n_ctx, lambda i: kv_block(ck_ref, cv_ref, i, i))
    for_blocks(n_lat, lambda i: kv_block(k_ref, v_ref, i, n_ctx + i))

    def step_f(j, s):
        st_ref[j, :, fwd] = s.astype(BF16)
        return blk_f * s + kv_ref[j, :, fwd]

    def step_b(j, s):
        st_ref[j, :, bwd] = s.astype(BF16)
        return blk_b * s + kv_ref[j, :, bwd]

    zero = jnp.zeros((dh, dh), F32)
    lax.fori_loop(0, n_tot, step_f, zero)
    s_b = lax.fori_loop(0, n_ctx, lambda t, s: step_b(n_ctx - 1 - t, s), zero)
    lax.fori_loop(0, n_lat, lambda t, s: step_b(n_tot - 1 - t, s), s_b)

    if need_ctx:
        for_blocks(n_ctx, lambda i: score_block(cq_ref, ck_ref, i, i))
    for_blocks(n_lat, lambda i: score_block(q_ref, k_ref, i, n_ctx + i))
    if need_ctx:
        for_blocks(n_ctx, lambda i: out_block(cq_ref, cv_ref, cg_ref, cy_ref, i, i))
    else:
        cy_ref[...] = jnp.zeros(cy_ref.shape, cy_ref.dtype)
    for_blocks(n_lat, lambda i: out_block(q_ref, v_ref, g_ref, y_ref, i, n_ctx + i))


def _retention(rd, layer, lat, ctx, need_ctx):
    rq, rk, rv, g = lat
    crq, crk, crv, cg = ctx
    b, seq, _ = rq.shape
    cl = crq.shape[1]
    dh = RET_HEAD_DIM
    n_blocks = (seq + cl) // RET_BLOCK
    head = lambda n: pl.BlockSpec((1, n, dh), lambda i, h: (i, 0, h))
    return pl.pallas_call(
        functools.partial(_ret_kernel, need_ctx=need_ctx),
        out_shape=(jax.ShapeDtypeStruct((b, seq, RET_W), BF16), jax.ShapeDtypeStruct((b, cl, RET_W), BF16)),
        grid=(b, N_RET_HEADS),
        in_specs=[pl.BlockSpec((None, 1, 2, RET_BLOCK), lambda i, h: (layer, h, 0, 0)),
                  head(seq), head(seq), head(seq), head(seq), head(cl), head(cl), head(cl), head(cl)],
        out_specs=(head(seq), head(cl)),
        scratch_shapes=[pltpu.VMEM((n_blocks, dh, 2 * dh), F32), pltpu.VMEM((n_blocks, dh, 2 * dh), BF16),
                        pltpu.VMEM((n_blocks, RET_BLOCK, RET_BLOCK), BF16),
                        pltpu.VMEM((N_RET_HEADS, RET_BLOCK, RET_BLOCK), F32),
                        pltpu.VMEM((N_RET_HEADS, RET_BLOCK + 8, 4 * dh), F32)],
        compiler_params=pltpu.CompilerParams(dimension_semantics=("arbitrary", "arbitrary"),
                                             vmem_limit_bytes=VMEM_LIMIT),
        name="retention_scan",
    )(rd, rq, rk, rv, g, crq, crk, crv, cg)


def _gla_kernel(gn_ref, q_ref, k_ref, v_ref, g_ref, af_ref, ab_ref, cq_ref, ck_ref, cv_ref, cg_ref,
                caf_ref, cab_ref, y_ref, cy_ref, c_ref, kv_ref, dec_ref, st_ref, sc_ref, qin_ref, *, need_ctx):
    t_blk = GLA_BLOCK
    dv = GLA_VAL_DIM
    dk2 = 2 * GLA_KEY_DIM
    assert t_blk == dk2 == dv
    ti = lax.broadcasted_iota(jnp.int32, (t_blk, t_blk), 0)
    si = lax.broadcasted_iota(jnp.int32, (t_blk, t_blk), 1)
    lower = si <= ti
    cum_f = lower.astype(BF16)
    head0 = si < GLA_KEY_DIM
    tn_dims = (((0,), (0,)), ((), ()))
    nt_dims = (((1,), (1,)), ((), ()))

    def rows(i):
        return pl.ds(i * t_blk, t_blk)

    def split2(a):
        hi = a.astype(BF16)
        return [hi, (a - hi.astype(F32)).astype(BF16)]

    dirs = ((GLA_HALF - 1, t_blk - 1), (GLA_HALF, 0))

    def cum_block(ars, i, row0):
        af = ars[0][0, rows(i), :]
        ab = ars[1][0, rows(i), :]
        z = jnp.dot(cum_f, jnp.concatenate(split2(af) + split2(ab), axis=1), preferred_element_type=F32)
        c_ref[0, rows(row0 + i), :] = z[:, 0:dk2] + z[:, dk2:2 * dk2]
        pb = z[:, 2 * dk2:3 * dk2] + z[:, 3 * dk2:4 * dk2]
        c_ref[1, rows(row0 + i), :] = pb[t_blk - 1:t_blk, :] - pb + ab

    def mid_block(qr, kr, vr, i, j, row0, outputs=True):
        r = rows(i)
        cr = rows(row0 + i)
        q = qr[0, r, :].astype(F32)
        k = kr[0, r, :].astype(F32)
        v = vr[0, r, :]
        sc, k_end = [], []
        for d, (mid, edge) in enumerate(dirs):
            c = c_ref[d, cr, :]
            c_mid = c[mid:mid + 1, :]
            c_edge = c[edge:edge + 1, :]
            k_mid = k * jnp.exp2(c_mid - c)
            k_end.append((k_mid * jnp.exp2(c_edge - c_mid)).astype(BF16))
            dec_ref[d, j] = jnp.broadcast_to(jnp.exp2(c_edge), (8, dk2))
            if not outputs:
                continue
            q_mid = q * jnp.exp2(c - c_mid)
            qin_ref[d, cr, :] = (q_mid * jnp.exp2(c_mid)).astype(BF16)
            k_mid_b = k_mid.astype(BF16)
            zero = jnp.zeros_like(k_mid_b)
            k_heads = jnp.concatenate([jnp.where(head0, k_mid_b, zero), jnp.where(head0, zero, k_mid_b)], axis=0)
            sc.append(lax.dot_general(q_mid.astype(BF16), k_heads, nt_dims, preferred_element_type=F32))
        kv = lax.dot_general(v, jnp.concatenate(k_end, axis=1), tn_dims, preferred_element_type=F32)
        for d in range(2):
            kvd = kv[:, d * dk2:(d + 1) * dk2]
            kv_ref[d, j] = jnp.where(head0, kvd[0:dv, :], kvd[dv:2 * dv, :])
        for hd in range(2 if outputs else 0):
            hs = slice(hd * t_blk, (hd + 1) * t_blk)
            sc_ref[hd, j] = jnp.where(lower, sc[0][:, hs], sc[1][:, hs]).astype(BF16)

    def out_block(qr, vr, gr, yr, i, j, row0):
        r = rows(i)
        cr = rows(row0 + i)
        v = vr[0, r, :]
        s_exp = []
        for d in range(2):
            s = st_ref[d, j]
            zero = jnp.zeros_like(s)
            s_exp.append(jnp.concatenate([jnp.where(head0, s, zero), jnp.where(head0, zero, s)], axis=0))
        q_in = jnp.concatenate([qin_ref[0, cr, :], qin_ref[1, cr, :]], axis=1)
        o = jnp.concatenate([jnp.dot(sc_ref[hd, j], v[:, hd * dv:(hd + 1) * dv], preferred_element_type=F32)
                             for hd in range(2)], axis=1)
        o = o + lax.dot_general(q_in, jnp.concatenate(s_exp, axis=1), nt_dims, preferred_element_type=F32)
        gate = gr[0, r, :].astype(F32)
        gn = gn_ref[...]
        for hd in range(2):
            sl = slice(hd * dv, (hd + 1) * dv)
            oh = o[:, sl]
            ms = jnp.mean(oh * oh, axis=-1, keepdims=True)
            yr[0, r, sl] = (oh * lax.rsqrt(ms + EPS) * gn * gate[:, sl]).astype(BF16)

    def for_blocks(n, fn):
        for i in range(n):
            fn(i)

    n_ctx = cq_ref.shape[1] // t_blk
    n_lat = q_ref.shape[1] // t_blk
    n_tot = n_ctx + n_lat
    for_blocks(n_ctx, lambda i: cum_block((caf_ref, cab_ref), i, 0))
    for_blocks(n_lat, lambda i: cum_block((af_ref, ab_ref), i, n_ctx))
    for_blocks(n_ctx, lambda i: mid_block(cq_ref, ck_ref, cv_ref, i, i, 0, outputs=need_ctx))
    for_blocks(n_lat, lambda i: mid_block(q_ref, k_ref, v_ref, i, n_ctx + i, n_ctx))

    def step(d, j, s):
        st_ref[d, j] = s.astype(BF16)
        return dec_ref[d, j, 0:1, :] * s + kv_ref[d, j]

    zero = jnp.zeros((dv, dk2), F32)
    lax.fori_loop(0, n_tot, lambda j, s: step(0, j, s), zero)
    s_b = lax.fori_loop(0, n_ctx, lambda t, s: step(1, n_ctx - 1 - t, s), zero)
    lax.fori_loop(0, n_lat, lambda t, s: step(1, n_tot - 1 - t, s), s_b)

    if need_ctx:
        for_blocks(n_ctx, lambda i: out_block(cq_ref, cv_ref, cg_ref, cy_ref, i, i, 0))
    else:
        cy_ref[...] = jnp.zeros(cy_ref.shape, cy_ref.dtype)
    for_blocks(n_lat, lambda i: out_block(q_ref, v_ref, g_ref, y_ref, i, n_ctx + i, n_ctx))


def _gla(gn, layer, lat, ctx, need_ctx):
    gq, gk, gv, g, ga = lat
    cgq, cgk, cgv, cg, cga = ctx
    b, seq, _ = gq.shape
    cl = cgq.shape[1]
    pair_k = 2 * GLA_KEY_DIM
    pair_v = 2 * GLA_VAL_DIM
    n_pairs = N_GLA_HEADS // 2
    n_blocks = (seq + cl) // GLA_BLOCK
    gate_off = RET_W // pair_v
    key = lambda n: pl.BlockSpec((1, n, pair_k), lambda i, p: (i, 0, p))
    key_b = lambda n: pl.BlockSpec((1, n, pair_k), lambda i, p: (i, 0, n_pairs + p))
    val = lambda n: pl.BlockSpec((1, n, pair_v), lambda i, p: (i, 0, p))
    mix = lambda n: pl.BlockSpec((1, n, pair_v), lambda i, p: (i, 0, gate_off + p))
    return pl.pallas_call(
        functools.partial(_gla_kernel, need_ctx=need_ctx),
        out_shape=(jax.ShapeDtypeStruct((b, seq, GLA_V), BF16), jax.ShapeDtypeStruct((b, cl, GLA_V), BF16)),
        grid=(b, n_pairs),
        in_specs=[_layer_spec(gn.shape[1:], layer),
                  key(seq), key(seq), val(seq), mix(seq), key(seq), key_b(seq),
                  key(cl), key(cl), val(cl), mix(cl), key(cl), key_b(cl)],
        out_specs=(val(seq), val(cl)),
        scratch_shapes=[pltpu.VMEM((2, seq + cl, pair_k), F32),
                        pltpu.VMEM((2, n_blocks, GLA_VAL_DIM, pair_k), F32),
                        pltpu.VMEM((2, n_blocks, 8, pair_k), F32),
                        pltpu.VMEM((2, n_blocks, GLA_VAL_DIM, pair_k), BF16),
                        pltpu.VMEM((2, n_blocks, GLA_BLOCK, GLA_BLOCK), BF16),
                        pltpu.VMEM((2, seq + cl, pair_k), BF16)],
        compiler_params=pltpu.CompilerParams(dimension_semantics=("parallel", "parallel"),
                                             vmem_limit_bytes=VMEM_LIMIT),
        name="gla_scan",
    )(gn, gq, gk, gv, g, ga, ga, cgq, cgk, cgv, cg, cga, cga)


def _mlp_kernel(x_ref, yr_ref, yg_ref, mod_ref, n2_ref, wo_ref, w1_ref, w2_ref, fg_ref, o_ref, *,
                final, ff_chunk):
    mod = mod_ref[0]
    mix = (jnp.dot(yr_ref[0], wo_ref[0:RET_W, :], preferred_element_type=F32)
           + jnp.dot(yg_ref[0], wo_ref[RET_W:MIX_W, :], preferred_element_type=F32))
    x1 = x_ref[0] + mod[2:3] * mix
    gain = n2_ref[...] * (1.0 + mod[4:5])
    ms = jnp.mean(x1 * x1, axis=-1, keepdims=True)
    hb = (x1 * lax.rsqrt(ms + EPS) * gain + mod[3:4]).astype(BF16)
    acc = jnp.zeros(x1.shape, F32)
    for c in range(w1_ref.shape[1] // ff_chunk):
        sl = slice(c * ff_chunk, (c + 1) * ff_chunk)
        a = jnp.maximum(jnp.dot(hb, w1_ref[:, sl], preferred_element_type=F32), 0.0)
        acc = acc + jnp.dot((a * a).astype(BF16), w2_ref[sl, :], preferred_element_type=F32)
    x2 = x1 + mod[5:6] * acc
    if final:
        ms = jnp.mean(x2 * x2, axis=-1, keepdims=True)
        x2 = x2 * lax.rsqrt(ms + EPS) * fg_ref[...]
    o_ref[0] = x2


def _out_mlp(xa, yr, yg, mod, mod_index, layer, n2, wo, w1, w2, fg, tm, final):
    b, seq, d = xa.shape
    tok = lambda n: pl.BlockSpec((1, tm, n), lambda i, j: (i, j, 0))
    return pl.pallas_call(
        functools.partial(_mlp_kernel, final=final, ff_chunk=1024),
        out_shape=jax.ShapeDtypeStruct(xa.shape, F32),
        grid=(b, seq // tm),
        in_specs=[tok(d), tok(RET_W), tok(GLA_V), pl.BlockSpec((1, 6, d), lambda i, j: (mod_index(i), 0, 0)),
                  _layer_spec(n2.shape[1:], layer), _layer_spec(wo.shape[1:], layer),
                  _layer_spec(w1.shape[1:], layer), _layer_spec(w2.shape[1:], layer), _const_spec((1, d))],
        out_specs=tok(d),
        compiler_params=pltpu.CompilerParams(dimension_semantics=("parallel", "parallel"),
                                             vmem_limit_bytes=VMEM_LIMIT),
        name="out_mlp",
    )(xa, yr, yg, mod, n2, wo, w1, w2, fg)


def kernel(x, c, ctx, c_ctx, ada_w, ada_b, norm1_g, w_in, ret_decay, gla_gate_up, gla_gate_b, gla_norm_g,
           w_out, norm2_g, w_mlp1, w_mlp2, final_g):
    batch, seq, d = x.shape
    cl = ctx.shape[1]
    depth = ada_w.shape[0]
    assert w_in.shape[2] == IN_W
    mod_rows = 16
    cs = jnp.concatenate([c, c_ctx[None, :], jnp.zeros((mod_rows - batch - 1, d), c.dtype)], axis=0)
    mod = _modulation(cs, ada_w, ada_b).reshape(depth * mod_rows, 6, d)

    lat_tables = _rope_tables(seq)
    scale = RET_HEAD_DIM ** -0.5
    ones = jnp.ones((cl, RET_HEAD_DIM), F32)
    ctx_tables = (ones * scale, ones * 0.0, ones, ones * 0.0)

    w_in_b = w_in.astype(BF16)
    zero = jnp.zeros((depth, GLA_GATE_RANK, GLA_K), gla_gate_up.dtype)
    up = jnp.concatenate([jnp.concatenate([gla_gate_up[:, 0], zero], axis=2),
                          jnp.concatenate([zero, gla_gate_up[:, 1]], axis=2)], axis=1).astype(BF16)
    ub = gla_gate_b.reshape(depth, 1, 2 * GLA_K)
    wo = w_out.astype(BF16)
    w1 = w_mlp1.astype(BF16)
    w2 = w_mlp2.astype(BF16)
    n1 = norm1_g.reshape(depth, 1, d)
    n2 = norm2_g.reshape(depth, 1, d)
    gn = gla_norm_g.reshape(depth, 1, GLA_VAL_DIM)
    fg = final_g.reshape(1, d)
    rd = jnp.broadcast_to(jnp.swapaxes(ret_decay, 1, 2)[..., None], (depth, N_RET_HEADS, 2, RET_BLOCK))

    tm = math.gcd(seq, ROW_TILE)
    for layer in range(depth):
        last = layer == depth - 1
        lat_mod = lambda i, layer=layer: layer * mod_rows + i
        ctx_mod = lambda i, layer=layer: layer * mod_rows + batch
        lat = _project(x, mod, lat_mod, layer, n1, w_in_b, up, ub, lat_tables, tm)
        cx = _project(ctx, mod, ctx_mod, layer, n1, w_in_b, up, ub, ctx_tables, cl, states_only=last)
        rq, rk, rv, g, gq, gk, gv, ga = lat
        crq, crk, crv, cg, cgq, cgk, cgv, cga = cx
        yr, cyr = _retention(rd, layer, (rq, rk, rv, g), (crq, crk, crv, cg), not last)
        yg, cyg = _gla(gn, layer, (gq, gk, gv, g, ga), (cgq, cgk, cgv, cg, cga), not last)
        x = _out_mlp(x, yr, yg, mod, lat_mod, layer, n2, wo, w1, w2, fg, tm, last)
        if not last:
            ctx = _out_mlp(ctx, cyr, cyg, mod, ctx_mod, layer, n2, wo, w1, w2, fg, cl, False)
    return x
```

```python
import functools
import math

import jax
import jax.numpy as jnp
from jax import lax
from jax.experimental import pallas as pl
from jax.experimental.pallas import tpu as pltpu

F32 = jnp.float32
BF16 = jnp.bfloat16

GRID_W = 64
N_RET_HEADS = 4
RET_HEAD_DIM = 128
N_GLA_HEADS = 4
GLA_KEY_DIM = 64
GLA_VAL_DIM = 128
GLA_GATE_RANK = 16
GLA_GATE_NORM = 16.0
ROPE_BASE = 10000.0
EPS = 1e-6

RET_W = N_RET_HEADS * RET_HEAD_DIM
GLA_K = N_GLA_HEADS * GLA_KEY_DIM
GLA_V = N_GLA_HEADS * GLA_VAL_DIM
MIX_W = RET_W + GLA_V

OFF_RV = 2 * RET_W
OFF_RG = 3 * RET_W
OFF_GQ = 4 * RET_W
OFF_GK = OFF_GQ + GLA_K
OFF_GV = OFF_GK + GLA_K
OFF_GG = OFF_GV + GLA_V
OFF_D = OFF_GG + GLA_V
IN_W = OFF_D + 2 * GLA_GATE_RANK

RET_BLOCK = 256
GLA_BLOCK = 128
GLA_HALF = GLA_BLOCK // 2

ROW_TILE = 1024
LOG2E = 1.4426950408889634

VMEM_LIMIT = 56 * 1024 * 1024


def _layer_spec(shape, layer):
    zeros = (0,) * len(shape)
    return pl.BlockSpec((None,) + tuple(shape), lambda *_: (layer,) + zeros, pipeline_mode=pl.Buffered(1))


def _const_spec(shape):
    zeros = (0,) * len(shape)
    return pl.BlockSpec(shape, lambda *_: zeros, pipeline_mode=pl.Buffered(1))


def _mod_kernel(cs_ref, w_ref, b_ref, o_ref):
    cs = cs_ref[...]
    s = cs * jax.nn.sigmoid(cs)
    o_ref[0] = jnp.dot(s.astype(BF16), w_ref[0].astype(BF16), preferred_element_type=F32) + b_ref[0]


def _modulation(cs, ada_w, ada_b):
    depth, d, n = ada_w.shape
    rows = cs.shape[0]
    tn = 1024
    return pl.pallas_call(
        _mod_kernel,
        out_shape=jax.ShapeDtypeStruct((depth, rows, n), F32),
        grid=(depth, n // tn),
        in_specs=[
            pl.BlockSpec((rows, d), lambda l, j: (0, 0)),
            pl.BlockSpec((1, d, tn), lambda l, j: (l, 0, j)),
            pl.BlockSpec((1, 1, tn), lambda l, j: (l, 0, j)),
        ],
        out_specs=pl.BlockSpec((1, rows, tn), lambda l, j: (l, 0, j)),
        compiler_params=pltpu.CompilerParams(dimension_semantics=("parallel", "parallel"),
                                             vmem_limit_bytes=VMEM_LIMIT),
        name="adaln_modulation",
    )(cs, ada_w, ada_b.reshape(depth, 1, n))


_QUARTER = RET_HEAD_DIM // 4


def _rope_kernel(cq_ref, sq_ref, ck_ref, sk_ref):
    shape = (GRID_W, RET_HEAD_DIM)
    p = lax.broadcasted_iota(jnp.int32, shape, 0).astype(F32)
    lane = lax.broadcasted_iota(jnp.int32, shape, 1)
    freq = (lane & (_QUARTER - 1)).astype(F32)
    ang = p * jnp.exp(freq * (-math.log(ROPE_BASE) / _QUARTER))
    cos = jnp.cos(ang)
    sin = jnp.where((lane & (2 * _QUARTER - 1)) < _QUARTER, -jnp.sin(ang), jnp.sin(ang))
    by_row = lane < 2 * _QUARTER
    scale = RET_HEAD_DIM ** -0.5
    for r in range(cq_ref.shape[0] // GRID_W):
        c_blk = jnp.where(by_row, cos[r:r + 1, :], cos)
        s_blk = jnp.where(by_row, sin[r:r + 1, :], sin)
        rs = slice(r * GRID_W, (r + 1) * GRID_W)
        cq_ref[rs, :] = c_blk * scale
        sq_ref[rs, :] = s_blk * scale
        ck_ref[rs, :] = c_blk
        sk_ref[rs, :] = s_blk


def _rope_tables(seq):
    sds = jax.ShapeDtypeStruct((seq, RET_HEAD_DIM), F32)
    return pl.pallas_call(_rope_kernel, out_shape=(sds, sds, sds, sds), name="rope_tables")()


def _silu(z):
    hz = 0.5 * z
    return hz * (1.0 + jnp.tanh(hz))


def _proj_kernel(x_ref, mod_ref, n1_ref, w_ref, wd_ref, up_ref, ub_ref, cq_ref, sq_ref, ck_ref, sk_ref,
                 rq_ref, rk_ref, rv_ref, g_ref, gq_ref, gk_ref, gv_ref, ga_ref, *, states_only):
    x = x_ref[0]
    mod = mod_ref[0]
    gain = n1_ref[...] * (1.0 + mod[1:2])
    ms = jnp.mean(x * x, axis=-1, keepdims=True)
    hb = (x * lax.rsqrt(ms + EPS) * gain + mod[0:1]).astype(BF16)

    def mm(off, n):
        return jnp.dot(hb, w_ref[:, off:off + n], preferred_element_type=F32)

    lane = lax.broadcasted_iota(jnp.int32, (x.shape[0], RET_HEAD_DIM), 1)
    first = (lane & (2 * _QUARTER - 1)) < _QUARTER

    def rotary(z, c_ref, s_ref, o_ref):
        c = c_ref[...]
        s = s_ref[...]
        for hd in range(N_RET_HEADS):
            sl = slice(hd * RET_HEAD_DIM, (hd + 1) * RET_HEAD_DIM)
            zh = z[:, sl]
            partner = jnp.where(first, pltpu.roll(zh, RET_HEAD_DIM - _QUARTER, axis=1),
                                pltpu.roll(zh, _QUARTER, axis=1))
            o_ref[0, :, sl] = (zh * c + partner * s).astype(BF16)

    d = jnp.dot(hb, wd_ref[...], preferred_element_type=F32).astype(BF16)
    zg = jnp.dot(d, up_ref[...], preferred_element_type=F32) + ub_ref[...]
    soft = jnp.log2(1.0 + jnp.exp2(jnp.abs(zg) * -LOG2E))
    ga_ref[0] = jnp.minimum(zg, 0.0) * (LOG2E / GLA_GATE_NORM) - soft * (1.0 / GLA_GATE_NORM)
    rv_ref[0] = mm(OFF_RV, RET_W).astype(BF16)
    gk_ref[0] = mm(OFF_GK, GLA_K).astype(BF16)
    rotary(mm(RET_W, RET_W), ck_ref, sk_ref, rk_ref)
    gv_ref[0] = mm(OFF_GV, GLA_V).astype(BF16)
    if states_only:
        for ref in (rq_ref, g_ref, gq_ref):
            ref[...] = jnp.zeros(ref.shape, ref.dtype)
        return
    rotary(mm(0, RET_W), cq_ref, sq_ref, rq_ref)
    g_ref[0, :, 0:RET_W] = _silu(mm(OFF_RG, RET_W)).astype(BF16)
    gq_ref[0] = (mm(OFF_GQ, GLA_K) * (GLA_KEY_DIM ** -0.5)).astype(BF16)
    g_ref[0, :, RET_W:MIX_W] = _silu(mm(OFF_GG, GLA_V)).astype(BF16)


def _project(xa, mod, mod_index, layer, n1, w, wd, up, ub, tables, tm, states_only=False):
    b, seq, d = xa.shape
    tok = lambda n: pl.BlockSpec((1, tm, n), lambda i, j: (i, j, 0))
    tab = pl.BlockSpec((tm, RET_HEAD_DIM), lambda i, j: (j, 0))
    sd = lambda n, dt: jax.ShapeDtypeStruct((b, seq, n), dt)
    return pl.pallas_call(
        functools.partial(_proj_kernel, states_only=states_only),
        out_shape=(sd(RET_W, BF16), sd(RET_W, BF16), sd(RET_W, BF16), sd(MIX_W, BF16),
                   sd(GLA_K, BF16), sd(GLA_K, BF16), sd(GLA_V, BF16), sd(2 * GLA_K, F32)),
        grid=(b, seq // tm),
        in_specs=[tok(d), pl.BlockSpec((1, 6, d), lambda i, j: (mod_index(i), 0, 0)),
                  _layer_spec(n1.shape[1:], layer),
                  _layer_spec(w.shape[1:], layer), _layer_spec(wd.shape[1:], layer),
                  _layer_spec(up.shape[1:], layer),
                  _layer_spec(ub.shape[1:], layer), tab, tab, tab, tab],
        out_specs=(tok(RET_W), tok(RET_W), tok(RET_W), tok(MIX_W), tok(GLA_K), tok(GLA_K), tok(GLA_V),
                   tok(2 * GLA_K)),
        compiler_params=pltpu.CompilerParams(dimension_semantics=("parallel", "parallel"),
                                             vmem_limit_bytes=VMEM_LIMIT),
        name="norm_project",
    )(xa, mod, n1, w, wd, up, ub, *tables)


def _ret_kernel(rd_ref, q_ref, k_ref, v_ref, g_ref, cq_ref, ck_ref, cv_ref, cg_ref,
                y_ref, cy_ref, kv_ref, st_ref, sc_ref, cmask_ref, cdec_ref, *, need_ctx):
    t_blk = RET_BLOCK
    dh = RET_HEAD_DIM
    head = pl.program_id(1)

    @pl.when(pl.program_id(0) == 0)
    def _():
        lg = jnp.log1p(-jnp.exp(rd_ref[0]))
        lgf, lgb = lg[0:1, :], lg[1:2, :]
        lgf_h, lgb_h = lgf[:, :dh], lgb[:, :dh]
        ti = lax.broadcasted_iota(jnp.int32, (t_blk, t_blk), 0)
        si = lax.broadcasted_iota(jnp.int32, (t_blk, t_blk), 1)
        diff = (ti - si).astype(F32)
        cmask_ref[head] = jnp.exp(jnp.where(diff >= 0, diff * lgf, -diff * lgb))
        tr = lax.broadcasted_iota(jnp.int32, (t_blk, dh), 0).astype(F32)
        cdec_ref[head, 0:t_blk, 0:dh] = jnp.exp((tr + 1.0) * lgf_h)
        cdec_ref[head, 0:t_blk, dh:2 * dh] = jnp.exp((t_blk - tr) * lgb_h)
        cdec_ref[head, 0:t_blk, 2 * dh:3 * dh] = jnp.exp((t_blk - 1.0 - tr) * lgf_h)
        cdec_ref[head, 0:t_blk, 3 * dh:4 * dh] = jnp.exp(tr * lgb_h)
        cdec_ref[head, t_blk:t_blk + 8, 0:dh] = jnp.broadcast_to(jnp.exp(t_blk * lgf_h), (8, dh))
        cdec_ref[head, t_blk:t_blk + 8, dh:2 * dh] = jnp.broadcast_to(jnp.exp(t_blk * lgb_h), (8, dh))

    mask = cmask_ref[head]
    q_dec = cdec_ref[head, 0:t_blk, 0:2 * dh]
    k_dec_f = cdec_ref[head, 0:t_blk, 2 * dh:3 * dh]
    k_dec_b = cdec_ref[head, 0:t_blk, 3 * dh:4 * dh]
    blk_f = cdec_ref[head, t_blk:t_blk + 1, 0:dh]
    blk_b = cdec_ref[head, t_blk:t_blk + 1, dh:2 * dh]
    tn_dims = (((0,), (0,)), ((), ()))
    nt_dims = (((1,), (1,)), ((), ()))
    n_ctx = cq_ref.shape[1] // t_blk
    n_lat = q_ref.shape[1] // t_blk
    n_tot = n_ctx + n_lat
    fwd = slice(0, dh)
    bwd = slice(dh, 2 * dh)

    def rows(i):
        return pl.ds(i * t_blk, t_blk)

    def kv_block(kr, vr, i, j):
        r = rows(i)
        v = vr[0, r, :].astype(F32)
        vv = jnp.concatenate([(v * k_dec_f).astype(BF16), (v * k_dec_b).astype(BF16)], axis=1)
        kv_ref[j] = lax.dot_general(kr[0, r, :], vv, tn_dims, preferred_element_type=F32)

    def score_block(qr, kr, i, j):
        r = rows(i)
        a = lax.dot_general(qr[0, r, :], kr[0, r, :], nt_dims, preferred_element_type=F32) * mask
        sc_ref[j] = a.astype(BF16)

    def out_block(qr, vr, gr, yr, i, j):
        r = rows(i)
        q, v = qr[0, r, :], vr[0, r, :]
        o = jnp.dot(sc_ref[j], v, preferred_element_type=F32)
        inter = jnp.dot(q, st_ref[j], preferred_element_type=F32) * q_dec
        o = o + inter[:, fwd] + inter[:, bwd]
        mu = jnp.mean(o, axis=-1, keepdims=True)
        dlt = o - mu
        var = jnp.mean(dlt * dlt, axis=-1, keepdims=True)
        yr[0, r, :] = (dlt * lax.rsqrt(var + EPS) * gr[0, r, :].astype(F32)).astype(BF16)

    def for_blocks(n, fn):
        for i in range(n):
            fn(i)

    for_blocks(n_ctx, lambda i: kv_block(ck_ref, cv_ref, i, i))
    for_blocks(n_lat, lambda i: kv_block(k_ref, v_ref, i, n_ctx + i))

    def step_f(j, s):
        st_ref[j, :, fwd] = s.astype(BF16)
        return blk_f * s + kv_ref[j, :, fwd]

    def step_b(j, s):
        st_ref[j, :, bwd] = s.astype(BF16)
        return blk_b * s + kv_ref[j, :, bwd]

    zero = jnp.zeros((dh, dh), F32)
    lax.fori_loop(0, n_tot, step_f, zero)
    s_b = lax.fori_loop(0, n_ctx, lambda t, s: step_b(n_ctx - 1 - t, s), zero)
    lax.fori_loop(0, n_lat, lambda t, s: step_b(n_tot - 1 - t, s), s_b)

    if need_ctx:
        for_blocks(n_ctx, lambda i: score_block(cq_ref, ck_ref, i, i))
    for_blocks(n_lat, lambda i: score_block(q_ref, k_ref, i, n_ctx + i))
    if need_ctx:
        for_blocks(n_ctx, lambda i: out_block(cq_ref, cv_ref, cg_ref, cy_ref, i, i))
    else:
        cy_ref[...] = jnp.zeros(cy_ref.shape, cy_ref.dtype)
    for_blocks(n_lat, lambda i: out_block(q_ref, v_ref, g_ref, y_ref, i, n_ctx + i))


def _retention(rd, layer, lat, ctx, need_ctx):
    rq, rk, rv, g = lat
    crq, crk, crv, cg = ctx
    b, seq, _ = rq.shape
    cl = crq.shape[1]
    dh = RET_HEAD_DIM
    n_blocks = (seq + cl) // RET_BLOCK
    head = lambda n: pl.BlockSpec((1, n, dh), lambda i, h: (i, 0, h))
    return pl.pallas_call(
        functools.partial(_ret_kernel, need_ctx=need_ctx),
        out_shape=(jax.ShapeDtypeStruct((b, seq, RET_W), BF16), jax.ShapeDtypeStruct((b, cl, RET_W), BF16)),
        grid=(b, N_RET_HEADS),
        in_specs=[pl.BlockSpec((None, 1, 2, RET_BLOCK), lambda i, h: (layer, h, 0, 0)),
                  head(seq), head(seq), head(seq), head(seq), head(cl), head(cl), head(cl), head(cl)],
        out_specs=(head(seq), head(cl)),
        scratch_shapes=[pltpu.VMEM((n_blocks, dh, 2 * dh), F32), pltpu.VMEM((n_blocks, dh, 2 * dh), BF16),
                        pltpu.VMEM((n_blocks, RET_BLOCK, RET_BLOCK), BF16),
                        pltpu.VMEM((N_RET_HEADS, RET_BLOCK, RET_BLOCK), F32),
                        pltpu.VMEM((N_RET_HEADS, RET_BLOCK + 8, 4 * dh), F32)],
        compiler_params=pltpu.CompilerParams(dimension_semantics=("arbitrary", "arbitrary"),
                                             vmem_limit_bytes=VMEM_LIMIT),
        name="retention_scan",
    )(rd, rq, rk, rv, g, crq, crk, crv, cg)


def _gla_kernel(gn_ref, q_ref, k_ref, v_ref, g_ref, af_ref, ab_ref, cq_ref, ck_ref, cv_ref, cg_ref,
                caf_ref, cab_ref, y_ref, cy_ref, c_ref, kv_ref, dec_ref, st_ref, sc_ref, qin_ref, *, need_ctx):
    t_blk = GLA_BLOCK
    dv = GLA_VAL_DIM
    dk2 = 2 * GLA_KEY_DIM
    assert t_blk == dk2 == dv
    ti = lax.broadcasted_iota(jnp.int32, (t_blk, t_blk), 0)
    si = lax.broadcasted_iota(jnp.int32, (t_blk, t_blk), 1)
    lower = si <= ti
    cum_f = lower.astype(BF16)
    head0 = si < GLA_KEY_DIM
    tn_dims = (((0,), (0,)), ((), ()))
    nt_dims = (((1,), (1,)), ((), ()))

    def rows(i):
        return pl.ds(i * t_blk, t_blk)

    def split2(a):
        hi = a.astype(BF16)
        return [hi, (a - hi.astype(F32)).astype(BF16)]

    dirs = ((GLA_HALF - 1, t_blk - 1), (GLA_HALF, 0))

    def cum_block(ars, i, row0):
        af = ars[0][0, rows(i), :]
        ab = ars[1][0, rows(i), :]
        z = jnp.dot(cum_f, jnp.concatenate(split2(af) + split2(ab), axis=1), preferred_element_type=F32)
        c_ref[0, rows(row0 + i), :] = z[:, 0:dk2] + z[:, dk2:2 * dk2]
        pb = z[:, 2 * dk2:3 * dk2] + z[:, 3 * dk2:4 * dk2]
        c_ref[1, rows(row0 + i), :] = pb[t_blk - 1:t_blk, :] - pb + ab

    def mid_block(qr, kr, vr, i, j, row0, outputs=True):
        r = rows(i)
        cr = rows(row0 + i)
        q = qr[0, r, :].astype(F32)
        k = kr[0, r, :].astype(F32)
        v = vr[0, r, :]
        sc, k_end = [], []
        for d, (mid, edge) in enumerate(dirs):
            c = c_ref[d, cr, :]
            c_mid = c[mid:mid + 1, :]
            c_edge = c[edge:edge + 1, :]
            k_mid = k * jnp.exp2(c_mid - c)
            k_end.append((k_mid * jnp.exp2(c_edge - c_mid)).astype(BF16))
            dec_ref[d, j] = jnp.broadcast_to(jnp.exp2(c_edge), (8, dk2))
            if not outputs:
                continue
            q_mid = q * jnp.exp2(c - c_mid)
            qin_ref[d, cr, :] = (q_mid * jnp.exp2(c_mid)).astype(BF16)
            k_mid_b = k_mid.astype(BF16)
            zero = jnp.zeros_like(k_mid_b)
            k_heads = jnp.concatenate([jnp.where(head0, k_mid_b, zero), jnp.where(head0, zero, k_mid_b)], axis=0)
            sc.append(lax.dot_general(q_mid.astype(BF16), k_heads, nt_dims, preferred_element_type=F32))
        kv = lax.dot_general(v, jnp.concatenate(k_end, axis=1), tn_dims, preferred_element_type=F32)
        for d in range(2):
            kvd = kv[:, d * dk2:(d + 1) * dk2]
            kv_ref[d, j] = jnp.where(head0, kvd[0:dv, :], kvd[dv:2 * dv, :])
        for hd in range(2 if outputs else 0):
            hs = slice(hd * t_blk, (hd + 1) * t_blk)
            sc_ref[hd, j] = jnp.where(lower, sc[0][:, hs], sc[1][:, hs]).astype(BF16)

    def out_block(qr, vr, gr, yr, i, j, row0):
        r = rows(i)
        cr = rows(row0 + i)
        v = vr[0, r, :]
        s_exp = []
        for d in range(2):
            s = st_ref[d, j]
            zero = jnp.zeros_like(s)
            s_exp.append(jnp.concatenate([jnp.where(head0, s, zero), jnp.where(head0, zero, s)], axis=0))
        q_in = jnp.concatenate([qin_ref[0, cr, :], qin_ref[1, cr, :]], axis=1)
        o = jnp.concatenate([jnp.dot(sc_ref[hd, j], v[:, hd * dv:(hd + 1) * dv], preferred_element_type=F32)
                             for hd in range(2)], axis=1)
        o = o + lax.dot_general(q_in, jnp.concatenate(s_exp, axis=1), nt_dims, preferred_element_type=F32)
        gate = gr[0, r, :].astype(F32)
        gn = gn_ref[...]
        for hd in range(2):
            sl = slice(hd * dv, (hd + 1) * dv)
            oh = o[:, sl]
            ms = jnp.mean(oh * oh, axis=-1, keepdims=True)
            yr[0, r, sl] = (oh * lax.rsqrt(ms + EPS) * gn * gate[:, sl]).astype(BF16)

    def for_blocks(n, fn):
        for i in range(n):
            fn(i)

    n_ctx = cq_ref.shape[1] // t_blk
    n_lat = q_ref.shape[1] // t_blk
    n_tot = n_ctx + n_lat
    for_blocks(n_ctx, lambda i: cum_block((caf_ref, cab_ref), i, 0))
    for_blocks(n_lat, lambda i: cum_block((af_ref, ab_ref), i, n_ctx))
    for_blocks(n_ctx, lambda i: mid_block(cq_ref, ck_ref, cv_ref, i, i, 0, outputs=need_ctx))
    for_blocks(n_lat, lambda i: mid_block(q_ref, k_ref, v_ref, i, n_ctx + i, n_ctx))

    def step(d, j, s):
        st_ref[d, j] = s.astype(BF16)
        return dec_ref[d, j, 0:1, :] * s + kv_ref[d, j]

    zero = jnp.zeros((dv, dk2), F32)
    lax.fori_loop(0, n_tot, lambda j, s: step(0, j, s), zero)
    s_b = lax.fori_loop(0, n_ctx, lambda t, s: step(1, n_ctx - 1 - t, s), zero)
    lax.fori_loop(0, n_lat, lambda t, s: step(1, n_tot - 1 - t, s), s_b)

    if need_ctx:
        for_blocks(n_ctx, lambda i: out_block(cq_ref, cv_ref, cg_ref, cy_ref, i, i, 0))
    else:
        cy_ref[...] = jnp.zeros(cy_ref.shape, cy_ref.dtype)
    for_blocks(n_lat, lambda i: out_block(q_ref, v_ref, g_ref, y_ref, i, n_ctx + i, n_ctx))


def _gla(gn, layer, lat, ctx, need_ctx):
    gq, gk, gv, g, ga = lat
    cgq, cgk, cgv, cg, cga = ctx
    b, seq, _ = gq.shape
    cl = cgq.shape[1]
    pair_k = 2 * GLA_KEY_DIM
    pair_v = 2 * GLA_VAL_DIM
    n_pairs = N_GLA_HEADS // 2
    n_blocks = (seq + cl) // GLA_BLOCK
    gate_off = RET_W // pair_v
    key = lambda n: pl.BlockSpec((1, n, pair_k), lambda i, p: (i, 0, p))
    key_b = lambda n: pl.BlockSpec((1, n, pair_k), lambda i, p: (i, 0, n_pairs + p))
    val = lambda n: pl.BlockSpec((1, n, pair_v), lambda i, p: (i, 0, p))
    mix = lambda n: pl.BlockSpec((1, n, pair_v), lambda i, p: (i, 0, gate_off + p))
    return pl.pallas_call(
        functools.partial(_gla_kernel, need_ctx=need_ctx),
        out_shape=(jax.ShapeDtypeStruct((b, seq, GLA_V), BF16), jax.ShapeDtypeStruct((b, cl, GLA_V), BF16)),
        grid=(b, n_pairs),
        in_specs=[_layer_spec(gn.shape[1:], layer),
                  key(seq), key(seq), val(seq), mix(seq), key(seq), key_b(seq),
                  key(cl), key(cl), val(cl), mix(cl), key(cl), key_b(cl)],
        out_specs=(val(seq), val(cl)),
        scratch_shapes=[pltpu.VMEM((2, seq + cl, pair_k), F32),
                        pltpu.VMEM((2, n_blocks, GLA_VAL_DIM, pair_k), F32),
                        pltpu.VMEM((2, n_blocks, 8, pair_k), F32),
                        pltpu.VMEM((2, n_blocks, GLA_VAL_DIM, pair_k), BF16),
                        pltpu.VMEM((2, n_blocks, GLA_BLOCK, GLA_BLOCK), BF16),
                        pltpu.VMEM((2, seq + cl, pair_k), BF16)],
        compiler_params=pltpu.CompilerParams(dimension_semantics=("parallel", "parallel"),
                                             vmem_limit_bytes=VMEM_LIMIT),
        name="gla_scan",
    )(gn, gq, gk, gv, g, ga, ga, cgq, cgk, cgv, cg, cga, cga)


def _mlp_kernel(x_ref, yr_ref, yg_ref, mod_ref, n2_ref, wo_ref, w1_ref, w2_ref, fg_ref, o_ref, *,
                final, ff_chunk):
    mod = mod_ref[0]
    mix = (jnp.dot(yr_ref[0], wo_ref[0:RET_W, :], preferred_element_type=F32)
           + jnp.dot(yg_ref[0], wo_ref[RET_W:MIX_W, :], preferred_element_type=F32))
    x1 = x_ref[0] + mod[2:3] * mix
    gain = n2_ref[...] * (1.0 + mod[4:5])
    ms = jnp.mean(x1 * x1, axis=-1, keepdims=True)
    hb = (x1 * lax.rsqrt(ms + EPS) * gain + mod[3:4]).astype(BF16)
    acc = jnp.zeros(x1.shape, F32)
    for c in range(w1_ref.shape[1] // ff_chunk):
        sl = slice(c * ff_chunk, (c + 1) * ff_chunk)
        a = jnp.maximum(jnp.dot(hb, w1_ref[:, sl], preferred_element_type=F32), 0.0)
        acc = acc + jnp.dot((a * a).astype(BF16), w2_ref[sl, :], preferred_element_type=F32)
    x2 = x1 + mod[5:6] * acc
    if final:
        ms = jnp.mean(x2 * x2, axis=-1, keepdims=True)
        x2 = x2 * lax.rsqrt(ms + EPS) * fg_ref[...]
    o_ref[0] = x2


def _out_mlp(xa, yr, yg, mod, mod_index, layer, n2, wo, w1, w2, fg, tm, final):
    b, seq, d = xa.shape
    tok = lambda n: pl.BlockSpec((1, tm, n), lambda i, j: (i, j, 0))
    return pl.pallas_call(
        functools.partial(_mlp_kernel, final=final, ff_chunk=1024),
        out_shape=jax.ShapeDtypeStruct(xa.shape, F32),
        grid=(b, seq // tm),
        in_specs=[tok(d), tok(RET_W), tok(GLA_V), pl.BlockSpec((1, 6, d), lambda i, j: (mod_index(i), 0, 0)),
                  _layer_spec(n2.shape[1:], layer), _layer_spec(wo.shape[1:], layer),
                  _layer_spec(w1.shape[1:], layer), _layer_spec(w2.shape[1:], layer), _const_spec((1, d))],
        out_specs=tok(d),
        compiler_params=pltpu.CompilerParams(dimension_semantics=("parallel", "parallel"),
                                             vmem_limit_bytes=VMEM_LIMIT),
        name="out_mlp",
    )(xa, yr, yg, mod, n2, wo, w1, w2, fg)


def kernel(x, c, ctx, c_ctx, ada_w, ada_b, norm1_g, w_in, ret_decay, gla_gate_up, gla_gate_b, gla_norm_g,
           w_out, norm2_g, w_mlp1, w_mlp2, final_g):
    batch, seq, d = x.shape
    cl = ctx.shape[1]
    depth = ada_w.shape[0]
    assert w_in.shape[2] == IN_W
    mod_rows = 16
    cs = jnp.concatenate([c, c_ctx[None, :], jnp.zeros((mod_rows - batch - 1, d), c.dtype)], axis=0)
    mod = _modulation(cs, ada_w, ada_b).reshape(depth * mod_rows, 6, d)

    lat_tables = _rope_tables(seq)
    scale = RET_HEAD_DIM ** -0.5
    ones = jnp.ones((cl, RET_HEAD_DIM), F32)
    ctx_tables = (ones * scale, ones * 0.0, ones, ones * 0.0)

    w_in_b = w_in[:, :, :OFF_D].astype(BF16)
    w_d = w_in[:, :, OFF_D:].astype(BF16)
    zero = jnp.zeros((depth, GLA_GATE_RANK, GLA_K), gla_gate_up.dtype)
    up = jnp.concatenate([jnp.concatenate([gla_gate_up[:, 0], zero], axis=2),
                          jnp.concatenate([zero, gla_gate_up[:, 1]], axis=2)], axis=1).astype(BF16)
    ub = gla_gate_b.reshape(depth, 1, 2 * GLA_K)
    wo = w_out.astype(BF16)
    w1 = w_mlp1.astype(BF16)
    w2 = w_mlp2.astype(BF16)
    n1 = norm1_g.reshape(depth, 1, d)
    n2 = norm2_g.reshape(depth, 1, d)
    gn = gla_norm_g.reshape(depth, 1, GLA_VAL_DIM)
    fg = final_g.reshape(1, d)
    rd = jnp.broadcast_to(jnp.swapaxes(ret_decay, 1, 2)[..., None], (depth, N_RET_HEADS, 2, RET_BLOCK))

    tm = math.gcd(seq, ROW_TILE)
    for layer in range(depth):
        last = layer == depth - 1
        lat_mod = lambda i, layer=layer: layer * mod_rows + i
        ctx_mod = lambda i, layer=layer: layer * mod_rows + batch
        lat = _project(x, mod, lat_mod, layer, n1, w_in_b, w_d, up, ub, lat_tables, tm)
        cx = _project(ctx, mod, ctx_mod, layer, n1, w_in_b, w_d, up, ub, ctx_tables, cl, states_only=last)
        rq, rk, rv, g, gq, gk, gv, ga = lat
        crq, crk, crv, cg, cgq, cgk, cgv, cga = cx
        yr, cyr = _retention(rd, layer, (rq, rk, rv, g), (crq, crk, crv, cg), not last)
        yg, cyg = _gla(gn, layer, (gq, gk, gv, g, ga), (cgq, cgk, cgv, cg, cga), not last)
        x = _out_mlp(x, yr, yg, mod, lat_mod, layer, n2, wo, w1, w2, fg, tm, last)
        if not last:
            ctx = _out_mlp(ctx, cyr, cyg, mod, ctx_mod, layer, n2, wo, w1, w2, fg, cl, False)
    return x
```

```python
import functools
import math

import jax
import jax.numpy as jnp
from jax import lax
from jax.experimental import pallas as pl
from jax.experimental.pallas import tpu as pltpu

F32 = jnp.float32
BF16 = jnp.bfloat16

GRID_W = 64
N_RET_HEADS = 4
RET_HEAD_DIM = 128
N_GLA_HEADS = 4
GLA_KEY_DIM = 64
GLA_VAL_DIM = 128
GLA_GATE_RANK = 16
GLA_GATE_NORM = 16.0
ROPE_BASE = 10000.0
EPS = 1e-6

RET_W = N_RET_HEADS * RET_HEAD_DIM
GLA_K = N_GLA_HEADS * GLA_KEY_DIM
GLA_V = N_GLA_HEADS * GLA_VAL_DIM
MIX_W = RET_W + GLA_V

OFF_RV = 2 * RET_W
OFF_RG = 3 * RET_W
OFF_GQ = 4 * RET_W
OFF_GK = OFF_GQ + GLA_K
OFF_GV = OFF_GK + GLA_K
OFF_GG = OFF_GV + GLA_V
OFF_D = OFF_GG + GLA_V
IN_W = OFF_D + 2 * GLA_GATE_RANK

RET_BLOCK = 256
GLA_BLOCK = 128
GLA_HALF = GLA_BLOCK // 2

ROW_TILE = 1024
LOG2E = 1.4426950408889634

VMEM_LIMIT = 56 * 1024 * 1024


def _layer_spec(shape, layer):
    zeros = (0,) * len(shape)
    return pl.BlockSpec((None,) + tuple(shape), lambda *_: (layer,) + zeros, pipeline_mode=pl.Buffered(1))


def _const_spec(shape):
    zeros = (0,) * len(shape)
    return pl.BlockSpec(shape, lambda *_: zeros, pipeline_mode=pl.Buffered(1))


def _mod_kernel(cs_ref, w_ref, b_ref, o_ref):
    cs = cs_ref[...]
    s = cs * jax.nn.sigmoid(cs)
    o_ref[0] = jnp.dot(s.astype(BF16), w_ref[0].astype(BF16), preferred_element_type=F32) + b_ref[0]


def _modulation(cs, ada_w, ada_b):
    depth, d, n = ada_w.shape
    rows = cs.shape[0]
    tn = 1024
    return pl.pallas_call(
        _mod_kernel,
        out_shape=jax.ShapeDtypeStruct((depth, rows, n), F32),
        grid=(depth, n // tn),
        in_specs=[
            pl.BlockSpec((rows, d), lambda l, j: (0, 0)),
            pl.BlockSpec((1, d, tn), lambda l, j: (l, 0, j)),
            pl.BlockSpec((1, 1, tn), lambda l, j: (l, 0, j)),
        ],
        out_specs=pl.BlockSpec((1, rows, tn), lambda l, j: (l, 0, j)),
        compiler_params=pltpu.CompilerParams(dimension_semantics=("parallel", "parallel"),
                                             vmem_limit_bytes=VMEM_LIMIT),
        name="adaln_modulation",
    )(cs, ada_w, ada_b.reshape(depth, 1, n))


_QUARTER = RET_HEAD_DIM // 4


def _rope_kernel(cq_ref, sq_ref, ck_ref, sk_ref):
    shape = (GRID_W, RET_HEAD_DIM)
    p = lax.broadcasted_iota(jnp.int32, shape, 0).astype(F32)
    lane = lax.broadcasted_iota(jnp.int32, shape, 1)
    freq = (lane & (_QUARTER - 1)).astype(F32)
    ang = p * jnp.exp(freq * (-math.log(ROPE_BASE) / _QUARTER))
    cos = jnp.cos(ang)
    sin = jnp.where((lane & (2 * _QUARTER - 1)) < _QUARTER, -jnp.sin(ang), jnp.sin(ang))
    by_row = lane < 2 * _QUARTER
    scale = RET_HEAD_DIM ** -0.5
    for r in range(cq_ref.shape[0] // GRID_W):
        c_blk = jnp.where(by_row, cos[r:r + 1, :], cos)
        s_blk = jnp.where(by_row, sin[r:r + 1, :], sin)
        rs = slice(r * GRID_W, (r + 1) * GRID_W)
        cq_ref[rs, :] = c_blk * scale
        sq_ref[rs, :] = s_blk * scale
        ck_ref[rs, :] = c_blk
        sk_ref[rs, :] = s_blk


def _rope_tables(seq):
    sds = jax.ShapeDtypeStruct((seq, RET_HEAD_DIM), F32)
    return pl.pallas_call(_rope_kernel, out_shape=(sds, sds, sds, sds), name="rope_tables")()


def _silu(z):
    hz = 0.5 * z
    return hz * (1.0 + jnp.tanh(hz))


def _proj_kernel(x_ref, mod_ref, n1_ref, w_ref, up_ref, ub_ref, cq_ref, sq_ref, ck_ref, sk_ref,
                 rq_ref, rk_ref, rv_ref, g_ref, gq_ref, gk_ref, gv_ref, ga_ref, *, states_only):
    x = x_ref[0]
    mod = mod_ref[0]
    gain = n1_ref[...] * (1.0 + mod[1:2])
    ms = jnp.mean(x * x, axis=-1, keepdims=True)
    hb = (x * lax.rsqrt(ms + EPS) * gain + mod[0:1]).astype(BF16)

    def mm(off, n):
        return jnp.dot(hb, w_ref[:, off:off + n], preferred_element_type=F32)

    lane = lax.broadcasted_iota(jnp.int32, (x.shape[0], RET_HEAD_DIM), 1)
    first = (lane & (2 * _QUARTER - 1)) < _QUARTER

    def rotary(z, c_ref, s_ref, o_ref):
        c = c_ref[...]
        s = s_ref[...]
        for hd in range(N_RET_HEADS):
            sl = slice(hd * RET_HEAD_DIM, (hd + 1) * RET_HEAD_DIM)
            zh = z[:, sl]
            partner = jnp.where(first, pltpu.roll(zh, RET_HEAD_DIM - _QUARTER, axis=1),
                                pltpu.roll(zh, _QUARTER, axis=1))
            o_ref[0, :, sl] = (zh * c + partner * s).astype(BF16)

    d = mm(OFF_D, 2 * GLA_GATE_RANK).astype(BF16)
    zg = jnp.dot(d, up_ref[...], preferred_element_type=F32) + ub_ref[...]
    soft = jnp.log2(1.0 + jnp.exp2(jnp.abs(zg) * -LOG2E))
    ga_ref[0] = jnp.minimum(zg, 0.0) * (LOG2E / GLA_GATE_NORM) - soft * (1.0 / GLA_GATE_NORM)
    rv_ref[0] = mm(OFF_RV, RET_W).astype(BF16)
    gk_ref[0] = mm(OFF_GK, GLA_K).astype(BF16)
    rotary(mm(RET_W, RET_W), ck_ref, sk_ref, rk_ref)
    gv_ref[0] = mm(OFF_GV, GLA_V).astype(BF16)
    if states_only:
        for ref in (rq_ref, g_ref, gq_ref):
            ref[...] = jnp.zeros(ref.shape, ref.dtype)
        return
    rotary(mm(0, RET_W), cq_ref, sq_ref, rq_ref)
    g_ref[0, :, 0:RET_W] = _silu(mm(OFF_RG, RET_W)).astype(BF16)
    gq_ref[0] = (mm(OFF_GQ, GLA_K) * (GLA_KEY_DIM ** -0.5)).astype(BF16)
    g_ref[0, :, RET_W:MIX_W] = _silu(mm(OFF_GG, GLA_V)).astype(BF16)


def _project(xa, mod, mod_index, layer, n1, w, up, ub, tables, tm, states_only=False):
    b, seq, d = xa.shape
    tok = lambda n: pl.BlockSpec((1, tm, n), lambda i, j: (i, j, 0))
    tab = pl.BlockSpec((tm, RET_HEAD_DIM), lambda i, j: (j, 0))
    sd = lambda n, dt: jax.ShapeDtypeStruct((b, seq, n), dt)
    return pl.pallas_call(
        functools.partial(_proj_kernel, states_only=states_only),
        out_shape=(sd(RET_W, BF16), sd(RET_W, BF16), sd(RET_W, BF16), sd(MIX_W, BF16),
                   sd(GLA_K, BF16), sd(GLA_K, BF16), sd(GLA_V, BF16), sd(2 * GLA_K, F32)),
        grid=(b, seq // tm),
        in_specs=[tok(d), pl.BlockSpec((1, 6, d), lambda i, j: (mod_index(i), 0, 0)),
                  _layer_spec(n1.shape[1:], layer),
                  _layer_spec(w.shape[1:], layer), _layer_spec(up.shape[1:], layer),
                  _layer_spec(ub.shape[1:], layer), tab, tab, tab, tab],
        out_specs=(tok(RET_W), tok(RET_W), tok(RET_W), tok(MIX_W), tok(GLA_K), tok(GLA_K), tok(GLA_V),
                   tok(2 * GLA_K)),
        compiler_params=pltpu.CompilerParams(dimension_semantics=("parallel", "parallel"),
                                             vmem_limit_bytes=VMEM_LIMIT),
        name="norm_project",
    )(xa, mod, n1, w, up, ub, *tables)


def _ret_kernel(rd_ref, q_ref, k_ref, v_ref, g_ref, cq_ref, ck_ref, cv_ref, cg_ref,
                y_ref, cy_ref, kv_ref, st_ref, sc_ref, cmask_ref, cdec_ref, *, need_ctx):
    t_blk = RET_BLOCK
    dh = RET_HEAD_DIM
    head = pl.program_id(1)

    @pl.when(pl.program_id(0) == 0)
    def _():
        lg = jnp.log1p(-jnp.exp(rd_ref[0]))
        lgf, lgb = lg[0:1, :], lg[1:2, :]
        lgf_h, lgb_h = lgf[:, :dh], lgb[:, :dh]
        ti = lax.broadcasted_iota(jnp.int32, (t_blk, t_blk), 0)
        si = lax.broadcasted_iota(jnp.int32, (t_blk, t_blk), 1)
        diff = (ti - si).astype(F32)
        cmask_ref[head] = jnp.exp(jnp.where(diff >= 0, diff * lgf, -diff * lgb))
        tr = lax.broadcasted_iota(jnp.int32, (t_blk, dh), 0).astype(F32)
        cdec_ref[head, 0:t_blk, 0:dh] = jnp.exp((tr + 1.0) * lgf_h)
        cdec_ref[head, 0:t_blk, dh:2 * dh] = jnp.exp((t_blk - tr) * lgb_h)
        cdec_ref[head, 0:t_blk, 2 * dh:3 * dh] = jnp.exp((t_blk - 1.0 - tr) * lgf_h)
        cdec_ref[head, 0:t_blk, 3 * dh:4 * dh] = jnp.exp(tr * lgb_h)
        cdec_ref[head, t_blk:t_blk + 8, 0:dh] = jnp.broadcast_to(jnp.exp(t_blk * lgf_h), (8, dh))
        cdec_ref[head, t_blk:t_blk + 8, dh:2 * dh] = jnp.broadcast_to(jnp.exp(t_blk * lgb_h), (8, dh))

    mask = cmask_ref[head]
    q_dec = cdec_ref[head, 0:t_blk, 0:2 * dh]
    k_dec_f = cdec_ref[head, 0:t_blk, 2 * dh:3 * dh]
    k_dec_b = cdec_ref[head, 0:t_blk, 3 * dh:4 * dh]
    blk_f = cdec_ref[head, t_blk:t_blk + 1, 0:dh]
    blk_b = cdec_ref[head, t_blk:t_blk + 1, dh:2 * dh]
    tn_dims = (((0,), (0,)), ((), ()))
    nt_dims = (((1,), (1,)), ((), ()))
    n_ctx = cq_ref.shape[1] // t_blk
    n_lat = q_ref.shape[1] // t_blk
    n_tot = n_ctx + n_lat
    fwd = slice(0, dh)
    bwd = slice(dh, 2 * dh)

    def rows(i):
        return pl.ds(i * t_blk, t_blk)

    def kv_block(kr, vr, i, j):
        r = rows(i)
        v = vr[0, r, :].astype(F32)
        vv = jnp.concatenate([(v * k_dec_f).astype(BF16), (v * k_dec_b).astype(BF16)], axis=1)
        kv_ref[j] = lax.dot_general(kr[0, r, :], vv, tn_dims, preferred_element_type=F32)

    def score_block(qr, kr, i, j):
        r = rows(i)
        a = lax.dot_general(qr[0, r, :], kr[0, r, :], nt_dims, preferred_element_type=F32) * mask
        sc_ref[j] = a.astype(BF16)

    def out_block(qr, vr, gr, yr, i, j):
        r = rows(i)
        q, v = qr[0, r, :], vr[0, r, :]
        o = jnp.dot(sc_ref[j], v, preferred_element_type=F32)
        inter = jnp.dot(q, st_ref[j], preferred_element_type=F32) * q_dec
        o = o + inter[:, fwd] + inter[:, bwd]
        mu = jnp.mean(o, axis=-1, keepdims=True)
        dlt = o - mu
        var = jnp.mean(dlt * dlt, axis=-1, keepdims=True)
        yr[0, r, :] = (dlt * lax.rsqrt(var + EPS) * gr[0, r, :].astype(F32)).astype(BF16)

    def for_blocks(n, fn):
        for i in range(n):
            fn(i)

    for_blocks(n_ctx, lambda i: kv_block(ck_ref, cv_ref, i, i))
    for_blocks(n_lat, lambda i: kv_block(k_ref, v_ref, i, n_ctx + i))

    def step_f(j, s):
        st_ref[j, :, fwd] = s.astype(BF16)
        return blk_f * s + kv_ref[j, :, fwd]

    def step_b(j, s):
        st_ref[j, :, bwd] = s.astype(BF16)
        return blk_b * s + kv_ref[j, :, bwd]

    zero = jnp.zeros((dh, dh), F32)
    s_f = s_b = zero
    for j in range(n_tot):
        s_f = step_f(j, s_f)
    for j in list(range(n_ctx - 1, -1, -1)) + list(range(n_tot - 1, n_ctx - 1, -1)):
        s_b = step_b(j, s_b)

    if need_ctx:
        for_blocks(n_ctx, lambda i: score_block(cq_ref, ck_ref, i, i))
    for_blocks(n_lat, lambda i: score_block(q_ref, k_ref, i, n_ctx + i))
    if need_ctx:
        for_blocks(n_ctx, lambda i: out_block(cq_ref, cv_ref, cg_ref, cy_ref, i, i))
    else:
        cy_ref[...] = jnp.zeros(cy_ref.shape, cy_ref.dtype)
    for_blocks(n_lat, lambda i: out_block(q_ref, v_ref, g_ref, y_ref, i, n_ctx + i))


def _retention(rd, layer, lat, ctx, need_ctx):
    rq, rk, rv, g = lat
    crq, crk, crv, cg = ctx
    b, seq, _ = rq.shape
    cl = crq.shape[1]
    dh = RET_HEAD_DIM
    n_blocks = (seq + cl) // RET_BLOCK
    head = lambda n: pl.BlockSpec((1, n, dh), lambda i, h: (i, 0, h))
    return pl.pallas_call(
        functools.partial(_ret_kernel, need_ctx=need_ctx),
        out_shape=(jax.ShapeDtypeStruct((b, seq, RET_W), BF16), jax.ShapeDtypeStruct((b, cl, RET_W), BF16)),
        grid=(b, N_RET_HEADS),
        in_specs=[pl.BlockSpec((None, 1, 2, RET_BLOCK), lambda i, h: (layer, h, 0, 0)),
                  head(seq), head(seq), head(seq), head(seq), head(cl), head(cl), head(cl), head(cl)],
        out_specs=(head(seq), head(cl)),
        scratch_shapes=[pltpu.VMEM((n_blocks, dh, 2 * dh), F32), pltpu.VMEM((n_blocks, dh, 2 * dh), BF16),
                        pltpu.VMEM((n_blocks, RET_BLOCK, RET_BLOCK), BF16),
                        pltpu.VMEM((N_RET_HEADS, RET_BLOCK, RET_BLOCK), F32),
                        pltpu.VMEM((N_RET_HEADS, RET_BLOCK + 8, 4 * dh), F32)],
        compiler_params=pltpu.CompilerParams(dimension_semantics=("arbitrary", "arbitrary"),
                                             vmem_limit_bytes=VMEM_LIMIT),
        name="retention_scan",
    )(rd, rq, rk, rv, g, crq, crk, crv, cg)


def _gla_kernel(gn_ref, q_ref, k_ref, v_ref, g_ref, af_ref, ab_ref, cq_ref, ck_ref, cv_ref, cg_ref,
                caf_ref, cab_ref, y_ref, cy_ref, c_ref, kv_ref, dec_ref, st_ref, sc_ref, qin_ref, *, need_ctx):
    t_blk = GLA_BLOCK
    dv = GLA_VAL_DIM
    dk2 = 2 * GLA_KEY_DIM
    assert t_blk == dk2 == dv
    ti = lax.broadcasted_iota(jnp.int32, (t_blk, t_blk), 0)
    si = lax.broadcasted_iota(jnp.int32, (t_blk, t_blk), 1)
    lower = si <= ti
    cum_f = lower.astype(BF16)
    head0 = si < GLA_KEY_DIM
    tn_dims = (((0,), (0,)), ((), ()))
    nt_dims = (((1,), (1,)), ((), ()))

    def rows(i):
        return pl.ds(i * t_blk, t_blk)

    def split2(a):
        hi = a.astype(BF16)
        return [hi, (a - hi.astype(F32)).astype(BF16)]

    dirs = ((GLA_HALF - 1, t_blk - 1), (GLA_HALF, 0))

    def cum_block(ars, i, row0):
        af = ars[0][0, rows(i), :]
        ab = ars[1][0, rows(i), :]
        z = jnp.dot(cum_f, jnp.concatenate(split2(af) + split2(ab), axis=1), preferred_element_type=F32)
        c_ref[0, rows(row0 + i), :] = z[:, 0:dk2] + z[:, dk2:2 * dk2]
        pb = z[:, 2 * dk2:3 * dk2] + z[:, 3 * dk2:4 * dk2]
        c_ref[1, rows(row0 + i), :] = pb[t_blk - 1:t_blk, :] - pb + ab

    def mid_block(qr, kr, vr, i, j, row0, outputs=True):
        r = rows(i)
        cr = rows(row0 + i)
        q = qr[0, r, :].astype(F32)
        k = kr[0, r, :].astype(F32)
        v = vr[0, r, :]
        sc, k_end = [], []
        for d, (mid, edge) in enumerate(dirs):
            c = c_ref[d, cr, :]
            c_mid = c[mid:mid + 1, :]
            c_edge = c[edge:edge + 1, :]
            k_mid = k * jnp.exp2(c_mid - c)
            k_end.append((k_mid * jnp.exp2(c_edge - c_mid)).astype(BF16))
            dec_ref[d, j] = jnp.broadcast_to(jnp.exp2(c_edge), (8, dk2))
            if not outputs:
                continue
            q_mid = q * jnp.exp2(c - c_mid)
            qin_ref[d, cr, :] = (q_mid * jnp.exp2(c_mid)).astype(BF16)
            k_mid_b = k_mid.astype(BF16)
            zero = jnp.zeros_like(k_mid_b)
            k_heads = jnp.concatenate([jnp.where(head0, k_mid_b, zero), jnp.where(head0, zero, k_mid_b)], axis=0)
            sc.append(lax.dot_general(q_mid.astype(BF16), k_heads, nt_dims, preferred_element_type=F32))
        kv = lax.dot_general(v, jnp.concatenate(k_end, axis=1), tn_dims, preferred_element_type=F32)
        for d in range(2):
            kvd = kv[:, d * dk2:(d + 1) * dk2]
            kv_ref[d, j] = jnp.where(head0, kvd[0:dv, :], kvd[dv:2 * dv, :])
        for hd in range(2 if outputs else 0):
            hs = slice(hd * t_blk, (hd + 1) * t_blk)
            sc_ref[hd, j] = jnp.where(lower, sc[0][:, hs], sc[1][:, hs]).astype(BF16)

    def out_block(qr, vr, gr, yr, i, j, row0):
        r = rows(i)
        cr = rows(row0 + i)
        v = vr[0, r, :]
        s_exp = []
        for d in range(2):
            s = st_ref[d, j]
            zero = jnp.zeros_like(s)
            s_exp.append(jnp.concatenate([jnp.where(head0, s, zero), jnp.where(head0, zero, s)], axis=0))
        q_in = jnp.concatenate([qin_ref[0, cr, :], qin_ref[1, cr, :]], axis=1)
        o = jnp.concatenate([jnp.dot(sc_ref[hd, j], v[:, hd * dv:(hd + 1) * dv], preferred_element_type=F32)
                             for hd in range(2)], axis=1)
        o = o + lax.dot_general(q_in, jnp.concatenate(s_exp, axis=1), nt_dims, preferred_element_type=F32)
        gate = gr[0, r, :].astype(F32)
        gn = gn_ref[...]
        for hd in range(2):
            sl = slice(hd * dv, (hd + 1) * dv)
            oh = o[:, sl]
            ms = jnp.mean(oh * oh, axis=-1, keepdims=True)
            yr[0, r, sl] = (oh * lax.rsqrt(ms + EPS) * gn * gate[:, sl]).astype(BF16)

    def for_blocks(n, fn):
        for i in range(n):
            fn(i)

    n_ctx = cq_ref.shape[1] // t_blk
    n_lat = q_ref.shape[1] // t_blk
    n_tot = n_ctx + n_lat
    for_blocks(n_ctx, lambda i: cum_block((caf_ref, cab_ref), i, 0))
    for_blocks(n_lat, lambda i: cum_block((af_ref, ab_ref), i, n_ctx))
    for_blocks(n_ctx, lambda i: mid_block(cq_ref, ck_ref, cv_ref, i, i, 0, outputs=need_ctx))
    for_blocks(n_lat, lambda i: mid_block(q_ref, k_ref, v_ref, i, n_ctx + i, n_ctx))

    def step(d, j, s):
        st_ref[d, j] = s.astype(BF16)
        return dec_ref[d, j, 0:1, :] * s + kv_ref[d, j]

    zero = jnp.zeros((dv, dk2), F32)
    s_f = s_b = zero
    for j in range(n_tot):
        s_f = step(0, j, s_f)
    for j in list(range(n_ctx - 1, -1, -1)) + list(range(n_tot - 1, n_ctx - 1, -1)):
        s_b = step(1, j, s_b)

    if need_ctx:
        for_blocks(n_ctx, lambda i: out_block(cq_ref, cv_ref, cg_ref, cy_ref, i, i, 0))
    else:
        cy_ref[...] = jnp.zeros(cy_ref.shape, cy_ref.dtype)
    for_blocks(n_lat, lambda i: out_block(q_ref, v_ref, g_ref, y_ref, i, n_ctx + i, n_ctx))


def _gla(gn, layer, lat, ctx, need_ctx):
    gq, gk, gv, g, ga = lat
    cgq, cgk, cgv, cg, cga = ctx
    b, seq, _ = gq.shape
    cl = cgq.shape[1]
    pair_k = 2 * GLA_KEY_DIM
    pair_v = 2 * GLA_VAL_DIM
    n_pairs = N_GLA_HEADS // 2
    n_blocks = (seq + cl) // GLA_BLOCK
    gate_off = RET_W // pair_v
    key = lambda n: pl.BlockSpec((1, n, pair_k), lambda i, p: (i, 0, p))
    key_b = lambda n: pl.BlockSpec((1, n, pair_k), lambda i, p: (i, 0, n_pairs + p))
    val = lambda n: pl.BlockSpec((1, n, pair_v), lambda i, p: (i, 0, p))
    mix = lambda n: pl.BlockSpec((1, n, pair_v), lambda i, p: (i, 0, gate_off + p))
    return pl.pallas_call(
        functools.partial(_gla_kernel, need_ctx=need_ctx),
        out_shape=(jax.ShapeDtypeStruct((b, seq, GLA_V), BF16), jax.ShapeDtypeStruct((b, cl, GLA_V), BF16)),
        grid=(b, n_pairs),
        in_specs=[_layer_spec(gn.shape[1:], layer),
                  key(seq), key(seq), val(seq), mix(seq), key(seq), key_b(seq),
                  key(cl), key(cl), val(cl), mix(cl), key(cl), key_b(cl)],
        out_specs=(val(seq), val(cl)),
        scratch_shapes=[pltpu.VMEM((2, seq + cl, pair_k), F32),
                        pltpu.VMEM((2, n_blocks, GLA_VAL_DIM, pair_k), F32),
                        pltpu.VMEM((2, n_blocks, 8, pair_k), F32),
                        pltpu.VMEM((2, n_blocks, GLA_VAL_DIM, pair_k), BF16),
                        pltpu.VMEM((2, n_blocks, GLA_BLOCK, GLA_BLOCK), BF16),
                        pltpu.VMEM((2, seq + cl, pair_k), BF16)],
        compiler_params=pltpu.CompilerParams(dimension_semantics=("parallel", "parallel"),
                                             vmem_limit_bytes=VMEM_LIMIT),
        name="gla_scan",
    )(gn, gq, gk, gv, g, ga, ga, cgq, cgk, cgv, cg, cga, cga)


def _mlp_kernel(x_ref, yr_ref, yg_ref, mod_ref, n2_ref, wo_ref, w1_ref, w2_ref, fg_ref, o_ref, *,
                final, ff_chunk):
    mod = mod_ref[0]
    mix = (jnp.dot(yr_ref[0], wo_ref[0:RET_W, :], preferred_element_type=F32)
           + jnp.dot(yg_ref[0], wo_ref[RET_W:MIX_W, :], preferred_element_type=F32))
    x1 = x_ref[0] + mod[2:3] * mix
    gain = n2_ref[...] * (1.0 + mod[4:5])
    ms = jnp.mean(x1 * x1, axis=-1, keepdims=True)
    hb = (x1 * lax.rsqrt(ms + EPS) * gain + mod[3:4]).astype(BF16)
    acc = jnp.zeros(x1.shape, F32)
    for c in range(w1_ref.shape[1] // ff_chunk):
        sl = slice(c * ff_chunk, (c + 1) * ff_chunk)
        a = jnp.maximum(jnp.dot(hb, w1_ref[:, sl], preferred_element_type=F32), 0.0)
        acc = acc + jnp.dot((a * a).astype(BF16), w2_ref[sl, :], preferred_element_type=F32)
    x2 = x1 + mod[5:6] * acc
    if final:
        ms = jnp.mean(x2 * x2, axis=-1, keepdims=True)
        x2 = x2 * lax.rsqrt(ms + EPS) * fg_ref[...]
    o_ref[0] = x2


def _out_mlp(xa, yr, yg, mod, mod_index, layer, n2, wo, w1, w2, fg, tm, final):
    b, seq, d = xa.shape
    tok = lambda n: pl.BlockSpec((1, tm, n), lambda i, j: (i, j, 0))
    return pl.pallas_call(
        functools.partial(_mlp_kernel, final=final, ff_chunk=1024),
        out_shape=jax.ShapeDtypeStruct(xa.shape, F32),
        grid=(b, seq // tm),
        in_specs=[tok(d), tok(RET_W), tok(GLA_V), pl.BlockSpec((1, 6, d), lambda i, j: (mod_index(i), 0, 0)),
                  _layer_spec(n2.shape[1:], layer), _layer_spec(wo.shape[1:], layer),
                  _layer_spec(w1.shape[1:], layer), _layer_spec(w2.shape[1:], layer), _const_spec((1, d))],
        out_specs=tok(d),
        compiler_params=pltpu.CompilerParams(dimension_semantics=("parallel", "parallel"),
                                             vmem_limit_bytes=VMEM_LIMIT),
        name="out_mlp",
    )(xa, yr, yg, mod, n2, wo, w1, w2, fg)


def kernel(x, c, ctx, c_ctx, ada_w, ada_b, norm1_g, w_in, ret_decay, gla_gate_up, gla_gate_b, gla_norm_g,
           w_out, norm2_g, w_mlp1, w_mlp2, final_g):
    batch, seq, d = x.shape
    cl = ctx.shape[1]
    depth = ada_w.shape[0]
    assert w_in.shape[2] == IN_W
    mod_rows = 16
    cs = jnp.concatenate([c, c_ctx[None, :], jnp.zeros((mod_rows - batch - 1, d), c.dtype)], axis=0)
    mod = _modulation(cs, ada_w, ada_b).reshape(depth * mod_rows, 6, d)

    lat_tables = _rope_tables(seq)
    scale = RET_HEAD_DIM ** -0.5
    ones = jnp.ones((cl, RET_HEAD_DIM), F32)
    ctx_tables = (ones * scale, ones * 0.0, ones, ones * 0.0)

    w_in_b = w_in.astype(BF16)
    zero = jnp.zeros((depth, GLA_GATE_RANK, GLA_K), gla_gate_up.dtype)
    up = jnp.concatenate([jnp.concatenate([gla_gate_up[:, 0], zero], axis=2),
                          jnp.concatenate([zero, gla_gate_up[:, 1]], axis=2)], axis=1).astype(BF16)
    ub = gla_gate_b.reshape(depth, 1, 2 * GLA_K)
    wo = w_out.astype(BF16)
    w1 = w_mlp1.astype(BF16)
    w2 = w_mlp2.astype(BF16)
    n1 = norm1_g.reshape(depth, 1, d)
    n2 = norm2_g.reshape(depth, 1, d)
    gn = gla_norm_g.reshape(depth, 1, GLA_VAL_DIM)
    fg = final_g.reshape(1, d)
    rd = jnp.broadcast_to(jnp.swapaxes(ret_decay, 1, 2)[..., None], (depth, N_RET_HEADS, 2, RET_BLOCK))

    tm = math.gcd(seq, ROW_TILE)
    for layer in range(depth):
        last = layer == depth - 1
        lat_mod = lambda i, layer=layer: layer * mod_rows + i
        ctx_mod = lambda i, layer=layer: layer * mod_rows + batch
        lat = _project(x, mod, lat_mod, layer, n1, w_in_b, up, ub, lat_tables, tm)
        cx = _project(ctx, mod, ctx_mod, layer, n1, w_in_b, up, ub, ctx_tables, cl, states_only=last)
        rq, rk, rv, g, gq, gk, gv, ga = lat
        crq, crk, crv, cg, cgq, cgk, cgv, cga = cx
        yr, cyr = _retention(rd, layer, (rq, rk, rv, g), (crq, crk, crv, cg), not last)
        yg, cyg = _gla(gn, layer, (gq, gk, gv, g, ga), (cgq, cgk, cgv, cg, cga), not last)
        x = _out_mlp(x, yr, yg, mod, lat_mod, layer, n2, wo, w1, w2, fg, tm, last)
        if not last:
            ctx = _out_mlp(ctx, cyr, cyg, mod, ctx_mod, layer, n2, wo, w1, w2, fg, cl, False)
    return x
```

```python
import functools
import math

import jax
import jax.numpy as jnp
from jax import lax
from jax.experimental import pallas as pl
from jax.experimental.pallas import tpu as pltpu

F32 = jnp.float32
BF16 = jnp.bfloat16

GRID_W = 64
N_RET_HEADS = 4
RET_HEAD_DIM = 128
N_GLA_HEADS = 4
GLA_KEY_DIM = 64
GLA_VAL_DIM = 128
GLA_GATE_RANK = 16
GLA_GATE_NORM = 16.0
ROPE_BASE = 10000.0
EPS = 1e-6

RET_W = N_RET_HEADS * RET_HEAD_DIM
GLA_K = N_GLA_HEADS * GLA_KEY_DIM
GLA_V = N_GLA_HEADS * GLA_VAL_DIM
MIX_W = RET_W + GLA_V

OFF_RV = 2 * RET_W
OFF_RG = 3 * RET_W
OFF_GQ = 4 * RET_W
OFF_GK = OFF_GQ + GLA_K
OFF_GV = OFF_GK + GLA_K
OFF_GG = OFF_GV + GLA_V
OFF_D = OFF_GG + GLA_V
IN_W = OFF_D + 2 * GLA_GATE_RANK

RET_BLOCK = 256
GLA_BLOCK = 128
GLA_HALF = GLA_BLOCK // 2

ROW_TILE = 1024
LOG2E = 1.4426950408889634

VMEM_LIMIT = 56 * 1024 * 1024


def _layer_spec(shape, layer):
    zeros = (0,) * len(shape)
    return pl.BlockSpec((None,) + tuple(shape), lambda *_: (layer,) + zeros, pipeline_mode=pl.Buffered(1))


def _const_spec(shape):
    zeros = (0,) * len(shape)
    return pl.BlockSpec(shape, lambda *_: zeros, pipeline_mode=pl.Buffered(1))


def _mod_kernel(cs_ref, w_ref, b_ref, o_ref):
    cs = cs_ref[...]
    s = cs * jax.nn.sigmoid(cs)
    o_ref[0] = jnp.dot(s.astype(BF16), w_ref[0].astype(BF16), preferred_element_type=F32) + b_ref[0]


def _modulation(cs, ada_w, ada_b):
    depth, d, n = ada_w.shape
    rows = cs.shape[0]
    tn = 1024
    return pl.pallas_call(
        _mod_kernel,
        out_shape=jax.ShapeDtypeStruct((depth, rows, n), F32),
        grid=(depth, n // tn),
        in_specs=[
            pl.BlockSpec((rows, d), lambda l, j: (0, 0)),
            pl.BlockSpec((1, d, tn), lambda l, j: (l, 0, j)),
            pl.BlockSpec((1, 1, tn), lambda l, j: (l, 0, j)),
        ],
        out_specs=pl.BlockSpec((1, rows, tn), lambda l, j: (l, 0, j)),
        compiler_params=pltpu.CompilerParams(dimension_semantics=("parallel", "parallel"),
                                             vmem_limit_bytes=VMEM_LIMIT),
        name="adaln_modulation",
    )(cs, ada_w, ada_b.reshape(depth, 1, n))


_QUARTER = RET_HEAD_DIM // 4


def _rope_kernel(cq_ref, sq_ref, ck_ref, sk_ref):
    shape = (GRID_W, RET_HEAD_DIM)
    p = lax.broadcasted_iota(jnp.int32, shape, 0).astype(F32)
    lane = lax.broadcasted_iota(jnp.int32, shape, 1)
    freq = (lane & (_QUARTER - 1)).astype(F32)
    ang = p * jnp.exp(freq * (-math.log(ROPE_BASE) / _QUARTER))
    cos = jnp.cos(ang)
    sin = jnp.where((lane & (2 * _QUARTER - 1)) < _QUARTER, -jnp.sin(ang), jnp.sin(ang))
    by_row = lane < 2 * _QUARTER
    scale = RET_HEAD_DIM ** -0.5
    for r in range(cq_ref.shape[0] // GRID_W):
        c_blk = jnp.where(by_row, cos[r:r + 1, :], cos)
        s_blk = jnp.where(by_row, sin[r:r + 1, :], sin)
        rs = slice(r * GRID_W, (r + 1) * GRID_W)
        cq_ref[rs, :] = c_blk * scale
        sq_ref[rs, :] = s_blk * scale
        ck_ref[rs, :] = c_blk
        sk_ref[rs, :] = s_blk


def _rope_tables(seq):
    sds = jax.ShapeDtypeStruct((seq, RET_HEAD_DIM), F32)
    return pl.pallas_call(_rope_kernel, out_shape=(sds, sds, sds, sds), name="rope_tables")()


def _silu(z):
    hz = 0.5 * z
    return hz * (1.0 + jnp.tanh(hz))


def _proj_kernel(x_ref, mod_ref, n1_ref, w_ref, up_ref, ub_ref, cq_ref, sq_ref, ck_ref, sk_ref,
                 rq_ref, rk_ref, rv_ref, g_ref, gq_ref, gk_ref, gv_ref, ga_ref, *, states_only):
    x = x_ref[0]
    mod = mod_ref[0]
    gain = n1_ref[...] * (1.0 + mod[1:2])
    ms = jnp.mean(x * x, axis=-1, keepdims=True)
    hb = (x * lax.rsqrt(ms + EPS) * gain + mod[0:1]).astype(BF16)

    def mm(off, n):
        return jnp.dot(hb, w_ref[:, off:off + n], preferred_element_type=F32)

    lane = lax.broadcasted_iota(jnp.int32, (x.shape[0], RET_HEAD_DIM), 1)
    first = (lane & (2 * _QUARTER - 1)) < _QUARTER

    def rotary(z, c_ref, s_ref, o_ref):
        c = c_ref[...]
        s = s_ref[...]
        for hd in range(N_RET_HEADS):
            sl = slice(hd * RET_HEAD_DIM, (hd + 1) * RET_HEAD_DIM)
            zh = z[:, sl]
            partner = jnp.where(first, pltpu.roll(zh, RET_HEAD_DIM - _QUARTER, axis=1),
                                pltpu.roll(zh, _QUARTER, axis=1))
            o_ref[0, :, sl] = (zh * c + partner * s).astype(BF16)

    d = mm(OFF_D, 2 * GLA_GATE_RANK).astype(BF16)
    zg = jnp.dot(d, up_ref[...], preferred_element_type=F32) + ub_ref[...]
    soft = jnp.log2(1.0 + jnp.exp2(jnp.abs(zg) * -LOG2E))
    ga_ref[0] = jnp.minimum(zg, 0.0) * (LOG2E / GLA_GATE_NORM) - soft * (1.0 / GLA_GATE_NORM)
    rv_ref[0] = mm(OFF_RV, RET_W).astype(BF16)
    gk_ref[0] = mm(OFF_GK, GLA_K).astype(BF16)
    rotary(mm(RET_W, RET_W), ck_ref, sk_ref, rk_ref)
    gv_ref[0] = mm(OFF_GV, GLA_V).astype(BF16)
    if states_only:
        for ref in (rq_ref, g_ref, gq_ref):
            ref[...] = jnp.zeros(ref.shape, ref.dtype)
        return
    rotary(mm(0, RET_W), cq_ref, sq_ref, rq_ref)
    g_ref[0, :, 0:RET_W] = _silu(mm(OFF_RG, RET_W)).astype(BF16)
    gq_ref[0] = (mm(OFF_GQ, GLA_K) * (GLA_KEY_DIM ** -0.5)).astype(BF16)
    g_ref[0, :, RET_W:MIX_W] = _silu(mm(OFF_GG, GLA_V)).astype(BF16)


def _project(xa, mod, mod_index, layer, n1, w, up, ub, tables, tm, states_only=False):
    b, seq, d = xa.shape
    tok = lambda n: pl.BlockSpec((1, tm, n), lambda i, j: (i, j, 0))
    tab = pl.BlockSpec((tm, RET_HEAD_DIM), lambda i, j: (j, 0))
    sd = lambda n, dt: jax.ShapeDtypeStruct((b, seq, n), dt)
    return pl.pallas_call(
        functools.partial(_proj_kernel, states_only=states_only),
        out_shape=(sd(RET_W, BF16), sd(RET_W, BF16), sd(RET_W, BF16), sd(MIX_W, BF16),
                   sd(GLA_K, BF16), sd(GLA_K, BF16), sd(GLA_V, BF16), sd(2 * GLA_K, F32)),
        grid=(b, seq // tm),
        in_specs=[tok(d), pl.BlockSpec((1, 6, d), lambda i, j: (mod_index(i), 0, 0)),
                  _layer_spec(n1.shape[1:], layer),
                  _layer_spec(w.shape[1:], layer), _layer_spec(up.shape[1:], layer),
                  _layer_spec(ub.shape[1:], layer), tab, tab, tab, tab],
        out_specs=(tok(RET_W), tok(RET_W), tok(RET_W), tok(MIX_W), tok(GLA_K), tok(GLA_K), tok(GLA_V),
                   tok(2 * GLA_K)),
        compiler_params=pltpu.CompilerParams(dimension_semantics=("parallel", "parallel"),
                                             vmem_limit_bytes=VMEM_LIMIT),
        name="norm_project",
    )(xa, mod, n1, w, up, ub, *tables)


def _ret_kernel(rd_ref, q_ref, k_ref, v_ref, g_ref, cq_ref, ck_ref, cv_ref, cg_ref,
                y_ref, cy_ref, kv_ref, st_ref, sc_ref, cmask_ref, cdec_ref, *, need_ctx):
    t_blk = RET_BLOCK
    dh = RET_HEAD_DIM
    head = pl.program_id(1)

    @pl.when(pl.program_id(0) == 0)
    def _():
        lg = jnp.log1p(-jnp.exp(rd_ref[0]))
        lgf, lgb = lg[0:1, :], lg[1:2, :]
        lgf_h, lgb_h = lgf[:, :dh], lgb[:, :dh]
        ti = lax.broadcasted_iota(jnp.int32, (t_blk, t_blk), 0)
        si = lax.broadcasted_iota(jnp.int32, (t_blk, t_blk), 1)
        diff = (ti - si).astype(F32)
        cmask_ref[head] = jnp.exp(jnp.where(diff >= 0, diff * lgf, -diff * lgb))
        tr = lax.broadcasted_iota(jnp.int32, (t_blk, dh), 0).astype(F32)
        cdec_ref[head, 0:t_blk, 0:dh] = jnp.exp((tr + 1.0) * lgf_h)
        cdec_ref[head, 0:t_blk, dh:2 * dh] = jnp.exp((t_blk - tr) * lgb_h)
        cdec_ref[head, 0:t_blk, 2 * dh:3 * dh] = jnp.exp((t_blk - 1.0 - tr) * lgf_h)
        cdec_ref[head, 0:t_blk, 3 * dh:4 * dh] = jnp.exp(tr * lgb_h)
        cdec_ref[head, t_blk:t_blk + 8, 0:dh] = jnp.broadcast_to(jnp.exp(t_blk * lgf_h), (8, dh))
        cdec_ref[head, t_blk:t_blk + 8, dh:2 * dh] = jnp.broadcast_to(jnp.exp(t_blk * lgb_h), (8, dh))

    mask = cmask_ref[head]
    q_dec = cdec_ref[head, 0:t_blk, 0:2 * dh]
    k_dec_f = cdec_ref[head, 0:t_blk, 2 * dh:3 * dh]
    k_dec_b = cdec_ref[head, 0:t_blk, 3 * dh:4 * dh]
    blk_f = cdec_ref[head, t_blk:t_blk + 1, 0:dh]
    blk_b = cdec_ref[head, t_blk:t_blk + 1, dh:2 * dh]
    tn_dims = (((0,), (0,)), ((), ()))
    nt_dims = (((1,), (1,)), ((), ()))
    n_ctx = cq_ref.shape[1] // t_blk
    n_lat = q_ref.shape[1] // t_blk
    n_tot = n_ctx + n_lat
    fwd = slice(0, dh)
    bwd = slice(dh, 2 * dh)

    def rows(i):
        return pl.ds(i * t_blk, t_blk)

    def kv_block(kr, vr, i, j):
        r = rows(i)
        v = vr[0, r, :].astype(F32)
        vv = jnp.concatenate([(v * k_dec_f).astype(BF16), (v * k_dec_b).astype(BF16)], axis=1)
        kv_ref[j] = lax.dot_general(kr[0, r, :], vv, tn_dims, preferred_element_type=F32)

    def score_block(qr, kr, i, j):
        r = rows(i)
        a = lax.dot_general(qr[0, r, :], kr[0, r, :], nt_dims, preferred_element_type=F32) * mask
        sc_ref[j] = a.astype(BF16)

    def out_block(qr, vr, gr, yr, i, j):
        r = rows(i)
        q, v = qr[0, r, :], vr[0, r, :]
        o = jnp.dot(sc_ref[j], v, preferred_element_type=F32)
        inter = jnp.dot(q, st_ref[j], preferred_element_type=F32) * q_dec
        o = o + inter[:, fwd] + inter[:, bwd]
        mu = jnp.mean(o, axis=-1, keepdims=True)
        dlt = o - mu
        var = jnp.mean(dlt * dlt, axis=-1, keepdims=True)
        yr[0, r, :] = (dlt * lax.rsqrt(var + EPS) * gr[0, r, :].astype(F32)).astype(BF16)

    def for_blocks(n, fn):
        for i in range(n):
            fn(i)

    for_blocks(n_ctx, lambda i: kv_block(ck_ref, cv_ref, i, i))
    for_blocks(n_lat, lambda i: kv_block(k_ref, v_ref, i, n_ctx + i))

    def step_f(j, s):
        st_ref[j, :, fwd] = s.astype(BF16)
        return blk_f * s + kv_ref[j, :, fwd]

    def step_b(j, s):
        st_ref[j, :, bwd] = s.astype(BF16)
        return blk_b * s + kv_ref[j, :, bwd]

    zero = jnp.zeros((dh, dh), F32)
    s_f = s_b = zero
    for j in range(n_tot):
        s_f = step_f(j, s_f)
    for j in list(range(n_ctx - 1, -1, -1)) + list(range(n_tot - 1, n_ctx - 1, -1)):
        s_b = step_b(j, s_b)

    if need_ctx:
        for_blocks(n_ctx, lambda i: score_block(cq_ref, ck_ref, i, i))
    for_blocks(n_lat, lambda i: score_block(q_ref, k_ref, i, n_ctx + i))
    if need_ctx:
        for_blocks(n_ctx, lambda i: out_block(cq_ref, cv_ref, cg_ref, cy_ref, i, i))
    else:
        cy_ref[...] = jnp.zeros(cy_ref.shape, cy_ref.dtype)
    for_blocks(n_lat, lambda i: out_block(q_ref, v_ref, g_ref, y_ref, i, n_ctx + i))


def _retention(rd, layer, lat, ctx, need_ctx):
    rq, rk, rv, g = lat
    crq, crk, crv, cg = ctx
    b, seq, _ = rq.shape
    cl = crq.shape[1]
    dh = RET_HEAD_DIM
    n_blocks = (seq + cl) // RET_BLOCK
    head = lambda n: pl.BlockSpec((1, n, dh), lambda i, h: (i, 0, h))
    return pl.pallas_call(
        functools.partial(_ret_kernel, need_ctx=need_ctx),
        out_shape=(jax.ShapeDtypeStruct((b, seq, RET_W), BF16), jax.ShapeDtypeStruct((b, cl, RET_W), BF16)),
        grid=(b, N_RET_HEADS),
        in_specs=[pl.BlockSpec((None, 1, 2, RET_BLOCK), lambda i, h: (layer, h, 0, 0)),
                  head(seq), head(seq), head(seq), head(seq), head(cl), head(cl), head(cl), head(cl)],
        out_specs=(head(seq), head(cl)),
        scratch_shapes=[pltpu.VMEM((n_blocks, dh, 2 * dh), F32), pltpu.VMEM((n_blocks, dh, 2 * dh), BF16),
                        pltpu.VMEM((n_blocks, RET_BLOCK, RET_BLOCK), BF16),
                        pltpu.VMEM((N_RET_HEADS, RET_BLOCK, RET_BLOCK), F32),
                        pltpu.VMEM((N_RET_HEADS, RET_BLOCK + 8, 4 * dh), F32)],
        compiler_params=pltpu.CompilerParams(dimension_semantics=("arbitrary", "arbitrary"),
                                             vmem_limit_bytes=VMEM_LIMIT),
        name="retention_scan",
    )(rd, rq, rk, rv, g, crq, crk, crv, cg)


def _gla_kernel(gn_ref, q_ref, k_ref, v_ref, g_ref, af_ref, ab_ref, cq_ref, ck_ref, cv_ref, cg_ref,
                caf_ref, cab_ref, y_ref, cy_ref, c_ref, kv_ref, dec_ref, st_ref, sc_ref, qin_ref, *, need_ctx):
    t_blk = GLA_BLOCK
    dv = GLA_VAL_DIM
    dk2 = 2 * GLA_KEY_DIM
    assert t_blk == dk2 == dv
    ti = lax.broadcasted_iota(jnp.int32, (t_blk, t_blk), 0)
    si = lax.broadcasted_iota(jnp.int32, (t_blk, t_blk), 1)
    lower = si <= ti
    cum_f = lower.astype(BF16)
    head0 = si < GLA_KEY_DIM
    tn_dims = (((0,), (0,)), ((), ()))
    nt_dims = (((1,), (1,)), ((), ()))

    def rows(i):
        return pl.ds(i * t_blk, t_blk)

    def split2(a):
        hi = a.astype(BF16)
        return [hi, (a - hi.astype(F32)).astype(BF16)]

    dirs = ((GLA_HALF - 1, t_blk - 1), (GLA_HALF, 0))

    def cum_block(ars, i, row0):
        af = ars[0][0, rows(i), :]
        ab = ars[1][0, rows(i), :]
        z = jnp.dot(cum_f, jnp.concatenate(split2(af) + split2(ab), axis=1), preferred_element_type=F32)
        c_ref[0, rows(row0 + i), :] = z[:, 0:dk2] + z[:, dk2:2 * dk2]
        pb = z[:, 2 * dk2:3 * dk2] + z[:, 3 * dk2:4 * dk2]
        c_ref[1, rows(row0 + i), :] = pb[t_blk - 1:t_blk, :] - pb + ab

    def mid_block(qr, kr, vr, i, j, row0, outputs=True):
        r = rows(i)
        cr = rows(row0 + i)
        q = qr[0, r, :].astype(F32)
        k = kr[0, r, :].astype(F32)
        v = vr[0, r, :]
        sc, k_end = [], []
        for d, (mid, edge) in enumerate(dirs):
            c = c_ref[d, cr, :]
            c_mid = c[mid:mid + 1, :]
            c_edge = c[edge:edge + 1, :]
            k_mid = k * jnp.exp2(c_mid - c)
            k_end.append((k_mid * jnp.exp2(c_edge - c_mid)).astype(BF16))
            dec_ref[d, j] = jnp.broadcast_to(jnp.exp2(c_edge), (8, dk2))
            if not outputs:
                continue
            q_mid = q * jnp.exp2(c - c_mid)
            qin_ref[d, cr, :] = (q_mid * jnp.exp2(c_mid)).astype(BF16)
            k_mid_b = k_mid.astype(BF16)
            zero = jnp.zeros_like(k_mid_b)
            k_heads = jnp.concatenate([jnp.where(head0, k_mid_b, zero), jnp.where(head0, zero, k_mid_b)], axis=0)
            sc.append(lax.dot_general(q_mid.astype(BF16), k_heads, nt_dims, preferred_element_type=F32))
        kv = lax.dot_general(v, jnp.concatenate(k_end, axis=1), tn_dims, preferred_element_type=F32)
        for d in range(2):
            kvd = kv[:, d * dk2:(d + 1) * dk2]
            kv_ref[d, j] = jnp.where(head0, kvd[0:dv, :], kvd[dv:2 * dv, :])
        for hd in range(2 if outputs else 0):
            hs = slice(hd * t_blk, (hd + 1) * t_blk)
            sc_ref[hd, j] = jnp.where(lower, sc[0][:, hs], sc[1][:, hs]).astype(BF16)

    def out_block(qr, vr, gr, yr, i, j, row0):
        r = rows(i)
        cr = rows(row0 + i)
        v = vr[0, r, :]
        s_exp = []
        for d in range(2):
            s = st_ref[d, j]
            zero = jnp.zeros_like(s)
            s_exp.append(jnp.concatenate([jnp.where(head0, s, zero), jnp.where(head0, zero, s)], axis=0))
        q_in = jnp.concatenate([qin_ref[0, cr, :], qin_ref[1, cr, :]], axis=1)
        o = jnp.concatenate([jnp.dot(sc_ref[hd, j], v[:, hd * dv:(hd + 1) * dv], preferred_element_type=F32)
                             for hd in range(2)], axis=1)
        o = o + lax.dot_general(q_in, jnp.concatenate(s_exp, axis=1), nt_dims, preferred_element_type=F32)
        gate = gr[0, r, :].astype(F32)
        gn = gn_ref[...]
        for hd in range(2):
            sl = slice(hd * dv, (hd + 1) * dv)
            oh = o[:, sl]
            ms = jnp.mean(oh * oh, axis=-1, keepdims=True)
            yr[0, r, sl] = (oh * lax.rsqrt(ms + EPS) * gn * gate[:, sl]).astype(BF16)

    def for_blocks(n, fn):
        for i in range(n):
            fn(i)

    n_ctx = cq_ref.shape[1] // t_blk
    n_lat = q_ref.shape[1] // t_blk
    n_tot = n_ctx + n_lat
    for_blocks(n_ctx, lambda i: cum_block((caf_ref, cab_ref), i, 0))
    for_blocks(n_lat, lambda i: cum_block((af_ref, ab_ref), i, n_ctx))
    for_blocks(n_ctx, lambda i: mid_block(cq_ref, ck_ref, cv_ref, i, i, 0, outputs=need_ctx))
    for_blocks(n_lat, lambda i: mid_block(q_ref, k_ref, v_ref, i, n_ctx + i, n_ctx))

    def step(d, j, s):
        st_ref[d, j] = s.astype(BF16)
        return dec_ref[d, j, 0:1, :] * s + kv_ref[d, j]

    zero = jnp.zeros((dv, dk2), F32)
    lax.fori_loop(0, n_tot, lambda j, s: step(0, j, s), zero)
    s_b = lax.fori_loop(0, n_ctx, lambda t, s: step(1, n_ctx - 1 - t, s), zero)
    lax.fori_loop(0, n_lat, lambda t, s: step(1, n_tot - 1 - t, s), s_b)

    if need_ctx:
        for_blocks(n_ctx, lambda i: out_block(cq_ref, cv_ref, cg_ref, cy_ref, i, i, 0))
    else:
        cy_ref[...] = jnp.zeros(cy_ref.shape, cy_ref.dtype)
    for_blocks(n_lat, lambda i: out_block(q_ref, v_ref, g_ref, y_ref, i, n_ctx + i, n_ctx))


def _gla(gn, layer, lat, ctx, need_ctx):
    gq, gk, gv, g, ga = lat
    cgq, cgk, cgv, cg, cga = ctx
    b, seq, _ = gq.shape
    cl = cgq.shape[1]
    pair_k = 2 * GLA_KEY_DIM
    pair_v = 2 * GLA_VAL_DIM
    n_pairs = N_GLA_HEADS // 2
    n_blocks = (seq + cl) // GLA_BLOCK
    gate_off = RET_W // pair_v
    key = lambda n: pl.BlockSpec((1, n, pair_k), lambda i, p: (i, 0, p))
    key_b = lambda n: pl.BlockSpec((1, n, pair_k), lambda i, p: (i, 0, n_pairs + p))
    val = lambda n: pl.BlockSpec((1, n, pair_v), lambda i, p: (i, 0, p))
    mix = lambda n: pl.BlockSpec((1, n, pair_v), lambda i, p: (i, 0, gate_off + p))
    return pl.pallas_call(
        functools.partial(_gla_kernel, need_ctx=need_ctx),
        out_shape=(jax.ShapeDtypeStruct((b, seq, GLA_V), BF16), jax.ShapeDtypeStruct((b, cl, GLA_V), BF16)),
        grid=(b, n_pairs),
        in_specs=[_layer_spec(gn.shape[1:], layer),
                  key(seq), key(seq), val(seq), mix(seq), key(seq), key_b(seq),
                  key(cl), key(cl), val(cl), mix(cl), key(cl), key_b(cl)],
        out_specs=(val(seq), val(cl)),
        scratch_shapes=[pltpu.VMEM((2, seq + cl, pair_k), F32),
                        pltpu.VMEM((2, n_blocks, GLA_VAL_DIM, pair_k), F32),
                        pltpu.VMEM((2, n_blocks, 8, pair_k), F32),
                        pltpu.VMEM((2, n_blocks, GLA_VAL_DIM, pair_k), BF16),
                        pltpu.VMEM((2, n_blocks, GLA_BLOCK, GLA_BLOCK), BF16),
                        pltpu.VMEM((2, seq + cl, pair_k), BF16)],
        compiler_params=pltpu.CompilerParams(dimension_semantics=("parallel", "parallel"),
                                             vmem_limit_bytes=VMEM_LIMIT),
        name="gla_scan",
    )(gn, gq, gk, gv, g, ga, ga, cgq, cgk, cgv, cg, cga, cga)


def _mlp_kernel(x_ref, yr_ref, yg_ref, mod_ref, n2_ref, wo_ref, w1_ref, w2_ref, fg_ref, o_ref, *,
                final, ff_chunk):
    mod = mod_ref[0]
    mix = (jnp.dot(yr_ref[0], wo_ref[0:RET_W, :], preferred_element_type=F32)
           + jnp.dot(yg_ref[0], wo_ref[RET_W:MIX_W, :], preferred_element_type=F32))
    x1 = x_ref[0] + mod[2:3] * mix
    gain = n2_ref[...] * (1.0 + mod[4:5])
    ms = jnp.mean(x1 * x1, axis=-1, keepdims=True)
    hb = (x1 * lax.rsqrt(ms + EPS) * gain + mod[3:4]).astype(BF16)
    acc = jnp.zeros(x1.shape, F32)
    for c in range(w1_ref.shape[1] // ff_chunk):
        sl = slice(c * ff_chunk, (c + 1) * ff_chunk)
        a = jnp.maximum(jnp.dot(hb, w1_ref[:, sl], preferred_element_type=F32), 0.0)
        acc = acc + jnp.dot((a * a).astype(BF16), w2_ref[sl, :], preferred_element_type=F32)
    x2 = x1 + mod[5:6] * acc
    if final:
        ms = jnp.mean(x2 * x2, axis=-1, keepdims=True)
        x2 = x2 * lax.rsqrt(ms + EPS) * fg_ref[...]
    o_ref[0] = x2


def _out_mlp(xa, yr, yg, mod, mod_index, layer, n2, wo, w1, w2, fg, tm, final):
    b, seq, d = xa.shape
    tok = lambda n: pl.BlockSpec((1, tm, n), lambda i, j: (i, j, 0))
    return pl.pallas_call(
        functools.partial(_mlp_kernel, final=final, ff_chunk=1024),
        out_shape=jax.ShapeDtypeStruct(xa.shape, F32),
        grid=(b, seq // tm),
        in_specs=[tok(d), tok(RET_W), tok(GLA_V), pl.BlockSpec((1, 6, d), lambda i, j: (mod_index(i), 0, 0)),
                  _layer_spec(n2.shape[1:], layer), _layer_spec(wo.shape[1:], layer),
                  _layer_spec(w1.shape[1:], layer), _layer_spec(w2.shape[1:], layer), _const_spec((1, d))],
        out_specs=tok(d),
        compiler_params=pltpu.CompilerParams(dimension_semantics=("parallel", "parallel"),
                                             vmem_limit_bytes=VMEM_LIMIT),
        name="out_mlp",
    )(xa, yr, yg, mod, n2, wo, w1, w2, fg)


def kernel(x, c, ctx, c_ctx, ada_w, ada_b, norm1_g, w_in, ret_decay, gla_gate_up, gla_gate_b, gla_norm_g,
           w_out, norm2_g, w_mlp1, w_mlp2, final_g):
    batch, seq, d = x.shape
    cl = ctx.shape[1]
    depth = ada_w.shape[0]
    assert w_in.shape[2] == IN_W
    mod_rows = 16
    cs = jnp.concatenate([c, c_ctx[None, :], jnp.zeros((mod_rows - batch - 1, d), c.dtype)], axis=0)
    mod = _modulation(cs, ada_w, ada_b).reshape(depth * mod_rows, 6, d)

    lat_tables = _rope_tables(seq)
    scale = RET_HEAD_DIM ** -0.5
    ones = jnp.ones((cl, RET_HEAD_DIM), F32)
    ctx_tables = (ones * scale, ones * 0.0, ones, ones * 0.0)

    w_in_b = w_in.astype(BF16)
    zero = jnp.zeros((depth, GLA_GATE_RANK, GLA_K), gla_gate_up.dtype)
    up = jnp.concatenate([jnp.concatenate([gla_gate_up[:, 0], zero], axis=2),
                          jnp.concatenate([zero, gla_gate_up[:, 1]], axis=2)], axis=1).astype(BF16)
    ub = gla_gate_b.reshape(depth, 1, 2 * GLA_K)
    wo = w_out.astype(BF16)
    w1 = w_mlp1.astype(BF16)
    w2 = w_mlp2.astype(BF16)
    n1 = norm1_g.reshape(depth, 1, d)
    n2 = norm2_g.reshape(depth, 1, d)
    gn = gla_norm_g.reshape(depth, 1, GLA_VAL_DIM)
    fg = final_g.reshape(1, d)
    rd = jnp.broadcast_to(jnp.swapaxes(ret_decay, 1, 2)[..., None], (depth, N_RET_HEADS, 2, RET_BLOCK))

    tm = math.gcd(seq, ROW_TILE)
    for layer in range(depth):
        last = layer == depth - 1
        lat_mod = lambda i, layer=layer: layer * mod_rows + i
        ctx_mod = lambda i, layer=layer: layer * mod_rows + batch
        lat = _project(x, mod, lat_mod, layer, n1, w_in_b, up, ub, lat_tables, tm)
        cx = _project(ctx, mod, ctx_mod, layer, n1, w_in_b, up, ub, ctx_tables, cl, states_only=last)
        rq, rk, rv, g, gq, gk, gv, ga = lat
        crq, crk, crv, cg, cgq, cgk, cgv, cga = cx
        yr, cyr = _retention(rd, layer, (rq, rk, rv, g), (crq, crk, crv, cg), not last)
        yg, cyg = _gla(gn, layer, (gq, gk, gv, g, ga), (cgq, cgk, cgv, cg, cga), not last)
        x = _out_mlp(x, yr, yg, mod, lat_mod, layer, n2, wo, w1, w2, fg, tm, last)
        if not last:
            ctx = _out_mlp(ctx, cyr, cyg, mod, ctx_mod, layer, n2, wo, w1, w2, fg, cl, False)
    return x
```

```python
import functools
import math

import jax
import jax.numpy as jnp
from jax import lax
from jax.experimental import pallas as pl
from jax.experimental.pallas import tpu as pltpu

F32 = jnp.float32
BF16 = jnp.bfloat16

GRID_W = 64
N_RET_HEADS = 4
RET_HEAD_DIM = 128
N_GLA_HEADS = 4
GLA_KEY_DIM = 64
GLA_VAL_DIM = 128
GLA_GATE_RANK = 16
GLA_GATE_NORM = 16.0
ROPE_BASE = 10000.0
EPS = 1e-6

RET_W = N_RET_HEADS * RET_HEAD_DIM
GLA_K = N_GLA_HEADS * GLA_KEY_DIM
GLA_V = N_GLA_HEADS * GLA_VAL_DIM
MIX_W = RET_W + GLA_V

OFF_RV = 2 * RET_W
OFF_RG = 3 * RET_W
OFF_GQ = 4 * RET_W
OFF_GK = OFF_GQ + GLA_K
OFF_GV = OFF_GK + GLA_K
OFF_GG = OFF_GV + GLA_V
OFF_D = OFF_GG + GLA_V
IN_W = OFF_D + 2 * GLA_GATE_RANK

RET_BLOCK = 256
GLA_BLOCK = 128
GLA_HALF = GLA_BLOCK // 2

ROW_TILE = 1024
LOG2E = 1.4426950408889634

VMEM_LIMIT = 56 * 1024 * 1024


def _layer_spec(shape, layer):
    zeros = (0,) * len(shape)
    return pl.BlockSpec((None,) + tuple(shape), lambda *_: (layer,) + zeros, pipeline_mode=pl.Buffered(1))


def _const_spec(shape):
    zeros = (0,) * len(shape)
    return pl.BlockSpec(shape, lambda *_: zeros, pipeline_mode=pl.Buffered(1))


def _mod_kernel(cs_ref, w_ref, b_ref, o_ref):
    cs = cs_ref[...]
    s = cs * jax.nn.sigmoid(cs)
    o_ref[0] = jnp.dot(s.astype(BF16), w_ref[0].astype(BF16), preferred_element_type=F32) + b_ref[0]


def _modulation(cs, ada_w, ada_b):
    depth, d, n = ada_w.shape
    rows = cs.shape[0]
    tn = 1024
    return pl.pallas_call(
        _mod_kernel,
        out_shape=jax.ShapeDtypeStruct((depth, rows, n), F32),
        grid=(depth, n // tn),
        in_specs=[
            pl.BlockSpec((rows, d), lambda l, j: (0, 0)),
            pl.BlockSpec((1, d, tn), lambda l, j: (l, 0, j)),
            pl.BlockSpec((1, 1, tn), lambda l, j: (l, 0, j)),
        ],
        out_specs=pl.BlockSpec((1, rows, tn), lambda l, j: (l, 0, j)),
        compiler_params=pltpu.CompilerParams(dimension_semantics=("parallel", "parallel"),
                                             vmem_limit_bytes=VMEM_LIMIT),
        name="adaln_modulation",
    )(cs, ada_w, ada_b.reshape(depth, 1, n))


_QUARTER = RET_HEAD_DIM // 4


def _rope_kernel(cq_ref, sq_ref, ck_ref, sk_ref):
    shape = (GRID_W, RET_HEAD_DIM)
    p = lax.broadcasted_iota(jnp.int32, shape, 0).astype(F32)
    lane = lax.broadcasted_iota(jnp.int32, shape, 1)
    freq = (lane & (_QUARTER - 1)).astype(F32)
    ang = p * jnp.exp(freq * (-math.log(ROPE_BASE) / _QUARTER))
    cos = jnp.cos(ang)
    sin = jnp.where((lane & (2 * _QUARTER - 1)) < _QUARTER, -jnp.sin(ang), jnp.sin(ang))
    by_row = lane < 2 * _QUARTER
    scale = RET_HEAD_DIM ** -0.5
    for r in range(cq_ref.shape[0] // GRID_W):
        c_blk = jnp.where(by_row, cos[r:r + 1, :], cos)
        s_blk = jnp.where(by_row, sin[r:r + 1, :], sin)
        rs = slice(r * GRID_W, (r + 1) * GRID_W)
        cq_ref[rs, :] = c_blk * scale
        sq_ref[rs, :] = s_blk * scale
        ck_ref[rs, :] = c_blk
        sk_ref[rs, :] = s_blk


def _rope_tables(seq):
    sds = jax.ShapeDtypeStruct((seq, RET_HEAD_DIM), F32)
    return pl.pallas_call(_rope_kernel, out_shape=(sds, sds, sds, sds), name="rope_tables")()


def _silu(z):
    hz = 0.5 * z
    return hz * (1.0 + jnp.tanh(hz))


def _proj_kernel(x_ref, mod_ref, n1_ref, w_ref, up_ref, ub_ref, cq_ref, sq_ref, ck_ref, sk_ref,
                 rq_ref, rk_ref, rv_ref, g_ref, gq_ref, gk_ref, gv_ref, ga_ref, *, states_only):
    x = x_ref[0]
    mod = mod_ref[0]
    gain = n1_ref[...] * (1.0 + mod[1:2])
    ms = jnp.mean(x * x, axis=-1, keepdims=True)
    hb = (x * lax.rsqrt(ms + EPS) * gain + mod[0:1]).astype(BF16)

    def mm(off, n):
        return jnp.dot(hb, w_ref[:, off:off + n], preferred_element_type=F32)

    lane = lax.broadcasted_iota(jnp.int32, (x.shape[0], RET_HEAD_DIM), 1)
    first = (lane & (2 * _QUARTER - 1)) < _QUARTER

    def rotary(z, c_ref, s_ref, o_ref):
        c = c_ref[...]
        s = s_ref[...]
        for hd in range(N_RET_HEADS):
            sl = slice(hd * RET_HEAD_DIM, (hd + 1) * RET_HEAD_DIM)
            zh = z[:, sl]
            partner = jnp.where(first, pltpu.roll(zh, RET_HEAD_DIM - _QUARTER, axis=1),
                                pltpu.roll(zh, _QUARTER, axis=1))
            o_ref[0, :, sl] = (zh * c + partner * s).astype(BF16)

    d = mm(OFF_D, 2 * GLA_GATE_RANK).astype(BF16)
    zg = jnp.dot(d, up_ref[...], preferred_element_type=F32) + ub_ref[...]
    soft = jnp.log2(1.0 + jnp.exp2(jnp.abs(zg) * -LOG2E))
    ga_ref[0] = jnp.minimum(zg, 0.0) * (LOG2E / GLA_GATE_NORM) - soft * (1.0 / GLA_GATE_NORM)
    rv_ref[0] = mm(OFF_RV, RET_W).astype(BF16)
    gk_ref[0] = mm(OFF_GK, GLA_K).astype(BF16)
    rotary(mm(RET_W, RET_W), ck_ref, sk_ref, rk_ref)
    gv_ref[0] = mm(OFF_GV, GLA_V).astype(BF16)
    if states_only:
        for ref in (rq_ref, g_ref, gq_ref):
            ref[...] = jnp.zeros(ref.shape, ref.dtype)
        return
    rotary(mm(0, RET_W), cq_ref, sq_ref, rq_ref)
    g_ref[0, :, 0:RET_W] = _silu(mm(OFF_RG, RET_W)).astype(BF16)
    gq_ref[0] = (mm(OFF_GQ, GLA_K) * (GLA_KEY_DIM ** -0.5)).astype(BF16)
    g_ref[0, :, RET_W:MIX_W] = _silu(mm(OFF_GG, GLA_V)).astype(BF16)


def _project(xa, mod, mod_index, layer, n1, w, up, ub, tables, tm, states_only=False):
    b, seq, d = xa.shape
    tok = lambda n: pl.BlockSpec((1, tm, n), lambda i, j: (i, j, 0))
    tab = pl.BlockSpec((tm, RET_HEAD_DIM), lambda i, j: (j, 0))
    sd = lambda n, dt: jax.ShapeDtypeStruct((b, seq, n), dt)
    return pl.pallas_call(
        functools.partial(_proj_kernel, states_only=states_only),
        out_shape=(sd(RET_W, BF16), sd(RET_W, BF16), sd(RET_W, BF16), sd(MIX_W, BF16),
                   sd(GLA_K, BF16), sd(GLA_K, BF16), sd(GLA_V, BF16), sd(2 * GLA_K, F32)),
        grid=(b, seq // tm),
        in_specs=[tok(d), pl.BlockSpec((1, 6, d), lambda i, j: (mod_index(i), 0, 0)),
                  _layer_spec(n1.shape[1:], layer),
                  _layer_spec(w.shape[1:], layer), _layer_spec(up.shape[1:], layer),
                  _layer_spec(ub.shape[1:], layer), tab, tab, tab, tab],
        out_specs=(tok(RET_W), tok(RET_W), tok(RET_W), tok(MIX_W), tok(GLA_K), tok(GLA_K), tok(GLA_V),
                   tok(2 * GLA_K)),
        compiler_params=pltpu.CompilerParams(dimension_semantics=("parallel", "parallel"),
                                             vmem_limit_bytes=VMEM_LIMIT),
        name="norm_project",
    )(xa, mod, n1, w, up, ub, *tables)


def _ret_kernel(rd_ref, q_ref, k_ref, v_ref, g_ref, cq_ref, ck_ref, cv_ref, cg_ref,
                y_ref, cy_ref, kv_ref, st_ref, sc_ref, cmask_ref, cdec_ref, *, need_ctx):
    t_blk = RET_BLOCK
    dh = RET_HEAD_DIM
    head = pl.program_id(1)

    @pl.when(pl.program_id(0) == 0)
    def _():
        lg = jnp.log1p(-jnp.exp(rd_ref[0]))
        lgf, lgb = lg[0:1, :], lg[1:2, :]
        lgf_h, lgb_h = lgf[:, :dh], lgb[:, :dh]
        ti = lax.broadcasted_iota(jnp.int32, (t_blk, t_blk), 0)
        si = lax.broadcasted_iota(jnp.int32, (t_blk, t_blk), 1)
        diff = (ti - si).astype(F32)
        cmask_ref[head] = jnp.exp(jnp.where(diff >= 0, diff * lgf, -diff * lgb))
        tr = lax.broadcasted_iota(jnp.int32, (t_blk, dh), 0).astype(F32)
        cdec_ref[head, 0:t_blk, 0:dh] = jnp.exp((tr + 1.0) * lgf_h)
        cdec_ref[head, 0:t_blk, dh:2 * dh] = jnp.exp((t_blk - tr) * lgb_h)
        cdec_ref[head, 0:t_blk, 2 * dh:3 * dh] = jnp.exp((t_blk - 1.0 - tr) * lgf_h)
        cdec_ref[head, 0:t_blk, 3 * dh:4 * dh] = jnp.exp(tr * lgb_h)
        cdec_ref[head, t_blk:t_blk + 8, 0:dh] = jnp.broadcast_to(jnp.exp(t_blk * lgf_h), (8, dh))
        cdec_ref[head, t_blk:t_blk + 8, dh:2 * dh] = jnp.broadcast_to(jnp.exp(t_blk * lgb_h), (8, dh))

    mask = cmask_ref[head]
    q_dec = cdec_ref[head, 0:t_blk, 0:2 * dh]
    k_dec_f = cdec_ref[head, 0:t_blk, 2 * dh:3 * dh]
    k_dec_b = cdec_ref[head, 0:t_blk, 3 * dh:4 * dh]
    blk_f = cdec_ref[head, t_blk:t_blk + 1, 0:dh]
    blk_b = cdec_ref[head, t_blk:t_blk + 1, dh:2 * dh]
    tn_dims = (((0,), (0,)), ((), ()))
    nt_dims = (((1,), (1,)), ((), ()))
    n_ctx = cq_ref.shape[1] // t_blk
    n_lat = q_ref.shape[1] // t_blk
    n_tot = n_ctx + n_lat
    fwd = slice(0, dh)
    bwd = slice(dh, 2 * dh)

    def rows(i):
        return pl.ds(i * t_blk, t_blk)

    def kv_block(kr, vr, i, j):
        r = rows(i)
        v = vr[0, r, :].astype(F32)
        vv = jnp.concatenate([(v * k_dec_f).astype(BF16), (v * k_dec_b).astype(BF16)], axis=1)
        kv_ref[j] = lax.dot_general(kr[0, r, :], vv, tn_dims, preferred_element_type=F32)

    def score_block(qr, kr, i, j):
        r = rows(i)
        a = lax.dot_general(qr[0, r, :], kr[0, r, :], nt_dims, preferred_element_type=F32) * mask
        sc_ref[j] = a.astype(BF16)

    def out_block(qr, vr, gr, yr, i, j):
        r = rows(i)
        q, v = qr[0, r, :], vr[0, r, :]
        o = jnp.dot(sc_ref[j], v, preferred_element_type=F32)
        inter = jnp.dot(q, st_ref[j], preferred_element_type=F32) * q_dec
        o = o + inter[:, fwd] + inter[:, bwd]
        mu = jnp.mean(o, axis=-1, keepdims=True)
        dlt = o - mu
        var = jnp.mean(dlt * dlt, axis=-1, keepdims=True)
        yr[0, r, :] = (dlt * lax.rsqrt(var + EPS) * gr[0, r, :].astype(F32)).astype(BF16)

    def for_blocks(n, fn):
        for i in range(n):
            fn(i)

    for_blocks(n_ctx, lambda i: kv_block(ck_ref, cv_ref, i, i))
    for_blocks(n_lat, lambda i: kv_block(k_ref, v_ref, i, n_ctx + i))

    def step_f(j, s):
        st_ref[j, :, fwd] = s.astype(BF16)
        return blk_f * s + kv_ref[j, :, fwd]

    def step_b(j, s):
        st_ref[j, :, bwd] = s.astype(BF16)
        return blk_b * s + kv_ref[j, :, bwd]

    zero = jnp.zeros((dh, dh), F32)
    s_f = s_b = zero
    for j in range(n_tot):
        s_f = step_f(j, s_f)
    for j in list(range(n_ctx - 1, -1, -1)) + list(range(n_tot - 1, n_ctx - 1, -1)):
        s_b = step_b(j, s_b)

    if need_ctx:
        for_blocks(n_ctx, lambda i: score_block(cq_ref, ck_ref, i, i))
    for_blocks(n_lat, lambda i: score_block(q_ref, k_ref, i, n_ctx + i))
    if need_ctx:
        for_blocks(n_ctx, lambda i: out_block(cq_ref, cv_ref, cg_ref, cy_ref, i, i))
    else:
        cy_ref[...] = jnp.zeros(cy_ref.shape, cy_ref.dtype)
    for_blocks(n_lat, lambda i: out_block(q_ref, v_ref, g_ref, y_ref, i, n_ctx + i))


def _retention(rd, layer, lat, ctx, need_ctx):
    rq, rk, rv, g = lat
    crq, crk, crv, cg = ctx
    b, seq, _ = rq.shape
    cl = crq.shape[1]
    dh = RET_HEAD_DIM
    n_blocks = (seq + cl) // RET_BLOCK
    head = lambda n: pl.BlockSpec((1, n, dh), lambda i, h: (i, 0, h))
    return pl.pallas_call(
        functools.partial(_ret_kernel, need_ctx=need_ctx),
        out_shape=(jax.ShapeDtypeStruct((b, seq, RET_W), BF16), jax.ShapeDtypeStruct((b, cl, RET_W), BF16)),
        grid=(b, N_RET_HEADS),
        in_specs=[pl.BlockSpec((None, 1, 2, RET_BLOCK), lambda i, h: (layer, h, 0, 0)),
                  head(seq), head(seq), head(seq), head(seq), head(cl), head(cl), head(cl), head(cl)],
        out_specs=(head(seq), head(cl)),
        scratch_shapes=[pltpu.VMEM((n_blocks, dh, 2 * dh), F32), pltpu.VMEM((n_blocks, dh, 2 * dh), BF16),
                        pltpu.VMEM((n_blocks, RET_BLOCK, RET_BLOCK), BF16),
                        pltpu.VMEM((N_RET_HEADS, RET_BLOCK, RET_BLOCK), F32),
                        pltpu.VMEM((N_RET_HEADS, RET_BLOCK + 8, 4 * dh), F32)],
        compiler_params=pltpu.CompilerParams(dimension_semantics=("arbitrary", "arbitrary"),
                                             vmem_limit_bytes=VMEM_LIMIT),
        name="retention_scan",
    )(rd, rq, rk, rv, g, crq, crk, crv, cg)


def _gla_kernel(gn_ref, q_ref, k_ref, v_ref, g_ref, af_ref, ab_ref, cq_ref, ck_ref, cv_ref, cg_ref,
                caf_ref, cab_ref, y_ref, cy_ref, c_ref, kv_ref, dec_ref, st_ref, sc_ref, qin_ref, *, need_ctx):
    t_blk = GLA_BLOCK
    dv = GLA_VAL_DIM
    dk2 = 2 * GLA_KEY_DIM
    assert t_blk == dk2 == dv
    ti = lax.broadcasted_iota(jnp.int32, (t_blk, t_blk), 0)
    si = lax.broadcasted_iota(jnp.int32, (t_blk, t_blk), 1)
    lower = si <= ti
    cum_f = lower.astype(BF16)
    head0 = si < GLA_KEY_DIM
    tn_dims = (((0,), (0,)), ((), ()))
    nt_dims = (((1,), (1,)), ((), ()))

    def rows(i):
        return pl.ds(i * t_blk, t_blk)

    def split2(a):
        hi = a.astype(BF16)
        return [hi, (a - hi.astype(F32)).astype(BF16)]

    dirs = ((GLA_HALF - 1, t_blk - 1), (GLA_HALF, 0))

    def cum_block(ars, i, row0):
        af = ars[0][0, rows(i), :]
        ab = ars[1][0, rows(i), :]
        z = jnp.dot(cum_f, jnp.concatenate(split2(af) + split2(ab), axis=1), preferred_element_type=F32)
        c_ref[0, rows(row0 + i), :] = z[:, 0:dk2] + z[:, dk2:2 * dk2]
        pb = z[:, 2 * dk2:3 * dk2] + z[:, 3 * dk2:4 * dk2]
        c_ref[1, rows(row0 + i), :] = pb[t_blk - 1:t_blk, :] - pb + ab

    def mid_block(qr, kr, vr, i, j, row0, outputs=True):
        r = rows(i)
        cr = rows(row0 + i)
        q = qr[0, r, :].astype(F32)
        k = kr[0, r, :].astype(F32)
        v = vr[0, r, :]
        sc, k_end = [], []
        for d, (mid, edge) in enumerate(dirs):
            c = c_ref[d, cr, :]
            c_mid = c[mid:mid + 1, :]
            c_edge = c[edge:edge + 1, :]
            k_mid = k * jnp.exp2(c_mid - c)
            k_end.append((k_mid * jnp.exp2(c_edge - c_mid)).astype(BF16))
            dec_ref[d, j] = jnp.broadcast_to(jnp.exp2(c_edge), (8, dk2))
            if not outputs:
                continue
            q_mid = q * jnp.exp2(c - c_mid)
            qin_ref[d, cr, :] = (q_mid * jnp.exp2(c_mid)).astype(BF16)
            k_mid_b = k_mid.astype(BF16)
            zero = jnp.zeros_like(k_mid_b)
            k_heads = jnp.concatenate([jnp.where(head0, k_mid_b, zero), jnp.where(head0, zero, k_mid_b)], axis=0)
            sc.append(lax.dot_general(q_mid.astype(BF16), k_heads, nt_dims, preferred_element_type=F32))
        kv = lax.dot_general(v, jnp.concatenate(k_end, axis=1), tn_dims, preferred_element_type=F32)
        for d in range(2):
            kvd = kv[:, d * dk2:(d + 1) * dk2]
            kv_ref[d, j] = jnp.where(head0, kvd[0:dv, :], kvd[dv:2 * dv, :])
        for hd in range(2 if outputs else 0):
            hs = slice(hd * t_blk, (hd + 1) * t_blk)
            sc_ref[hd, j] = jnp.where(lower, sc[0][:, hs], sc[1][:, hs]).astype(BF16)

    def out_block(qr, vr, gr, yr, i, j, row0):
        r = rows(i)
        cr = rows(row0 + i)
        v = vr[0, r, :]
        s_exp = []
        for d in range(2):
            s = st_ref[d, j]
            zero = jnp.zeros_like(s)
            s_exp.append(jnp.concatenate([jnp.where(head0, s, zero), jnp.where(head0, zero, s)], axis=0))
        q_in = jnp.concatenate([qin_ref[0, cr, :], qin_ref[1, cr, :]], axis=1)
        o = jnp.concatenate([jnp.dot(sc_ref[hd, j], v[:, hd * dv:(hd + 1) * dv], preferred_element_type=F32)
                             for hd in range(2)], axis=1)
        o = o + lax.dot_general(q_in, jnp.concatenate(s_exp, axis=1), nt_dims, preferred_element_type=F32)
        gate = gr[0, r, :].astype(F32)
        gn = gn_ref[...]
        for hd in range(2):
            sl = slice(hd * dv, (hd + 1) * dv)
            oh = o[:, sl]
            ms = jnp.mean(oh * oh, axis=-1, keepdims=True)
            yr[0, r, sl] = (oh * lax.rsqrt(ms + EPS) * gn * gate[:, sl]).astype(BF16)

    def for_blocks(n, fn):
        for i in range(n):
            fn(i)

    n_ctx = cq_ref.shape[1] // t_blk
    n_lat = q_ref.shape[1] // t_blk
    n_tot = n_ctx + n_lat
    for_blocks(n_ctx, lambda i: cum_block((caf_ref, cab_ref), i, 0))
    for_blocks(n_lat, lambda i: cum_block((af_ref, ab_ref), i, n_ctx))
    for_blocks(n_ctx, lambda i: mid_block(cq_ref, ck_ref, cv_ref, i, i, 0, outputs=need_ctx))
    for_blocks(n_lat, lambda i: mid_block(q_ref, k_ref, v_ref, i, n_ctx + i, n_ctx))

    def step(d, j, s):
        st_ref[d, j] = s.astype(BF16)
        return dec_ref[d, j, 0:1, :] * s + kv_ref[d, j]

    zero = jnp.zeros((dv, dk2), F32)
    lax.fori_loop(0, n_tot, lambda j, s: step(0, j, s), zero)
    s_b = lax.fori_loop(0, n_ctx, lambda t, s: step(1, n_ctx - 1 - t, s), zero)
    lax.fori_loop(0, n_lat, lambda t, s: step(1, n_tot - 1 - t, s), s_b)

    if need_ctx:
        for_blocks(n_ctx, lambda i: out_block(cq_ref, cv_ref, cg_ref, cy_ref, i, i, 0))
    else:
        cy_ref[...] = jnp.zeros(cy_ref.shape, cy_ref.dtype)
    for_blocks(n_lat, lambda i: out_block(q_ref, v_ref, g_ref, y_ref, i, n_ctx + i, n_ctx))


def _gla(gn, layer, lat, ctx, need_ctx):
    gq, gk, gv, g, ga = lat
    cgq, cgk, cgv, cg, cga = ctx
    b, seq, _ = gq.shape
    cl = cgq.shape[1]
    pair_k = 2 * GLA_KEY_DIM
    pair_v = 2 * GLA_VAL_DIM
    n_pairs = N_GLA_HEADS // 2
    n_blocks = (seq + cl) // GLA_BLOCK
    gate_off = RET_W // pair_v
    key = lambda n: pl.BlockSpec((1, n, pair_k), lambda i, p: (i, 0, p))
    key_b = lambda n: pl.BlockSpec((1, n, pair_k), lambda i, p: (i, 0, n_pairs + p))
    val = lambda n: pl.BlockSpec((1, n, pair_v), lambda i, p: (i, 0, p))
    mix = lambda n: pl.BlockSpec((1, n, pair_v), lambda i, p: (i, 0, gate_off + p))
    return pl.pallas_call(
        functools.partial(_gla_kernel, need_ctx=need_ctx),
        out_shape=(jax.ShapeDtypeStruct((b, seq, GLA_V), BF16), jax.ShapeDtypeStruct((b, cl, GLA_V), BF16)),
        grid=(b, n_pairs),
        in_specs=[_layer_spec(gn.shape[1:], layer),
                  key(seq), key(seq), val(seq), mix(seq), key(seq), key_b(seq),
                  key(cl), key(cl), val(cl), mix(cl), key(cl), key_b(cl)],
        out_specs=(val(seq), val(cl)),
        scratch_shapes=[pltpu.VMEM((2, seq + cl, pair_k), F32),
                        pltpu.VMEM((2, n_blocks, GLA_VAL_DIM, pair_k), F32),
                        pltpu.VMEM((2, n_blocks, 8, pair_k), F32),
                        pltpu.VMEM((2, n_blocks, GLA_VAL_DIM, pair_k), BF16),
                        pltpu.VMEM((2, n_blocks, GLA_BLOCK, GLA_BLOCK), BF16),
                        pltpu.VMEM((2, seq + cl, pair_k), BF16)],
        compiler_params=pltpu.CompilerParams(dimension_semantics=("parallel", "parallel"),
                                             vmem_limit_bytes=VMEM_LIMIT),
        name="gla_scan",
    )(gn, gq, gk, gv, g, ga, ga, cgq, cgk, cgv, cg, cga, cga)


def _mlp_kernel(x_ref, yr_ref, yg_ref, mod_ref, n2_ref, wo_ref, w1_ref, w2_ref, fg_ref, o_ref, *,
                final, ff_chunk):
    mod = mod_ref[0]
    mix = (jnp.dot(yr_ref[0], wo_ref[0:RET_W, :], preferred_element_type=F32)
           + jnp.dot(yg_ref[0], wo_ref[RET_W:MIX_W, :], preferred_element_type=F32))
    x1 = x_ref[0] + mod[2:3] * mix
    gain = n2_ref[...] * (1.0 + mod[4:5])
    ms = jnp.mean(x1 * x1, axis=-1, keepdims=True)
    hb = (x1 * lax.rsqrt(ms + EPS) * gain + mod[3:4]).astype(BF16)
    acc = jnp.zeros(x1.shape, F32)
    for c in range(w1_ref.shape[1] // ff_chunk):
        sl = slice(c * ff_chunk, (c + 1) * ff_chunk)
        a = jnp.maximum(jnp.dot(hb, w1_ref[:, sl], preferred_element_type=F32), 0.0)
        acc = acc + jnp.dot((a * a).astype(BF16), w2_ref[sl, :], preferred_element_type=F32)
    x2 = x1 + mod[5:6] * acc
    if final:
        ms = jnp.mean(x2 * x2, axis=-1, keepdims=True)
        x2 = x2 * lax.rsqrt(ms + EPS) * fg_ref[...]
    o_ref[0] = x2


def _out_mlp(xa, yr, yg, mod, mod_index, layer, n2, wo, w1, w2, fg, tm, final):
    b, seq, d = xa.shape
    tok = lambda n: pl.BlockSpec((1, tm, n), lambda i, j: (i, j, 0))
    return pl.pallas_call(
        functools.partial(_mlp_kernel, final=final, ff_chunk=1024),
        out_shape=jax.ShapeDtypeStruct(xa.shape, F32),
        grid=(b, seq // tm),
        in_specs=[tok(d), tok(RET_W), tok(GLA_V), pl.BlockSpec((1, 6, d), lambda i, j: (mod_index(i), 0, 0)),
                  _layer_spec(n2.shape[1:], layer), _layer_spec(wo.shape[1:], layer),
                  _layer_spec(w1.shape[1:], layer), _layer_spec(w2.shape[1:], layer), _const_spec((1, d))],
        out_specs=tok(d),
        compiler_params=pltpu.CompilerParams(dimension_semantics=("parallel", "parallel"),
                                             vmem_limit_bytes=VMEM_LIMIT),
        name="out_mlp",
    )(xa, yr, yg, mod, n2, wo, w1, w2, fg)


def kernel(x, c, ctx, c_ctx, ada_w, ada_b, norm1_g, w_in, ret_decay, gla_gate_up, gla_gate_b, gla_norm_g,
           w_out, norm2_g, w_mlp1, w_mlp2, final_g):
    batch, seq, d = x.shape
    cl = ctx.shape[1]
    depth = ada_w.shape[0]
    assert w_in.shape[2] == IN_W
    mod_rows = 16
    cs = jnp.concatenate([c, c_ctx[None, :], jnp.zeros((mod_rows - batch - 1, d), c.dtype)], axis=0)
    mod = _modulation(cs, ada_w, ada_b).reshape(depth * mod_rows, 6, d)

    lat_tables = _rope_tables(seq)
    scale = RET_HEAD_DIM ** -0.5
    ctx_rows = batch * cl
    ones = jnp.ones((ctx_rows, RET_HEAD_DIM), F32)
    ctx_tables = (ones * scale, ones * 0.0, ones, ones * 0.0)
    ctx = ctx.reshape(1, ctx_rows, d)
    ctm = math.gcd(ctx_rows, ROW_TILE)
    per_batch = lambda a: a.reshape(batch, cl, a.shape[-1])
    flat = lambda a: a.reshape(1, ctx_rows, a.shape[-1])

    w_in_b = w_in.astype(BF16)
    zero = jnp.zeros((depth, GLA_GATE_RANK, GLA_K), gla_gate_up.dtype)
    up = jnp.concatenate([jnp.concatenate([gla_gate_up[:, 0], zero], axis=2),
                          jnp.concatenate([zero, gla_gate_up[:, 1]], axis=2)], axis=1).astype(BF16)
    ub = gla_gate_b.reshape(depth, 1, 2 * GLA_K)
    wo = w_out.astype(BF16)
    w1 = w_mlp1.astype(BF16)
    w2 = w_mlp2.astype(BF16)
    n1 = norm1_g.reshape(depth, 1, d)
    n2 = norm2_g.reshape(depth, 1, d)
    gn = gla_norm_g.reshape(depth, 1, GLA_VAL_DIM)
    fg = final_g.reshape(1, d)
    rd = jnp.broadcast_to(jnp.swapaxes(ret_decay, 1, 2)[..., None], (depth, N_RET_HEADS, 2, RET_BLOCK))

    tm = math.gcd(seq, ROW_TILE)
    for layer in range(depth):
        last = layer == depth - 1
        lat_mod = lambda i, layer=layer: layer * mod_rows + i
        ctx_mod = lambda i, layer=layer: layer * mod_rows + batch
        lat = _project(x, mod, lat_mod, layer, n1, w_in_b, up, ub, lat_tables, tm)
        cx = _project(ctx, mod, ctx_mod, layer, n1, w_in_b, up, ub, ctx_tables, ctm, states_only=last)
        rq, rk, rv, g, gq, gk, gv, ga = lat
        crq, crk, crv, cg, cgq, cgk, cgv, cga = (per_batch(a) for a in cx)
        yr, cyr = _retention(rd, layer, (rq, rk, rv, g), (crq, crk, crv, cg), not last)
        yg, cyg = _gla(gn, layer, (gq, gk, gv, g, ga), (cgq, cgk, cgv, cg, cga), not last)
        x = _out_mlp(x, yr, yg, mod, lat_mod, layer, n2, wo, w1, w2, fg, tm, last)
        if not last:
            ctx = _out_mlp(ctx, flat(cyr), flat(cyg), mod, ctx_mod, layer, n2, wo, w1, w2, fg, ctm, False)
    return x
```
